```python
import jax, jax.numpy as jnp
from jax import lax
import numpy as np

D_MODEL = 1024
BATCH = 16
SEQ = 4096
DEPTH = 2
DEC_BATCH = 8
DEC_SEQ = 64
PAST_LEN = 2048

CHUNK = 64
N_A = DEPTH // 2
N_B = DEPTH - N_A
A_HEADS = 16
Q_LORA = 384
KV_LORA = 256
NOPE_DIM = 64
ROPE_DIM = 32
V_DIM = 64
QK_DIM = NOPE_DIM + ROPE_DIM
ROPE_THETA = 10000.0
Q_BLOCK = 128
B_HEADS = 16
B_HEAD_DIM = D_MODEL // B_HEADS
LEFT_CHUNKS = 8
B_WINDOW = LEFT_CHUNKS * CHUNK
REL_CLIP = 128
P_HEADS = 8
N_KEYS = 128
N_EXPERTS = N_KEYS * N_KEYS
P_DKEY = 128
P_HALF = P_DKEY // 2
P_TOPK = 16
PEER_BLOCK = 128
NEG = -1e30

kernel_name = "yoco_mla_band_peer_stream_step"


def rmsnorm(x, w, eps=1e-6):
    xf = x.astype(jnp.float32)
    y = xf * lax.rsqrt(jnp.mean(xf * xf, axis=-1, keepdims=True) + eps)
    return (y * w.astype(jnp.float32)).astype(x.dtype)


def modulate(x, w, shift, scale):
    return rmsnorm(x, w) * (1 + scale[:, None, :]) + shift[:, None, :]


def rope(x, pos):
    half = ROPE_DIM // 2
    freqs = ROPE_THETA ** (-jnp.arange(half, dtype=jnp.float32) / half)
    ang = pos.astype(jnp.float32)[:, None] * freqs[None, :]
    cos = jnp.cos(ang)[:, None, :].astype(x.dtype)
    sin = jnp.sin(ang)[:, None, :].astype(x.dtype)
    x1, x2 = x[..., :half], x[..., half:]
    return jnp.concatenate([x1 * cos - x2 * sin, x2 * cos + x1 * sin], axis=-1)


def mla_attention(q, k, v, q_pos, k_pos):
    B, T, H, E = q.shape
    qb = min(Q_BLOCK, T)
    nb = T // qb
    scale = E ** -0.5
    k_chunk = k_pos // CHUNK

    def one_block(args):
        qi, pi = args
        s = jnp.einsum('bqhe,bshe->bhqs', qi, k, preferred_element_type=jnp.float32) * scale
        allowed = k_chunk[None, :] <= (pi // CHUNK)[:, None]
        s = jnp.where(allowed[None, None], s, NEG)
        p = jax.nn.softmax(s, axis=-1).astype(v.dtype)
        return jnp.einsum('bhqs,bshd->bqhd', p, v)

    qs = q.reshape(B, nb, qb, H, E).transpose(1, 0, 2, 3, 4)
    ps = q_pos.reshape(nb, qb)
    out = lax.map(one_block, (qs, ps))
    return out.transpose(1, 0, 2, 3, 4).reshape(B, T, H, v.shape[-1])


def mla_mixer(h, pos, past_ckv, past_kr, w_in, q_lora_norm, kv_lora_norm, w_uq, w_ukv, q_norm, k_norm, w_o):
    B, T, _ = h.shape
    proj = h @ w_in
    c_q = rmsnorm(proj[..., :Q_LORA], q_lora_norm)
    c_kv = rmsnorm(proj[..., Q_LORA:Q_LORA + KV_LORA], kv_lora_norm)
    k_r = proj[..., Q_LORA + KV_LORA:]
    if past_ckv is None:
        ckv_all, kr_all, k_pos = c_kv, k_r, pos
    else:
        P = past_ckv.shape[1]
        ckv_all = jnp.concatenate([past_ckv, c_kv], axis=1)
        kr_all = jnp.concatenate([past_kr, k_r], axis=1)
        k_pos = jnp.concatenate([jnp.arange(P, dtype=jnp.int32), pos])
    S = ckv_all.shape[1]
    q = (c_q @ w_uq).reshape(B, T, A_HEADS, QK_DIM)
    kv = (ckv_all @ w_ukv).reshape(B, S, A_HEADS, NOPE_DIM + V_DIM)
    k_nope, v = kv[..., :NOPE_DIM], kv[..., NOPE_DIM:]
    k = jnp.concatenate([k_nope, jnp.broadcast_to(kr_all[:, :, None, :], (B, S, A_HEADS, ROPE_DIM))], axis=-1)
    q = rmsnorm(q, q_norm)
    k = rmsnorm(k, k_norm)
    q = jnp.concatenate([q[..., :NOPE_DIM], rope(q[..., NOPE_DIM:], pos)], axis=-1)
    k = jnp.concatenate([k[..., :NOPE_DIM], rope(k[..., NOPE_DIM:], k_pos)], axis=-1)
    o = mla_attention(q, k, v, pos, k_pos)
    return o.reshape(B, T, A_HEADS * V_DIM) @ w_o, c_kv, k_r


def band_attend(q, k, v, q_pos, k_pos, bias_table):
    scale = q.shape[-1] ** -0.5
    s = jnp.einsum('bthe,bshe->bhts', q, k, preferred_element_type=jnp.float32) * scale
    rel = jnp.clip(q_pos[:, None] - k_pos[None, :], -REL_CLIP, REL_CLIP) + REL_CLIP
    s = s + bias_table[:, rel].astype(jnp.float32)[None]
    qc = q_pos // CHUNK
    kc = k_pos // CHUNK
    allowed = (k_pos[None, :] >= 0) & (kc[None, :] <= qc[:, None]) & (kc[None, :] >= qc[:, None] - LEFT_CHUNKS)
    s = jnp.where(allowed[None, None], s, NEG)
    p = jax.nn.softmax(s, axis=-1).astype(v.dtype)
    return jnp.einsum('bhts,bshe->bthe', p, v)


def band_attention_prompt(q, k, v, bias_table):
    B, T, H, E = q.shape
    nc = T // CHUNK
    band = (LEFT_CHUNKS + 1) * CHUNK
    kp = jnp.pad(k, ((0, 0), (B_WINDOW, 0), (0, 0), (0, 0)))
    vp = jnp.pad(v, ((0, 0), (B_WINDOW, 0), (0, 0), (0, 0)))
    qs = q.reshape(B, nc, CHUNK, H, E).transpose(1, 0, 2, 3, 4)

    def one_chunk(args):
        n, qn = args
        kn = lax.dynamic_slice_in_dim(kp, n * CHUNK, band, axis=1)
        vn = lax.dynamic_slice_in_dim(vp, n * CHUNK, band, axis=1)
        q_pos = n * CHUNK + jnp.arange(CHUNK, dtype=jnp.int32)
        k_pos = n * CHUNK - B_WINDOW + jnp.arange(band, dtype=jnp.int32)
        return band_attend(qn, kn, vn, q_pos, k_pos, bias_table)

    out = lax.map(one_chunk, (jnp.arange(nc, dtype=jnp.int32), qs))
    return out.transpose(1, 0, 2, 3, 4).reshape(B, T, H, E)


def band_mixer(h, pos, k_all, v_all, k_pos, w_q, q_norm, rel_bias, w_o, prompt):
    B, T, _ = h.shape
    q = rmsnorm((h @ w_q).reshape(B, T, B_HEADS, B_HEAD_DIM), q_norm)
    if prompt:
        o = band_attention_prompt(q, k_all, v_all, rel_bias)
    else:
        o = band_attend(q, k_all, v_all, pos, k_pos, rel_bias)
    return o.reshape(B, T, B_HEADS * B_HEAD_DIM) @ w_o


def shared_kv(x, c, kv_ada_w, kv_ada_b, kv_norm_w, w_kv, k_norm):
    B, T, _ = x.shape
    mod = jax.nn.silu(c) @ kv_ada_w + kv_ada_b
    shift, scale = jnp.split(mod, 2, axis=-1)
    h = modulate(x, kv_norm_w, shift, scale)
    kv = (h @ w_kv).reshape(B, T, 2, B_HEADS, B_HEAD_DIM)
    return rmsnorm(kv[:, :, 0], k_norm), kv[:, :, 1]


def peer_ffn(h, w_q, subkeys, u, v):
    B, T, D = h.shape
    n = B * T
    nb = -(-n // PEER_BLOCK)
    xf = jnp.pad(h.reshape(n, D), ((0, nb * PEER_BLOCK - n), (0, 0)))

    def one_block(xb):
        q = (xb @ w_q).reshape(PEER_BLOCK, P_HEADS, 2, P_HALF)
        s = jnp.einsum('nhpe,hpke->nhpk', q, subkeys, preferred_element_type=jnp.float32)
        top_s, top_i = lax.top_k(s, P_TOPK)
        cand_s = (top_s[:, :, 0, :, None] + top_s[:, :, 1, None, :]).reshape(PEER_BLOCK, P_HEADS, P_TOPK * P_TOPK)
        cand_i = (top_i[:, :, 0, :, None] * N_KEYS + top_i[:, :, 1, None, :]).reshape(PEER_BLOCK, P_HEADS, P_TOPK * P_TOPK)
        best_s, best_pos = lax.top_k(cand_s, P_TOPK)
        idx = jnp.take_along_axis(cand_i, best_pos, axis=-1)
        g = jax.nn.softmax(best_s, axis=-1).astype(xb.dtype)
        ue = u[idx]
        ve = v[idx]
        a = jax.nn.gelu(jnp.einsum('nd,nhkd->nhk', xb, ue))
        return jnp.einsum('nhk,nhkd->nd', g * a, ve)

    y = lax.map(one_block, xf.reshape(nb, PEER_BLOCK, D))
    return y.reshape(nb * PEER_BLOCK, D)[:n].reshape(B, T, D)


def setup_inputs(seed: int = 0) -> dict:
    key = jax.random.key(seed)
    ks = iter(jax.random.split(key, 48))

    def nrm(shape, scale):
        return jax.random.normal(next(ks), shape, jnp.float32) * scale

    def gain(shape):
        return 1.0 + nrm(shape, 0.05)

    D = D_MODEL
    b_cache = min(B_WINDOW, PAST_LEN)
    return {
        "x_prompt": nrm((BATCH, SEQ, D), 1.0),
        "x_sample": nrm((DEC_BATCH, DEC_SEQ, D), 1.0),
        "c_prompt": nrm((BATCH, D), 1.0),
        "c_sample": nrm((DEC_BATCH, D), 1.0),
        "cache_a_ckv": nrm((N_A, DEC_BATCH, PAST_LEN, KV_LORA), 1.0),
        "cache_a_krope": nrm((N_A, DEC_BATCH, PAST_LEN, ROPE_DIM), 1.0),
        "cache_b_k": nrm((DEC_BATCH, b_cache, B_HEADS, B_HEAD_DIM), 1.0),
        "cache_b_v": nrm((DEC_BATCH, b_cache, B_HEADS, B_HEAD_DIM), 1.0),
        "ada_w": nrm((DEPTH, D, 6 * D), 0.5 * D ** -0.5),
        "ada_b": nrm((DEPTH, 6 * D), 0.02),
        "norm_mix_w": gain((DEPTH, D)),
        "norm_ffn_w": gain((DEPTH, D)),
        "a_w_in": nrm((N_A, D, Q_LORA + KV_LORA + ROPE_DIM), D ** -0.5),
        "a_q_lora_norm": gain((N_A, Q_LORA)),
        "a_kv_lora_norm": gain((N_A, KV_LORA)),
        "a_w_uq": nrm((N_A, Q_LORA, A_HEADS * QK_DIM), Q_LORA ** -0.5),
        "a_w_ukv": nrm((N_A, KV_LORA, A_HEADS * (NOPE_DIM + V_DIM)), KV_LORA ** -0.5),
        "a_q_norm": gain((N_A, QK_DIM)),
        "a_k_norm": gain((N_A, QK_DIM)),
        "a_w_o": nrm((N_A, A_HEADS * V_DIM, D), (A_HEADS * V_DIM) ** -0.5),
        "kv_ada_w": nrm((D, 2 * D), 0.5 * D ** -0.5),
        "kv_ada_b": nrm((2 * D,), 0.02),
        "kv_norm_w": gain((D,)),
        "b_w_kv": nrm((D, 2 * B_HEADS * B_HEAD_DIM), D ** -0.5),
        "b_k_norm": gain((B_HEAD_DIM,)),
        "b_w_q": nrm((N_B, D, B_HEADS * B_HEAD_DIM), D ** -0.5),
        "b_q_norm": gain((N_B, B_HEAD_DIM)),
        "b_rel_bias": nrm((N_B, B_HEADS, 2 * REL_CLIP + 1), 0.5),
        "b_w_o": nrm((N_B, B_HEADS * B_HEAD_DIM, D), (B_HEADS * B_HEAD_DIM) ** -0.5),
        "p_w_q": nrm((DEPTH, D, P_HEADS * P_DKEY), D ** -0.5),
        "p_subkeys": nrm((DEPTH, P_HEADS, 2, N_KEYS, P_HALF), P_HALF ** -0.5),
        "p_u": nrm((DEPTH, N_EXPERTS, D), D ** -0.5),
        "p_v": nrm((DEPTH, N_EXPERTS, D), P_HEADS ** -0.5),
    }


def reference(x_prompt, x_sample, c_prompt, c_sample, cache_a_ckv, cache_a_krope, cache_b_k, cache_b_v,
              ada_w, ada_b, norm_mix_w, norm_ffn_w,
              a_w_in, a_q_lora_norm, a_kv_lora_norm, a_w_uq, a_w_ukv, a_q_norm, a_k_norm, a_w_o,
              kv_ada_w, kv_ada_b, kv_norm_w, b_w_kv, b_k_norm,
              b_w_q, b_q_norm, b_rel_bias, b_w_o,
              p_w_q, p_subkeys, p_u, p_v):

    def run(x, c, pos, past_ckv, past_kr, past_bk, past_bv):
        prompt = past_bk is None
        new_ckv, new_kr = [], []
        k_all = v_all = k_pos = new_bk = new_bv = None
        for layer in range(DEPTH):
            mod = jax.nn.silu(c) @ ada_w[layer] + ada_b[layer]
            sh_m, sc_m, g_m, sh_f, sc_f, g_f = jnp.split(mod, 6, axis=-1)
            h = modulate(x, norm_mix_w[layer], sh_m, sc_m)
            if layer < N_A:
                i = layer
                out, ckv, kr = mla_mixer(h, pos,
                                         None if prompt else past_ckv[i],
                                         None if prompt else past_kr[i],
                                         a_w_in[i], a_q_lora_norm[i], a_kv_lora_norm[i],
                                         a_w_uq[i], a_w_ukv[i], a_q_norm[i], a_k_norm[i], a_w_o[i])
                new_ckv.append(ckv)
                new_kr.append(kr)
            else:
                j = layer - N_A
                out = band_mixer(h, pos, k_all, v_all, k_pos, b_w_q[j], b_q_norm[j], b_rel_bias[j], b_w_o[j], prompt)
            x = x + g_m[:, None, :] * out
            h = modulate(x, norm_ffn_w[layer], sh_f, sc_f)
            x = x + g_f[:, None, :] * peer_ffn(h, p_w_q[layer], p_subkeys[layer], p_u[layer], p_v[layer])
            if layer == N_A - 1:
                k_new, v_new = shared_kv(x, c, kv_ada_w, kv_ada_b, kv_norm_w, b_w_kv, b_k_norm)
                if prompt:
                    k_all, v_all, k_pos = k_new, v_new, pos
                    new_bk, new_bv = k_new[:, -B_WINDOW:], v_new[:, -B_WINDOW:]
                else:
                    P = past_ckv.shape[2]
                    Pb = past_bk.shape[1]
                    k_all = jnp.concatenate([past_bk, k_new], axis=1)
                    v_all = jnp.concatenate([past_bv, v_new], axis=1)
                    k_pos = jnp.concatenate([P - Pb + jnp.arange(Pb, dtype=jnp.int32), pos])
                    new_bk, new_bv = k_all[:, -Pb:], v_all[:, -Pb:]
        return x, jnp.stack(new_ckv), jnp.stack(new_kr), new_bk, new_bv

    T_p = x_prompt.shape[1]
    T_s = x_sample.shape[1]
    P = cache_a_ckv.shape[2]
    pos_p = jnp.arange(T_p, dtype=jnp.int32)
    pos_s = P + jnp.arange(T_s, dtype=jnp.int32)
    y_prompt, p_ckv, p_kr, p_bk, p_bv = run(x_prompt, c_prompt, pos_p, None, None, None, None)
    y_sample, s_ckv, s_kr, s_bk, s_bv = run(x_sample, c_sample, pos_s, cache_a_ckv, cache_a_krope, cache_b_k, cache_b_v)
    return (y_prompt, y_sample, p_ckv, p_kr, p_bk, p_bv, s_ckv, s_kr, s_bk, s_bv)
```

```python
import functools
import math

import numpy as np
import jax
import jax.numpy as jnp
from jax import lax
from jax.experimental import pallas as pl
from jax.experimental.pallas import tpu as pltpu

F32 = jnp.float32
BF16 = jnp.bfloat16

D_MODEL = 1024
CHUNK = 64
A_HEADS = 16
Q_LORA = 384
KV_LORA = 256
NOPE_DIM = 64
ROPE_DIM = 32
V_DIM = 64
QK_DIM = NOPE_DIM + ROPE_DIM
ROPE_THETA = 10000.0
B_HEADS = 16
B_HEAD_DIM = 64
LEFT_CHUNKS = 8
B_WINDOW = LEFT_CHUNKS * CHUNK
REL_CLIP = 128
P_HEADS = 8
N_KEYS = 128
N_EXPERTS = N_KEYS * N_KEYS
P_HALF = 64
P_TOPK = 16
NEG = -1e30
EPS = 1e-6

LANES = 128
HEAD_PAD = 128
BAND_SUB = 2 * CHUNK
BAND_WIN = B_WINDOW + BAND_SUB
VMEM_LIMIT = 56 * 1024 * 1024

_NT = (((1,), (1,)), ((), ()))


def _cparams(sem):
    return pltpu.CompilerParams(dimension_semantics=sem, vmem_limit_bytes=VMEM_LIMIT)


def _modulate(x, w, shift, scale):
    ms = jnp.mean(x * x, axis=-1, keepdims=True)
    y = x * lax.rsqrt(ms + EPS) * w
    return y * (1.0 + scale) + shift


def _rms(x, w):
    ms = jnp.mean(x * x, axis=-1, keepdims=True)
    return x * lax.rsqrt(ms + EPS) * w


def _mod_spec(arr, tb):
    if arr.shape[1] == 1:
        return pl.BlockSpec((1, 1, arr.shape[2]), lambda b, i, *_: (b, 0, 0))
    return pl.BlockSpec((1, tb, arr.shape[2]), lambda b, i, *_: (b, i, 0))


def _full_spec(arr):
    nd = arr.ndim
    return pl.BlockSpec(arr.shape, lambda *_: (0,) * nd)


def _ada_kernel(c_ref, w_ref, b_ref, o_ref):
    c = c_ref[...]
    a = c / (1.0 + jnp.exp(-c))
    o_ref[...] = jnp.dot(a.astype(BF16), w_ref[...], preferred_element_type=F32) + b_ref[...]


def _ada(c, w, b, tn=2048):
    m, k = c.shape
    n = w.shape[1]
    return pl.pallas_call(
        _ada_kernel,
        grid=(n // tn,),
        in_specs=[pl.BlockSpec((m, k), lambda j: (0, 0)),
                  pl.BlockSpec((k, tn), lambda j: (0, j)),
                  pl.BlockSpec((1, tn), lambda j: (0, j))],
        out_specs=pl.BlockSpec((m, tn), lambda j: (0, j)),
        out_shape=jax.ShapeDtypeStruct((m, n), F32),
        compiler_params=_cparams(("arbitrary",)),
        name="ada_mod",
    )(c, w, b)


def _rope_head(xh, cos, sin, lane):
    rolled = jnp.where(lane < NOPE_DIM + ROPE_DIM // 2,
                       pltpu.roll(xh, LANES - ROPE_DIM // 2, 1),
                       pltpu.roll(xh, ROPE_DIM // 2, 1))
    return xh * cos + rolled * sin


def _mla_pre_kernel(x_ref, sh_ref, sc_ref, nw_ref, win_ref, qln_ref, kvln_ref, wuq_ref, qn_ref,
                    cos_ref, sin_ref, ckv_ref, kr_ref, q_ref):
    x = x_ref[0]
    h = _modulate(x, nw_ref[...], sh_ref[0], sc_ref[0])
    proj = jnp.dot(h.astype(BF16), win_ref[...], preferred_element_type=F32)
    c_q = _rms(proj[:, :Q_LORA], qln_ref[...])
    ckv_ref[0] = _rms(proj[:, Q_LORA:Q_LORA + KV_LORA], kvln_ref[...])
    kr_ref[0] = proj[:, Q_LORA + KV_LORA:]
    q = jnp.dot(c_q.astype(BF16), wuq_ref[...], preferred_element_type=F32)
    cos = cos_ref[...]
    sin = sin_ref[...]
    qn = qn_ref[...]
    lane = lax.broadcasted_iota(jnp.int32, cos.shape, 1)
    scale = QK_DIM ** -0.5
    for hd in range(A_HEADS):
        qh = q[:, hd * HEAD_PAD:(hd + 1) * HEAD_PAD]
        ss = jnp.sum(qh * qh, axis=-1, keepdims=True) * (1.0 / QK_DIM)
        qh = qh * lax.rsqrt(ss + EPS) * qn
        qh = _rope_head(qh, cos, sin, lane)
        q_ref[0, :, hd * HEAD_PAD:(hd + 1) * HEAD_PAD] = (qh * scale).astype(BF16)


def _mla_pre(x, shift, scale, nw, w_in, qln, kvln, wuq_pad, qn_pad, cos_t, sin_t, tb):
    B, T, D = x.shape
    nT = T // tb
    tab_spec = pl.BlockSpec((tb, LANES), lambda b, i: (i, 0))
    return pl.pallas_call(
        _mla_pre_kernel,
        grid=(B, nT),
        in_specs=[pl.BlockSpec((1, tb, D), lambda b, i: (b, i, 0)),
                  _mod_spec(shift, tb), _mod_spec(scale, tb),
                  _full_spec(nw), _full_spec(w_in), _full_spec(qln), _full_spec(kvln),
                  _full_spec(wuq_pad), _full_spec(qn_pad), tab_spec, tab_spec],
        out_specs=[pl.BlockSpec((1, tb, KV_LORA), lambda b, i: (b, i, 0)),
                   pl.BlockSpec((1, tb, ROPE_DIM), lambda b, i: (b, i, 0)),
                   pl.BlockSpec((1, tb, A_HEADS * HEAD_PAD), lambda b, i: (b, i, 0))],
        out_shape=[jax.ShapeDtypeStruct((B, T, KV_LORA), F32),
                   jax.ShapeDtypeStruct((B, T, ROPE_DIM), F32),
                   jax.ShapeDtypeStruct((B, T, A_HEADS * HEAD_PAD), BF16)],
        compiler_params=_cparams(("parallel", "parallel")),
        name="mla_pre",
    )(x, shift, scale, nw, w_in, qln, kvln, wuq_pad, qn_pad, cos_t, sin_t)


def _mla_kv_kernel(ckv_ref, kr_ref, wuk_ref, wuv_ref, kn_ref, place_ref, cos_ref, sin_ref, k_ref, v_ref):
    ckv = ckv_ref[0].astype(BF16)
    kn = jnp.dot(ckv, wuk_ref[...], preferred_element_type=F32)
    v_ref[0] = jnp.dot(ckv, wuv_ref[...], preferred_element_type=F32).astype(BF16)
    kr = kr_ref[0]
    kr_hi = kr.astype(BF16)
    kr_lo = (kr - kr_hi.astype(F32)).astype(BF16)
    place = place_ref[...]
    krp = (jnp.dot(kr_hi, place, preferred_element_type=F32)
           + jnp.dot(kr_lo, place, preferred_element_type=F32))
    cos = cos_ref[...]
    sin = sin_ref[...]
    knw = kn_ref[...]
    lane = lax.broadcasted_iota(jnp.int32, cos.shape, 1)
    for hd in range(A_HEADS):
        kh = kn[:, hd * HEAD_PAD:(hd + 1) * HEAD_PAD] + krp
        ss = jnp.sum(kh * kh, axis=-1, keepdims=True) * (1.0 / QK_DIM)
        kh = kh * lax.rsqrt(ss + EPS) * knw
        kh = _rope_head(kh, cos, sin, lane)
        k_ref[0, :, hd * HEAD_PAD:(hd + 1) * HEAD_PAD] = kh.astype(BF16)


def _mla_kv(ckv, kr, wuk_pad, wuv, kn_pad, place, cos_t, sin_t, tb):
    B, S, _ = ckv.shape
    tab_spec = pl.BlockSpec((tb, LANES), lambda b, i: (i, 0))
    return pl.pallas_call(
        _mla_kv_kernel,
        grid=(B, S // tb),
        in_specs=[pl.BlockSpec((1, tb, KV_LORA), lambda b, i: (b, i, 0)),
                  pl.BlockSpec((1, tb, ROPE_DIM), lambda b, i: (b, i, 0)),
                  _full_spec(wuk_pad), _full_spec(wuv), _full_spec(kn_pad), _full_spec(place),
                  tab_spec, tab_spec],
        out_specs=[pl.BlockSpec((1, tb, A_HEADS * HEAD_PAD), lambda b, i: (b, i, 0)),
                   pl.BlockSpec((1, tb, A_HEADS * V_DIM), lambda b, i: (b, i, 0))],
        out_shape=[jax.ShapeDtypeStruct((B, S, A_HEADS * HEAD_PAD), BF16),
                   jax.ShapeDtypeStruct((B, S, A_HEADS * V_DIM), BF16)],
        compiler_params=_cparams(("parallel", "parallel")),
        name="mla_kv",
    )(ckv, kr, wuk_pad, wuv, kn_pad, place, cos_t, sin_t)


def _mla_attn_kernel(q_ref, k_ref, v_ref, o_ref, m_sc, l_sc, acc_sc, *, tq, tk, causal, nk):
    i = pl.program_id(2)
    j = pl.program_id(3)

    @pl.when(j == 0)
    def _():
        m_sc[...] = jnp.full(m_sc.shape, -jnp.inf, F32)
        l_sc[...] = jnp.zeros(l_sc.shape, F32)
        acc_sc[...] = jnp.zeros(acc_sc.shape, F32)

    def step(masked):
        v = v_ref[0]
        if masked:
            qc = lax.broadcasted_iota(jnp.int32, (tq, tk), 0) // CHUNK
            kc = lax.broadcasted_iota(jnp.int32, (tq, tk), 1) // CHUNK
            allowed = kc <= qc
        for hh in range(2):
            qh = q_ref[0, :, hh * HEAD_PAD:(hh + 1) * HEAD_PAD]
            kh = k_ref[0, :, hh * HEAD_PAD:(hh + 1) * HEAD_PAD]
            s = lax.dot_general(qh, kh, _NT, preferred_element_type=F32)
            if masked:
                s = jnp.where(allowed, s, NEG)
            m_prev = m_sc[hh]
            m_new = jnp.maximum(m_prev, jnp.max(s, axis=-1, keepdims=True))
            alpha = jnp.exp(m_prev - m_new)
            p = jnp.exp(s - m_new)
            l_sc[hh] = alpha * l_sc[hh] + jnp.sum(p, axis=-1, keepdims=True)
            acc_sc[hh] = alpha * acc_sc[hh] + jnp.dot(p.astype(BF16), v, preferred_element_type=F32)
            m_sc[hh] = m_new

    if causal:
        pl.when(j < i)(lambda: step(False))
        pl.when(j == i)(lambda: step(True))
        last = i
    else:
        step(False)
        last = nk - 1

    @pl.when(j == last)
    def _():
        lane = lax.broadcasted_iota(jnp.int32, (tq, LANES), 1)
        o = jnp.where(lane < V_DIM, acc_sc[0] / l_sc[0], acc_sc[1] / l_sc[1])
        o_ref[0] = o.astype(BF16)


def _mla_attn(q, k, v, tq, tk, causal):
    B, T, _ = q.shape
    S = k.shape[1]
    nq, nk = T // tq, S // tk
    if causal:
        kv_map = lambda b, hp, i, j: (b, jnp.minimum(j, i), hp)
    else:
        kv_map = lambda b, hp, i, j: (b, j, hp)
    return pl.pallas_call(
        functools.partial(_mla_attn_kernel, tq=tq, tk=tk, causal=causal, nk=nk),
        grid=(B, A_HEADS // 2, nq, nk),
        in_specs=[pl.BlockSpec((1, tq, 2 * HEAD_PAD), lambda b, hp, i, j: (b, i, hp)),
                  pl.BlockSpec((1, tk, 2 * HEAD_PAD), kv_map),
                  pl.BlockSpec((1, tk, 2 * V_DIM), kv_map)],
        out_specs=pl.BlockSpec((1, tq, 2 * V_DIM), lambda b, hp, i, j: (b, i, hp)),
        out_shape=jax.ShapeDtypeStruct((B, T, A_HEADS * V_DIM), BF16),
        scratch_shapes=[pltpu.VMEM((2, tq, 1), F32), pltpu.VMEM((2, tq, 1), F32),
                        pltpu.VMEM((2, tq, LANES), F32)],
        compiler_params=_cparams(("parallel", "parallel", "parallel", "arbitrary")),
        name="mla_attn",
    )(q, k, v)


def _top16_rows(s, t_sc):
    work = s
    for k in range(P_TOPK):
        m = jnp.max(work, axis=0, keepdims=True)
        t_sc[k:k + 1, :] = m
        if k + 1 < P_TOPK:
            work = jnp.where(work == m, -jnp.inf, work)


def _post_mix_kernel(o_ref, x_ref, g_ref, sh_ref, sc_ref, nw_ref, wo_ref, wq_ref, sk1_ref, sk2_ref,
                     x1_ref, h2_ref, rank_ref, cnt_ref, e2_ref, r_ref, t1_sc, t2_sc):
    mix = jnp.dot(o_ref[0], wo_ref[...], preferred_element_type=F32)
    x1 = x_ref[0] + g_ref[0] * mix
    x1_ref[0] = x1
    h2 = _modulate(x1, nw_ref[...], sh_ref[0], sc_ref[0]).astype(BF16)
    h2_ref[0] = h2
    tb = h2.shape[0]
    row8 = lax.broadcasted_iota(jnp.int32, (8, tb), 0)

    def head_body(h, carry):
        qh = jnp.dot(h2, wq_ref[h], preferred_element_type=F32).astype(BF16)
        s1 = lax.dot_general(sk1_ref[h], qh, _NT, preferred_element_type=F32)
        s2 = lax.dot_general(sk2_ref[h], qh, _NT, preferred_element_type=F32)
        _top16_rows(s1, t1_sc)
        _top16_rows(s2, t2_sc)
        t1 = [t1_sc[k:k + 1, :] for k in range(P_TOPK)]
        t2 = [t2_sc[k:k + 1, :] for k in range(P_TOPK)]
        t2_lo = t2_sc[0:8, :]
        t2_hi = t2_sc[8:16, :]
        t1_hi = t1_sc[8:16, :]
        cands = [t1[0] + t2_lo, t1[0] + t2_hi, t1_hi + t2[0]]
        for k1 in range(1, 8):
            lim = P_TOPK // (k1 + 1)
            c = t1[k1] + t2_lo
            cands.append(c if lim >= 8 else jnp.where(row8 < lim, c, -jnp.inf))
        top = t1[0] + t2[0]
        z = jnp.zeros_like(top)
        tau = top
        for k in range(P_TOPK):
            m = cands[0]
            for c in cands[1:]:
                m = jnp.maximum(m, c)
            m = jnp.max(m, axis=0, keepdims=True)
            z = z + jnp.exp(m - top)
            tau = m
            if k + 1 < P_TOPK:
                cands = [jnp.where(c == m, -jnp.inf, c) for c in cands]
        cnt = jnp.zeros_like(s1)
        rank = jnp.zeros_like(s2)
        for k in range(P_TOPK):
            cnt = cnt + jnp.where(s1 + t2[k] >= tau, 1.0, 0.0)
            rank = rank + jnp.where(t2[k] > s2, 1.0, 0.0)
        rank_ref[h] = rank
        cnt_ref[h] = cnt
        e2_ref[h] = jnp.exp(s2 - t2[0])
        r_ref[h] = jnp.exp(s1 - t1[0]) / z
        return carry

    lax.fori_loop(0, P_HEADS, head_body, 0)


def _post_mix(o, x, gate, shift, scale, nw, w_o, wq_heads, sk1, sk2, tb):
    B, T, D = x.shape
    nT = T // tb
    n = B * T
    tok = lambda b, i: (b, i, 0)
    rt_spec = pl.BlockSpec((P_HEADS, N_KEYS, tb), lambda b, i: (0, 0, b * nT + i))
    rt_shape = jax.ShapeDtypeStruct((P_HEADS, N_KEYS, n), F32)
    return pl.pallas_call(
        _post_mix_kernel,
        grid=(B, nT),
        in_specs=[pl.BlockSpec((1, tb, o.shape[2]), tok), pl.BlockSpec((1, tb, D), tok),
                  _mod_spec(gate, tb), _mod_spec(shift, tb), _mod_spec(scale, tb),
                  _full_spec(nw), _full_spec(w_o), _full_spec(wq_heads), _full_spec(sk1), _full_spec(sk2)],
        out_specs=[pl.BlockSpec((1, tb, D), tok), pl.BlockSpec((1, tb, D), tok),
                   rt_spec, rt_spec, rt_spec, rt_spec],
        out_shape=[jax.ShapeDtypeStruct((B, T, D), F32), jax.ShapeDtypeStruct((B, T, D), BF16),
                   rt_shape, rt_shape, rt_shape, rt_shape],
        scratch_shapes=[pltpu.VMEM((P_TOPK, tb), F32), pltpu.VMEM((P_TOPK, tb), F32)],
        compiler_params=_cparams(("parallel", "parallel")),
        name="post_mix_route",
    )(o, x, gate, shift, scale, nw, w_o, wq_heads, sk1, sk2)


def _gelu_tanh(a):
    c = math.sqrt(2.0 / math.pi)
    return 0.5 * a * (1.0 + jnp.tanh(c * (a + 0.044715 * (a * a * a))))


def _peer_dense_kernel(h_ref, u_ref, vt_ref, rank_ref, cnt_ref, e2_ref, r_ref, x_ref, g_ref,
                       o_ref, acc_ref, *, ec, ne):
    e = pl.program_id(2)

    @pl.when(e == 0)
    def _():
        acc_ref[...] = jnp.zeros(acc_ref.shape, F32)

    a = lax.dot_general(u_ref[...], h_ref[0], _NT, preferred_element_type=F32)
    g = _gelu_tanh(a)
    per = ec // N_KEYS
    parts = []
    for ii in range(per):
        i1 = e * per + ii
        w = None
        for h in range(P_HEADS):
            cnt_row = cnt_ref[h, pl.ds(i1, 1), :]
            r_row = r_ref[h, pl.ds(i1, 1), :]
            contrib = jnp.where(rank_ref[h] < cnt_row, e2_ref[h] * r_row, 0.0)
            w = contrib if w is None else w + contrib
        parts.append((g[ii * N_KEYS:(ii + 1) * N_KEYS] * w).astype(BF16))
    hc = parts[0] if per == 1 else jnp.concatenate(parts, axis=0)
    acc_ref[...] += jnp.dot(vt_ref[...], hc, preferred_element_type=F32)

    @pl.when(e == ne - 1)
    def _():
        o_ref[0] = x_ref[0] + g_ref[0] * acc_ref[...].T


def _peer_dense(h2, u, vt, rank, cnt, e2, r, x, gate, tb, ec):
    B, T, D = x.shape
    nT = T // tb
    ne = N_EXPERTS // ec
    tok = lambda b, i, e: (b, i, 0)
    rt_spec = pl.BlockSpec((P_HEADS, N_KEYS, tb), lambda b, i, e: (0, 0, b * nT + i))
    return pl.pallas_call(
        functools.partial(_peer_dense_kernel, ec=ec, ne=ne),
        grid=(B, nT, ne),
        in_specs=[pl.BlockSpec((1, tb, D), tok),
                  pl.BlockSpec((ec, D), lambda b, i, e: (e, 0)),
                  pl.BlockSpec((D, ec), lambda b, i, e: (0, e)),
                  rt_spec, rt_spec, rt_spec, rt_spec,
                  pl.BlockSpec((1, tb, D), tok), _mod_spec(gate, tb)],
        out_specs=pl.BlockSpec((1, tb, D), tok),
        out_shape=jax.ShapeDtypeStruct((B, T, D), F32),
        scratch_shapes=[pltpu.VMEM((D, tb), F32)],
        compiler_params=_cparams(("parallel", "parallel", "arbitrary")),
        name="peer_dense",
    )(h2, u, vt, rank, cnt, e2, r, x, gate)


def _pair_rms(y, w, lane):
    y2 = y * y
    lo = jnp.sum(jnp.where(lane < B_HEAD_DIM, y2, 0.0), axis=-1, keepdims=True)
    hi = jnp.sum(y2, axis=-1, keepdims=True) - lo
    ms = jnp.where(lane < B_HEAD_DIM, lo, hi) * (1.0 / B_HEAD_DIM)
    return y * lax.rsqrt(ms + EPS) * w


def _shared_kv_kernel(x_ref, sh_ref, sc_ref, nw_ref, wk_ref, wv_ref, kn_ref, k_ref, v_ref, kb_ref, vb_ref):
    h = _modulate(x_ref[0], nw_ref[...], sh_ref[0], sc_ref[0]).astype(BF16)
    kraw = jnp.dot(h, wk_ref[...], preferred_element_type=F32)
    v = jnp.dot(h, wv_ref[...], preferred_element_type=F32)
    v_ref[0] = v
    vb_ref[0] = v.astype(BF16)
    knw = kn_ref[...]
    lane = lax.broadcasted_iota(jnp.int32, (h.shape[0], LANES), 1)
    for hp in range(B_HEADS // 2):
        kh = _pair_rms(kraw[:, hp * LANES:(hp + 1) * LANES], knw, lane)
        k_ref[0, :, hp * LANES:(hp + 1) * LANES] = kh
        kb_ref[0, :, hp * LANES:(hp + 1) * LANES] = kh.astype(BF16)


def _shared_kv(x, shift, scale, nw, wk, wv, kn_pair, tb):
    B, T, D = x.shape
    tok = lambda b, i: (b, i, 0)
    blk = pl.BlockSpec((1, tb, D), tok)
    return pl.pallas_call(
        _shared_kv_kernel,
        grid=(B, T // tb),
        in_specs=[blk, _mod_spec(shift, tb), _mod_spec(scale, tb), _full_spec(nw),
                  _full_spec(wk), _full_spec(wv), _full_spec(kn_pair)],
        out_specs=[blk, blk, blk, blk],
        out_shape=[jax.ShapeDtypeStruct((B, T, D), F32), jax.ShapeDtypeStruct((B, T, D), F32),
                   jax.ShapeDtypeStruct((B, T, D), BF16), jax.ShapeDtypeStruct((B, T, D), BF16)],
        compiler_params=_cparams(("parallel", "parallel")),
        name="shared_kv",
    )(x, shift, scale, nw, wk, wv, kn_pair)


def _band_pre_kernel(x_ref, sh_ref, sc_ref, nw_ref, wq_ref, qn_ref, q_ref):
    h = _modulate(x_ref[0], nw_ref[...], sh_ref[0], sc_ref[0]).astype(BF16)
    q = jnp.dot(h, wq_ref[...], preferred_element_type=F32)
    qnw = qn_ref[...]
    lane = lax.broadcasted_iota(jnp.int32, (h.shape[0], LANES), 1)
    scale = B_HEAD_DIM ** -0.5
    for hp in range(B_HEADS // 2):
        qh = _pair_rms(q[:, hp * LANES:(hp + 1) * LANES], qnw, lane)
        q_ref[0, :, hp * LANES:(hp + 1) * LANES] = (qh * scale).astype(BF16)


def _band_pre(x, shift, scale, nw, wq, qn_pair, tb):
    B, T, D = x.shape
    tok = lambda b, i: (b, i, 0)
    blk = pl.BlockSpec((1, tb, D), tok)
    return pl.pallas_call(
        _band_pre_kernel,
        grid=(B, T // tb),
        in_specs=[blk, _mod_spec(shift, tb), _mod_spec(scale, tb), _full_spec(nw),
                  _full_spec(wq), _full_spec(qn_pair)],
        out_specs=blk,
        out_shape=jax.ShapeDtypeStruct((B, T, D), BF16),
        compiler_params=_cparams(("parallel", "parallel")),
        name="band_pre",
    )(x, shift, scale, nw, wq, qn_pair)


def _band_bias_kernel(tab_ref, o_ref):
    h = pl.program_id(0)
    nvar = 2 * LANES
    r = lax.broadcasted_iota(jnp.int32, (BAND_SUB, nvar), 0)
    w = lax.broadcasted_iota(jnp.int32, (BAND_SUB, nvar), 1) + (BAND_WIN - nvar)
    idx = jnp.clip(r + B_WINDOW - w, -REL_CLIP, REL_CLIP) + REL_CLIP
    far = tab_ref[h, 2 * REL_CLIP]

    def body(t, acc):
        return jnp.where(idx == t, tab_ref[h, t], acc)

    var = lax.fori_loop(0, 2 * REL_CLIP, body, jnp.full((BAND_SUB, nvar), far, F32))
    full = jnp.concatenate([jnp.full((BAND_SUB, BAND_WIN - nvar), far, F32), var], axis=1)
    rr = lax.broadcasted_iota(jnp.int32, (BAND_SUB, BAND_WIN), 0)
    ww = lax.broadcasted_iota(jnp.int32, (BAND_SUB, BAND_WIN), 1)
    qc = rr // CHUNK + LEFT_CHUNKS
    kc = ww // CHUNK
    allowed = (kc <= qc) & (kc >= qc - LEFT_CHUNKS)
    o_ref[0] = jnp.where(allowed, full, NEG)


def _band_bias(table):
    nh = table.shape[0]
    return pl.pallas_call(
        _band_bias_kernel,
        grid=(nh,),
        in_specs=[pl.BlockSpec(memory_space=pltpu.SMEM)],
        out_specs=pl.BlockSpec((1, BAND_SUB, BAND_WIN), lambda h: (h, 0, 0)),
        out_shape=jax.ShapeDtypeStruct((nh, BAND_SUB, BAND_WIN), F32),
        compiler_params=_cparams(("arbitrary",)),
        name="band_bias",
    )(table)


def _band_sub(qs, kw, vw, bias_ref, col0, lane):
    outs = []
    nkw = kw.shape[0]
    for hh in range(2):
        sel = (lane < B_HEAD_DIM) if hh == 0 else (lane >= B_HEAD_DIM)
        qh = jnp.where(sel, qs, jnp.zeros_like(qs))
        s = lax.dot_general(qh, kw, _NT, preferred_element_type=F32)
        s = s + bias_ref[hh, :, col0:col0 + nkw]
        m = jnp.max(s, axis=-1, keepdims=True)
        p = jnp.exp(s - m)
        l = jnp.sum(p, axis=-1, keepdims=True)
        outs.append(jnp.dot(p.astype(BF16), vw, preferred_element_type=F32) / l)
    return jnp.where(lane < B_HEAD_DIM, outs[0], outs[1])


def _band_attn_kernel(q_ref, kp_ref, kc_ref, vp_ref, vc_ref, bias_ref, o_ref, *, tq):
    i = pl.program_id(2)
    lane = lax.broadcasted_iota(jnp.int32, (BAND_SUB, LANES), 1)
    nsub = tq // BAND_SUB

    def run(first):
        for c in range(nsub):
            qs = q_ref[0, c * BAND_SUB:(c + 1) * BAND_SUB, :]
            hi = (c + 1) * BAND_SUB
            if first or hi >= BAND_WIN:
                lo = max(hi - BAND_WIN, 0)
                kw = kc_ref[0, lo:hi, :]
                vw = vc_ref[0, lo:hi, :]
            else:
                lo = tq - (BAND_WIN - hi)
                kw = jnp.concatenate([kp_ref[0, lo:tq, :], kc_ref[0, 0:hi, :]], axis=0)
                vw = jnp.concatenate([vp_ref[0, lo:tq, :], vc_ref[0, 0:hi, :]], axis=0)
            o = _band_sub(qs, kw, vw, bias_ref, BAND_WIN - kw.shape[0], lane)
            o_ref[0, c * BAND_SUB:(c + 1) * BAND_SUB, :] = o.astype(BF16)

    pl.when(i == 0)(lambda: run(True))
    pl.when(i > 0)(lambda: run(False))


def _band_attn(q, k, v, bias, tq):
    B, T, D = q.shape
    assert tq >= B_WINDOW and T % tq == 0
    cur = lambda b, hp, i: (b, i, hp)
    prev = lambda b, hp, i: (b, jnp.maximum(i - 1, 0), hp)
    blk = lambda m: pl.BlockSpec((1, tq, LANES), m)
    return pl.pallas_call(
        functools.partial(_band_attn_kernel, tq=tq),
        grid=(B, B_HEADS // 2, T // tq),
        in_specs=[blk(cur), blk(prev), blk(cur), blk(prev), blk(cur),
                  pl.BlockSpec((2, BAND_SUB, BAND_WIN), lambda b, hp, i: (hp, 0, 0))],
        out_specs=blk(cur),
        out_shape=jax.ShapeDtypeStruct((B, T, D), BF16),
        compiler_params=_cparams(("parallel", "parallel", "arbitrary")),
        name="band_attn",
    )(q, k, k, v, v, bias)


def _band_step_kernel(q_ref, k_ref, v_ref, bias_ref, o_ref):
    lane = lax.broadcasted_iota(jnp.int32, (BAND_SUB, LANES), 1)
    o = _band_sub(q_ref[0], k_ref[0], v_ref[0], bias_ref, 0, lane)
    o_ref[0] = o.astype(BF16)


def _band_step(q, kwin, vwin, bias):
    B, _, D = q.shape
    return pl.pallas_call(
        _band_step_kernel,
        grid=(B, B_HEADS // 2),
        in_specs=[pl.BlockSpec((1, BAND_SUB, LANES), lambda b, hp: (b, 0, hp)),
                  pl.BlockSpec((1, BAND_WIN, LANES), lambda b, hp: (b, 0, hp)),
                  pl.BlockSpec((1, BAND_WIN, LANES), lambda b, hp: (b, 0, hp)),
                  pl.BlockSpec((2, BAND_SUB, BAND_WIN), lambda b, hp: (hp, 0, 0))],
        out_specs=pl.BlockSpec((1, BAND_SUB, LANES), lambda b, hp: (b, 0, hp)),
        out_shape=jax.ShapeDtypeStruct((B, BAND_SUB, D), BF16),
        compiler_params=_cparams(("parallel", "parallel")),
        name="band_step",
    )(q, kwin, vwin, bias)


def _rope_tables(pos):
    half = ROPE_DIM // 2
    freqs = ROPE_THETA ** (-jnp.arange(half, dtype=F32) / half)
    ang = pos.astype(F32)[:, None] * freqs[None, :]
    c, s = jnp.cos(ang), jnp.sin(ang)
    n = pos.shape[0]
    ones = jnp.ones((n, NOPE_DIM), F32)
    zeros = jnp.zeros((n, NOPE_DIM), F32)
    pad1 = jnp.ones((n, HEAD_PAD - QK_DIM), F32)
    pad0 = jnp.zeros((n, HEAD_PAD - QK_DIM), F32)
    return (jnp.concatenate([ones, c, c, pad1], axis=1),
            jnp.concatenate([zeros, -s, s, pad0], axis=1))


def _pad_heads(w, nheads, width):
    lead = w.shape[:-1]
    w = w.reshape(lead + (nheads, width))
    w = jnp.pad(w, [(0, 0)] * len(lead) + [(0, 0), (0, HEAD_PAD - width)])
    return w.reshape(lead + (nheads * HEAD_PAD,))


def _block(n, pref):
    for t in pref:
        if n % t == 0:
            return t
    return n


def kernel(x_prompt, x_sample, c_prompt, c_sample, cache_a_ckv, cache_a_krope, cache_b_k, cache_b_v, ada_w, ada_b, norm_mix_w, norm_ffn_w, a_w_in, a_q_lora_norm, a_kv_lora_norm, a_w_uq, a_w_ukv, a_q_norm, a_k_norm, a_w_o, kv_ada_w, kv_ada_b, kv_norm_w, b_w_kv, b_k_norm, b_w_q, b_q_norm, b_rel_bias, b_w_o, p_w_q, p_subkeys, p_u, p_v):
    D = D_MODEL
    Bp, Tp, _ = x_prompt.shape
    Bs, Ts, _ = x_sample.shape
    P = cache_a_ckv.shape[2]
    Pb = cache_b_k.shape[1]
    assert Ts == CHUNK and Pb == B_WINDOW and P % CHUNK == 0

    ada_all_w = jnp.concatenate([ada_w[0], ada_w[1], kv_ada_w], axis=1).astype(BF16)
    ada_all_b = jnp.concatenate([ada_b[0], ada_b[1], kv_ada_b])[None, :]
    c_all = jnp.concatenate([c_prompt, c_sample], axis=0)
    mod = _ada(c_all, ada_all_w, ada_all_b)

    w_in = a_w_in[0].astype(BF16)
    qln = a_q_lora_norm[0][None, :]
    kvln = a_kv_lora_norm[0][None, :]
    wuq_pad = _pad_heads(a_w_uq[0], A_HEADS, QK_DIM).astype(BF16)
    wukv = a_w_ukv[0].reshape(KV_LORA, A_HEADS, NOPE_DIM + V_DIM)
    wuk_pad = jnp.pad(wukv[:, :, :NOPE_DIM], ((0, 0), (0, 0), (0, HEAD_PAD - NOPE_DIM))
                      ).reshape(KV_LORA, A_HEADS * HEAD_PAD).astype(BF16)
    wuv = wukv[:, :, NOPE_DIM:].reshape(KV_LORA, A_HEADS * V_DIM).astype(BF16)
    qn_pad = jnp.pad(a_q_norm[0], (0, HEAD_PAD - QK_DIM))[None, :]
    kn_pad = jnp.pad(a_k_norm[0], (0, HEAD_PAD - QK_DIM))[None, :]
    place = jnp.asarray(np.eye(ROPE_DIM, HEAD_PAD, k=NOPE_DIM), BF16)
    a_wo = a_w_o[0].astype(BF16)
    wk_b = b_w_kv[:, :D].astype(BF16)
    wv_b = b_w_kv[:, D:].astype(BF16)
    bkn_pair = jnp.tile(b_k_norm, 2)[None, :]
    bqn_pair = jnp.tile(b_q_norm[0], 2)[None, :]
    bwq = b_w_q[0].astype(BF16)
    bwo = b_w_o[0].astype(BF16)
    bias_tile = _band_bias(b_rel_bias[0])

    def peer_weights(layer):
        wq = p_w_q[layer].reshape(D, P_HEADS, 2 * P_HALF).transpose(1, 0, 2).astype(BF16)
        sk = p_subkeys[layer]
        sk1 = jnp.pad(sk[:, 0], ((0, 0), (0, 0), (0, P_HALF))).astype(BF16)
        sk2 = jnp.pad(sk[:, 1], ((0, 0), (0, 0), (P_HALF, 0))).astype(BF16)
        return wq, sk1, sk2, p_u[layer].astype(BF16), p_v[layer].T.astype(BF16)

    peer_w = [peer_weights(0), peer_weights(1)]
    norm_mix = norm_mix_w[:, None, :]
    norm_ffn = norm_ffn_w[:, None, :]
    kv_nw = kv_norm_w[None, :]

    def run(x, modp, per_token, pos_q, past):
        B, T, _ = x.shape
        if per_token:
            modv = jnp.repeat(modp, T, axis=0)[None]
            xw = x.reshape(1, B * T, D)
        else:
            modv = modp[:, None, :]
            xw = x
        Bw, Tw, _ = xw.shape
        sl = lambda k: modv[:, :, k * D:(k + 1) * D]
        tb = _block(Tw, (512, 256, 128))
        tbr = _block(Tw, (256, 128))
        tbe = _block(Tw, (512, 256, 128))

        cos_q, sin_q = _rope_tables(pos_q)
        ckv, kr, q = _mla_pre(xw, sl(0), sl(1), norm_mix[0], w_in, qln, kvln, wuq_pad, qn_pad,
                              cos_q, sin_q, tb)
        ckv = ckv.reshape(B, T, KV_LORA)
        kr = kr.reshape(B, T, ROPE_DIM)
        q = q.reshape(B, T, A_HEADS * HEAD_PAD)
        if past is None:
            ckv_all, kr_all = ckv, kr
            pos_k = pos_q[:T]
        else:
            ckv_all = jnp.concatenate([past[0], ckv], axis=1)
            kr_all = jnp.concatenate([past[1], kr], axis=1)
            pos_k = jnp.arange(P + T, dtype=jnp.int32)
        S = ckv_all.shape[1]
        cos_k, sin_k = _rope_tables(pos_k)
        tbk = _block(S, (512, 704, 256, 192, 64))
        k, v = _mla_kv(ckv_all, kr_all, wuk_pad, wuv, kn_pad, place, cos_k, sin_k, tbk)
        if past is None:
            ta = _block(T, (512, 256, 128, 64))
            o = _mla_attn(q, k, v, ta, ta, True)
        else:
            o = _mla_attn(q, k, v, T, S, False)
        o = o.reshape(Bw, Tw, A_HEADS * V_DIM)

        wq, sk1, sk2, u, vt = peer_w[0]
        x1, h2, rank, cnt, e2, r = _post_mix(o, xw, sl(2), sl(3), sl(4), norm_ffn[0], a_wo, wq, sk1, sk2, tbr)
        x2 = _peer_dense(h2, u, vt, rank, cnt, e2, r, x1, sl(5), tbe, 512)

        kf, vf, kb, vb = _shared_kv(x2, sl(12), sl(13), kv_nw, wk_b, wv_b, bkn_pair, tb)

        qb = _band_pre(x2, sl(6), sl(7), norm_mix[1], bwq, bqn_pair, tb)
        if past is None:
            ob = _band_attn(qb, kb, vb, bias_tile, B_WINDOW)
            new_bk = kf[:, -B_WINDOW:].reshape(B, B_WINDOW, B_HEADS, B_HEAD_DIM)
            new_bv = vf[:, -B_WINDOW:].reshape(B, B_WINDOW, B_HEADS, B_HEAD_DIM)
        else:
            zq = jnp.zeros((B, CHUNK, D), BF16)
            qpad = jnp.concatenate([zq, qb.reshape(B, T, D)], axis=1)
            kwin = jnp.concatenate([zq, past[2].reshape(B, Pb, D).astype(BF16), kb.reshape(B, T, D)], axis=1)
            vwin = jnp.concatenate([zq, past[3].reshape(B, Pb, D).astype(BF16), vb.reshape(B, T, D)], axis=1)
            ob = _band_step(qpad, kwin, vwin, bias_tile)[:, CHUNK:].reshape(Bw, Tw, D)
            new_bk = jnp.concatenate([past[2], kf.reshape(B, T, B_HEADS, B_HEAD_DIM)], axis=1)[:, -Pb:]
            new_bv = jnp.concatenate([past[3], vf.reshape(B, T, B_HEADS, B_HEAD_DIM)], axis=1)[:, -Pb:]

        wq, sk1, sk2, u, vt = peer_w[1]
        x3, h4, rank, cnt, e2, r = _post_mix(ob, x2, sl(8), sl(9), sl(10), norm_ffn[1], bwo, wq, sk1, sk2, tbr)
        y = _peer_dense(h4, u, vt, rank, cnt, e2, r, x3, sl(11), tbe, 512)
        return y.reshape(B, T, D), ckv[None], kr[None], new_bk, new_bv

    pos_p = jnp.arange(Tp, dtype=jnp.int32)
    pos_s = jnp.tile(P + jnp.arange(Ts, dtype=jnp.int32), Bs)
    y_p, p_ckv, p_kr, p_bk, p_bv = run(x_prompt, mod[:Bp], False, pos_p, None)
    y_s, s_ckv, s_kr, s_bk, s_bv = run(x_sample, mod[Bp:], True, pos_s,
                                       (cache_a_ckv[0], cache_a_krope[0], cache_b_k, cache_b_v))
    return (y_p, y_s, p_ckv, p_kr, p_bk, p_bv, s_ckv, s_kr, s_bk, s_bv)
```

```python
import functools
import math

import numpy as np
import jax
import jax.numpy as jnp
from jax import lax
from jax.experimental import pallas as pl
from jax.experimental.pallas import tpu as pltpu

F32 = jnp.float32
BF16 = jnp.bfloat16

D_MODEL = 1024
CHUNK = 64
A_HEADS = 16
Q_LORA = 384
KV_LORA = 256
NOPE_DIM = 64
ROPE_DIM = 32
V_DIM = 64
QK_DIM = NOPE_DIM + ROPE_DIM
ROPE_THETA = 10000.0
B_HEADS = 16
B_HEAD_DIM = 64
LEFT_CHUNKS = 8
B_WINDOW = LEFT_CHUNKS * CHUNK
REL_CLIP = 128
P_HEADS = 8
N_KEYS = 128
N_EXPERTS = N_KEYS * N_KEYS
P_HALF = 64
P_TOPK = 16
NEG = -1e30
EPS = 1e-6

LANES = 128
HEAD_PAD = 128
BAND_SUB = 2 * CHUNK
BAND_WIN = B_WINDOW + BAND_SUB
VMEM_LIMIT = 56 * 1024 * 1024
PEER_EC = 1024
PEER_SUB = 256

_NT = (((1,), (1,)), ((), ()))


def _cparams(sem):
    return pltpu.CompilerParams(dimension_semantics=sem, vmem_limit_bytes=VMEM_LIMIT)


def _modulate(x, w, shift, scale):
    ms = jnp.mean(x * x, axis=-1, keepdims=True)
    y = x * lax.rsqrt(ms + EPS) * w
    return y * (1.0 + scale) + shift


def _rms(x, w):
    ms = jnp.mean(x * x, axis=-1, keepdims=True)
    return x * lax.rsqrt(ms + EPS) * w


def _mod_spec(arr, tb):
    if arr.shape[1] == 1:
        return pl.BlockSpec((1, 1, arr.shape[2]), lambda b, i, *_: (b, 0, 0))
    return pl.BlockSpec((1, tb, arr.shape[2]), lambda b, i, *_: (b, i, 0))


def _full_spec(arr):
    nd = arr.ndim
    return pl.BlockSpec(arr.shape, lambda *_: (0,) * nd)


def _ada_kernel(c_ref, w_ref, b_ref, o_ref):
    c = c_ref[...]
    a = c / (1.0 + jnp.exp(-c))
    o_ref[...] = jnp.dot(a.astype(BF16), w_ref[...], preferred_element_type=F32) + b_ref[...]


def _ada(c, w, b, tn=2048):
    m, k = c.shape
    n = w.shape[1]
    return pl.pallas_call(
        _ada_kernel,
        grid=(n // tn,),
        in_specs=[pl.BlockSpec((m, k), lambda j: (0, 0)),
                  pl.BlockSpec((k, tn), lambda j: (0, j)),
                  pl.BlockSpec((1, tn), lambda j: (0, j))],
        out_specs=pl.BlockSpec((m, tn), lambda j: (0, j)),
        out_shape=jax.ShapeDtypeStruct((m, n), F32),
        compiler_params=_cparams(("arbitrary",)),
        name="ada_mod",
    )(c, w, b)


def _rope_head(xh, cos, sin, lane):
    rolled = jnp.where(lane < NOPE_DIM + ROPE_DIM // 2,
                       pltpu.roll(xh, LANES - ROPE_DIM // 2, 1),
                       pltpu.roll(xh, ROPE_DIM // 2, 1))
    return xh * cos + rolled * sin


def _mla_pre_kernel(x_ref, sh_ref, sc_ref, nw_ref, win_ref, qln_ref, kvln_ref, wuq_ref, qn_ref,
                    cos_ref, sin_ref, ckv_ref, kr_ref, q_ref):
    x = x_ref[0]
    h = _modulate(x, nw_ref[...], sh_ref[0], sc_ref[0])
    proj = jnp.dot(h.astype(BF16), win_ref[...], preferred_element_type=F32)
    c_q = _rms(proj[:, :Q_LORA], qln_ref[...])
    ckv_ref[0] = _rms(proj[:, Q_LORA:Q_LORA + KV_LORA], kvln_ref[...])
    kr_ref[0] = proj[:, Q_LORA + KV_LORA:]
    q = jnp.dot(c_q.astype(BF16), wuq_ref[...], preferred_element_type=F32)
    cos = cos_ref[...]
    sin = sin_ref[...]
    qn = qn_ref[...]
    lane = lax.broadcasted_iota(jnp.int32, cos.shape, 1)
    scale = QK_DIM ** -0.5
    for hd in range(A_HEADS):
        qh = q[:, hd * HEAD_PAD:(hd + 1) * HEAD_PAD]
        ss = jnp.sum(qh * qh, axis=-1, keepdims=True) * (1.0 / QK_DIM)
        qh = qh * lax.rsqrt(ss + EPS) * qn
        qh = _rope_head(qh, cos, sin, lane)
        q_ref[0, :, hd * HEAD_PAD:(hd + 1) * HEAD_PAD] = (qh * scale).astype(BF16)


def _mla_pre(x, shift, scale, nw, w_in, qln, kvln, wuq_pad, qn_pad, cos_t, sin_t, tb):
    B, T, D = x.shape
    nT = T // tb
    tab_spec = pl.BlockSpec((tb, LANES), lambda b, i: (i, 0))
    return pl.pallas_call(
        _mla_pre_kernel,
        grid=(B, nT),
        in_specs=[pl.BlockSpec((1, tb, D), lambda b, i: (b, i, 0)),
                  _mod_spec(shift, tb), _mod_spec(scale, tb),
                  _full_spec(nw), _full_spec(w_in), _full_spec(qln), _full_spec(kvln),
                  _full_spec(wuq_pad), _full_spec(qn_pad), tab_spec, tab_spec],
        out_specs=[pl.BlockSpec((1, tb, KV_LORA), lambda b, i: (b, i, 0)),
                   pl.BlockSpec((1, tb, ROPE_DIM), lambda b, i: (b, i, 0)),
                   pl.BlockSpec((1, tb, A_HEADS * HEAD_PAD), lambda b, i: (b, i, 0))],
        out_shape=[jax.ShapeDtypeStruct((B, T, KV_LORA), F32),
                   jax.ShapeDtypeStruct((B, T, ROPE_DIM), F32),
                   jax.ShapeDtypeStruct((B, T, A_HEADS * HEAD_PAD), BF16)],
        compiler_params=_cparams(("parallel", "parallel")),
        name="mla_pre",
    )(x, shift, scale, nw, w_in, qln, kvln, wuq_pad, qn_pad, cos_t, sin_t)


def _mla_kv_kernel(ckv_ref, kr_ref, wuk_ref, wuv_ref, kn_ref, place_ref, cos_ref, sin_ref, k_ref, v_ref):
    ckv = ckv_ref[0].astype(BF16)
    kn = jnp.dot(ckv, wuk_ref[...], preferred_element_type=F32)
    v_ref[0] = jnp.dot(ckv, wuv_ref[...], preferred_element_type=F32).astype(BF16)
    kr = kr_ref[0]
    kr_hi = kr.astype(BF16)
    kr_lo = (kr - kr_hi.astype(F32)).astype(BF16)
    place = place_ref[...]
    krp = (jnp.dot(kr_hi, place, preferred_element_type=F32)
           + jnp.dot(kr_lo, place, preferred_element_type=F32))
    cos = cos_ref[...]
    sin = sin_ref[...]
    knw = kn_ref[...]
    lane = lax.broadcasted_iota(jnp.int32, cos.shape, 1)
    for hd in range(A_HEADS):
        kh = kn[:, hd * HEAD_PAD:(hd + 1) * HEAD_PAD] + krp
        ss = jnp.sum(kh * kh, axis=-1, keepdims=True) * (1.0 / QK_DIM)
        kh = kh * lax.rsqrt(ss + EPS) * knw
        kh = _rope_head(kh, cos, sin, lane)
        k_ref[0, :, hd * HEAD_PAD:(hd + 1) * HEAD_PAD] = kh.astype(BF16)


def _mla_kv(ckv, kr, wuk_pad, wuv, kn_pad, place, cos_t, sin_t, tb):
    B, S, _ = ckv.shape
    tab_spec = pl.BlockSpec((tb, LANES), lambda b, i: (i, 0))
    return pl.pallas_call(
        _mla_kv_kernel,
        grid=(B, S // tb),
        in_specs=[pl.BlockSpec((1, tb, KV_LORA), lambda b, i: (b, i, 0)),
                  pl.BlockSpec((1, tb, ROPE_DIM), lambda b, i: (b, i, 0)),
                  _full_spec(wuk_pad), _full_spec(wuv), _full_spec(kn_pad), _full_spec(place),
                  tab_spec, tab_spec],
        out_specs=[pl.BlockSpec((1, tb, A_HEADS * HEAD_PAD), lambda b, i: (b, i, 0)),
                   pl.BlockSpec((1, tb, A_HEADS * V_DIM), lambda b, i: (b, i, 0))],
        out_shape=[jax.ShapeDtypeStruct((B, S, A_HEADS * HEAD_PAD), BF16),
                   jax.ShapeDtypeStruct((B, S, A_HEADS * V_DIM), BF16)],
        compiler_params=_cparams(("parallel", "parallel")),
        name="mla_kv",
    )(ckv, kr, wuk_pad, wuv, kn_pad, place, cos_t, sin_t)


def _mla_attn_kernel(q_ref, k_ref, v_ref, o_ref, m_sc, l_sc, acc_sc, *, tq, tk, causal, nk):
    i = pl.program_id(2)
    j = pl.program_id(3)

    @pl.when(j == 0)
    def _():
        m_sc[...] = jnp.full(m_sc.shape, -jnp.inf, F32)
        l_sc[...] = jnp.zeros(l_sc.shape, F32)
        acc_sc[...] = jnp.zeros(acc_sc.shape, F32)

    def step(masked):
        v = v_ref[0]
        if masked:
            qc = lax.broadcasted_iota(jnp.int32, (tq, tk), 0) // CHUNK
            kc = lax.broadcasted_iota(jnp.int32, (tq, tk), 1) // CHUNK
            allowed = kc <= qc
        for hh in range(2):
            qh = q_ref[0, :, hh * HEAD_PAD:(hh + 1) * HEAD_PAD]
            kh = k_ref[0, :, hh * HEAD_PAD:(hh + 1) * HEAD_PAD]
            s = lax.dot_general(qh, kh, _NT, preferred_element_type=F32)
            if masked:
                s = jnp.where(allowed, s, NEG)
            m_prev = m_sc[hh]
            m_new = jnp.maximum(m_prev, jnp.max(s, axis=-1, keepdims=True))
            alpha = jnp.exp(m_prev - m_new)
            p = jnp.exp(s - m_new)
            l_sc[hh] = alpha * l_sc[hh] + jnp.sum(p, axis=-1, keepdims=True)
            acc_sc[hh] = alpha * acc_sc[hh] + jnp.dot(p.astype(BF16), v, preferred_element_type=F32)
            m_sc[hh] = m_new

    if causal:
        pl.when(j < i)(lambda: step(False))
        pl.when(j == i)(lambda: step(True))
        last = i
    else:
        step(False)
        last = nk - 1

    @pl.when(j == last)
    def _():
        lane = lax.broadcasted_iota(jnp.int32, (tq, LANES), 1)
        o = jnp.where(lane < V_DIM, acc_sc[0] / l_sc[0], acc_sc[1] / l_sc[1])
        o_ref[0] = o.astype(BF16)


def _mla_attn(q, k, v, tq, tk, causal):
    B, T, _ = q.shape
    S = k.shape[1]
    nq, nk = T // tq, S // tk
    if causal:
        kv_map = lambda b, hp, i, j: (b, jnp.minimum(j, i), hp)
    else:
        kv_map = lambda b, hp, i, j: (b, j, hp)
    return pl.pallas_call(
        functools.partial(_mla_attn_kernel, tq=tq, tk=tk, causal=causal, nk=nk),
        grid=(B, A_HEADS // 2, nq, nk),
        in_specs=[pl.BlockSpec((1, tq, 2 * HEAD_PAD), lambda b, hp, i, j: (b, i, hp)),
                  pl.BlockSpec((1, tk, 2 * HEAD_PAD), kv_map),
                  pl.BlockSpec((1, tk, 2 * V_DIM), kv_map)],
        out_specs=pl.BlockSpec((1, tq, 2 * V_DIM), lambda b, hp, i, j: (b, i, hp)),
        out_shape=jax.ShapeDtypeStruct((B, T, A_HEADS * V_DIM), BF16),
        scratch_shapes=[pltpu.VMEM((2, tq, 1), F32), pltpu.VMEM((2, tq, 1), F32),
                        pltpu.VMEM((2, tq, LANES), F32)],
        compiler_params=_cparams(("parallel", "parallel", "parallel", "arbitrary")),
        name="mla_attn",
    )(q, k, v)


def _top16_rows(s, t_sc):
    work = s
    for k in range(P_TOPK):
        m = jnp.max(work, axis=0, keepdims=True)
        t_sc[k:k + 1, :] = m
        if k + 1 < P_TOPK:
            work = jnp.where(work == m, -jnp.inf, work)


def _post_mix_kernel(o_ref, x_ref, g_ref, sh_ref, sc_ref, nw_ref, wo_ref, wq_ref, sk1_ref, sk2_ref,
                     x1_ref, h2_ref, rank_ref, cnt_ref, e2_ref, r_ref, t1_sc, t2_sc):
    mix = jnp.dot(o_ref[0], wo_ref[...], preferred_element_type=F32)
    x1 = x_ref[0] + g_ref[0] * mix
    x1_ref[0] = x1
    h2 = _modulate(x1, nw_ref[...], sh_ref[0], sc_ref[0]).astype(BF16)
    h2_ref[0] = h2
    tb = h2.shape[0]
    row8 = lax.broadcasted_iota(jnp.int32, (8, tb), 0)

    def head_body(h, carry):
        qh = jnp.dot(h2, wq_ref[h], preferred_element_type=F32).astype(BF16)
        s1 = lax.dot_general(sk1_ref[h], qh, _NT, preferred_element_type=F32)
        s2 = lax.dot_general(sk2_ref[h], qh, _NT, preferred_element_type=F32)
        _top16_rows(s1, t1_sc)
        _top16_rows(s2, t2_sc)
        t1 = [t1_sc[k:k + 1, :] for k in range(P_TOPK)]
        t2 = [t2_sc[k:k + 1, :] for k in range(P_TOPK)]
        t2_lo = t2_sc[0:8, :]
        t2_hi = t2_sc[8:16, :]
        t1_hi = t1_sc[8:16, :]
        cands = [t1[0] + t2_lo, t1[0] + t2_hi, t1_hi + t2[0]]
        for k1 in range(1, 8):
            lim = P_TOPK // (k1 + 1)
            c = t1[k1] + t2_lo
            cands.append(c if lim >= 8 else jnp.where(row8 < lim, c, -jnp.inf))
        top = t1[0] + t2[0]
        z = jnp.zeros_like(top)
        tau = top
        for k in range(P_TOPK):
            m = cands[0]
            for c in cands[1:]:
                m = jnp.maximum(m, c)
            m = jnp.max(m, axis=0, keepdims=True)
            z = z + jnp.exp(m - top)
            tau = m
            if k + 1 < P_TOPK:
                cands = [jnp.where(c == m, -jnp.inf, c) for c in cands]
        cnt = jnp.zeros_like(s1)
        rank = jnp.zeros_like(s2)
        for k in range(P_TOPK):
            cnt = cnt + jnp.where(s1 + t2[k] >= tau, 1.0, 0.0)
            rank = rank + jnp.where(t2[k] > s2, 1.0, 0.0)
        rank_ref[h] = rank.astype(BF16)
        cnt_ref[h] = cnt
        e2_ref[h] = jnp.exp(s2 - t2[0]).astype(BF16)
        r_ref[h] = jnp.exp(s1 - t1[0]) / z
        return carry

    lax.fori_loop(0, P_HEADS, head_body, 0)


def _post_mix(o, x, gate, shift, scale, nw, w_o, wq_heads, sk1, sk2, tb):
    B, T, D = x.shape
    nT = T // tb
    n = B * T
    tok = lambda b, i: (b, i, 0)
    rt_spec = pl.BlockSpec((P_HEADS, N_KEYS, tb), lambda b, i: (0, 0, b * nT + i))
    rt_shape = jax.ShapeDtypeStruct((P_HEADS, N_KEYS, n), F32)
    rt_shape_b = jax.ShapeDtypeStruct((P_HEADS, N_KEYS, n), BF16)
    return pl.pallas_call(
        _post_mix_kernel,
        grid=(B, nT),
        in_specs=[pl.BlockSpec((1, tb, o.shape[2]), tok), pl.BlockSpec((1, tb, D), tok),
                  _mod_spec(gate, tb), _mod_spec(shift, tb), _mod_spec(scale, tb),
                  _full_spec(nw), _full_spec(w_o), _full_spec(wq_heads), _full_spec(sk1), _full_spec(sk2)],
        out_specs=[pl.BlockSpec((1, tb, D), tok), pl.BlockSpec((1, tb, D), tok),
                   rt_spec, rt_spec, rt_spec, rt_spec],
        out_shape=[jax.ShapeDtypeStruct((B, T, D), F32), jax.ShapeDtypeStruct((B, T, D), BF16),
                   rt_shape_b, rt_shape, rt_shape_b, rt_shape],
        scratch_shapes=[pltpu.VMEM((P_TOPK, tb), F32), pltpu.VMEM((P_TOPK, tb), F32)],
        compiler_params=_cparams(("parallel", "parallel")),
        name="post_mix_route",
    )(o, x, gate, shift, scale, nw, w_o, wq_heads, sk1, sk2)


def _gelu_tanh(a):
    c = math.sqrt(2.0 / math.pi)
    return 0.5 * a * (1.0 + jnp.tanh(c * (a + 0.044715 * (a * a * a))))


def _peer_dense_kernel(h_ref, u_ref, vt_ref, rank_ref, cnt_ref, e2_ref, r_ref, x_ref, g_ref,
                       o_ref, acc_ref, *, ec, sub, ne):
    e = pl.program_id(2)

    @pl.when(e == 0)
    def _():
        acc_ref[...] = jnp.zeros(acc_ref.shape, F32)

    hb = h_ref[0]
    zero = jnp.zeros((), BF16)
    per = sub // N_KEYS
    y = None
    for c in range(ec // sub):
        a = lax.dot_general(u_ref[c * sub:(c + 1) * sub, :], hb, _NT, preferred_element_type=F32)
        g = _gelu_tanh(a.astype(BF16))
        parts = []
        for ii in range(per):
            i1 = e * (ec // N_KEYS) + c * per + ii
            w = None
            for h in range(P_HEADS):
                cnt_row = cnt_ref[h, pl.ds(i1, 1), :].astype(BF16)
                r_row = r_ref[h, pl.ds(i1, 1), :].astype(BF16)
                contrib = jnp.where(rank_ref[h] < cnt_row, e2_ref[h], zero) * r_row
                w = contrib if w is None else w + contrib
            parts.append(g[ii * N_KEYS:(ii + 1) * N_KEYS] * w)
        hc = parts[0] if per == 1 else jnp.concatenate(parts, axis=0)
        yc = jnp.dot(vt_ref[:, c * sub:(c + 1) * sub], hc, preferred_element_type=F32)
        y = yc if y is None else y + yc
    acc_ref[...] += y

    @pl.when(e == ne - 1)
    def _():
        o_ref[0] = x_ref[0] + g_ref[0] * acc_ref[...].T


def _peer_dense(h2, u, vt, rank, cnt, e2, r, x, gate, tb, ec=PEER_EC, sub=PEER_SUB):
    B, T, D = x.shape
    nT = T // tb
    ne = N_EXPERTS // ec
    tok = lambda b, i, e: (b, i, 0)
    rt_spec = pl.BlockSpec((P_HEADS, N_KEYS, tb), lambda b, i, e: (0, 0, b * nT + i))
    return pl.pallas_call(
        functools.partial(_peer_dense_kernel, ec=ec, sub=sub, ne=ne),
        grid=(B, nT, ne),
        in_specs=[pl.BlockSpec((1, tb, D), tok),
                  pl.BlockSpec((ec, D), lambda b, i, e: (e, 0)),
                  pl.BlockSpec((D, ec), lambda b, i, e: (0, e)),
                  rt_spec, rt_spec, rt_spec, rt_spec,
                  pl.BlockSpec((1, tb, D), tok), _mod_spec(gate, tb)],
        out_specs=pl.BlockSpec((1, tb, D), tok),
        out_shape=jax.ShapeDtypeStruct((B, T, D), F32),
        scratch_shapes=[pltpu.VMEM((D, tb), F32)],
        compiler_params=_cparams(("parallel", "parallel", "arbitrary")),
        name="peer_dense",
    )(h2, u, vt, rank, cnt, e2, r, x, gate)


def _pair_rms(y, w, lane):
    y2 = y * y
    lo = jnp.sum(jnp.where(lane < B_HEAD_DIM, y2, 0.0), axis=-1, keepdims=True)
    hi = jnp.sum(y2, axis=-1, keepdims=True) - lo
    ms = jnp.where(lane < B_HEAD_DIM, lo, hi) * (1.0 / B_HEAD_DIM)
    return y * lax.rsqrt(ms + EPS) * w


def _shared_kv_kernel(x_ref, sh_ref, sc_ref, nw_ref, wk_ref, wv_ref, kn_ref, k_ref, v_ref, kb_ref, vb_ref):
    h = _modulate(x_ref[0], nw_ref[...], sh_ref[0], sc_ref[0]).astype(BF16)
    kraw = jnp.dot(h, wk_ref[...], preferred_element_type=F32)
    v = jnp.dot(h, wv_ref[...], preferred_element_type=F32)
    v_ref[0] = v
    vb_ref[0] = v.astype(BF16)
    knw = kn_ref[...]
    lane = lax.broadcasted_iota(jnp.int32, (h.shape[0], LANES), 1)
    for hp in range(B_HEADS // 2):
        kh = _pair_rms(kraw[:, hp * LANES:(hp + 1) * LANES], knw, lane)
        k_ref[0, :, hp * LANES:(hp + 1) * LANES] = kh
        kb_ref[0, :, hp * LANES:(hp + 1) * LANES] = kh.astype(BF16)


def _shared_kv(x, shift, scale, nw, wk, wv, kn_pair, tb):
    B, T, D = x.shape
    tok = lambda b, i: (b, i, 0)
    blk = pl.BlockSpec((1, tb, D), tok)
    return pl.pallas_call(
        _shared_kv_kernel,
        grid=(B, T // tb),
        in_specs=[blk, _mod_spec(shift, tb), _mod_spec(scale, tb), _full_spec(nw),
                  _full_spec(wk), _full_spec(wv), _full_spec(kn_pair)],
        out_specs=[blk, blk, blk, blk],
        out_shape=[jax.ShapeDtypeStruct((B, T, D), F32), jax.ShapeDtypeStruct((B, T, D), F32),
                   jax.ShapeDtypeStruct((B, T, D), BF16), jax.ShapeDtypeStruct((B, T, D), BF16)],
        compiler_params=_cparams(("parallel", "parallel")),
        name="shared_kv",
    )(x, shift, scale, nw, wk, wv, kn_pair)


def _band_pre_kernel(x_ref, sh_ref, sc_ref, nw_ref, wq_ref, qn_ref, q_ref):
    h = _modulate(x_ref[0], nw_ref[...], sh_ref[0], sc_ref[0]).astype(BF16)
    q = jnp.dot(h, wq_ref[...], preferred_element_type=F32)
    qnw = qn_ref[...]
    lane = lax.broadcasted_iota(jnp.int32, (h.shape[0], LANES), 1)
    scale = B_HEAD_DIM ** -0.5
    for hp in range(B_HEADS // 2):
        qh = _pair_rms(q[:, hp * LANES:(hp + 1) * LANES], qnw, lane)
        q_ref[0, :, hp * LANES:(hp + 1) * LANES] = (qh * scale).astype(BF16)


def _band_pre(x, shift, scale, nw, wq, qn_pair, tb):
    B, T, D = x.shape
    tok = lambda b, i: (b, i, 0)
    blk = pl.BlockSpec((1, tb, D), tok)
    return pl.pallas_call(
        _band_pre_kernel,
        grid=(B, T // tb),
        in_specs=[blk, _mod_spec(shift, tb), _mod_spec(scale, tb), _full_spec(nw),
                  _full_spec(wq), _full_spec(qn_pair)],
        out_specs=blk,
        out_shape=jax.ShapeDtypeStruct((B, T, D), BF16),
        compiler_params=_cparams(("parallel", "parallel")),
        name="band_pre",
    )(x, shift, scale, nw, wq, qn_pair)


def _band_bias_kernel(tab_ref, o_ref):
    h = pl.program_id(0)
    nvar = 2 * LANES
    r = lax.broadcasted_iota(jnp.int32, (BAND_SUB, nvar), 0)
    w = lax.broadcasted_iota(jnp.int32, (BAND_SUB, nvar), 1) + (BAND_WIN - nvar)
    idx = jnp.clip(r + B_WINDOW - w, -REL_CLIP, REL_CLIP) + REL_CLIP
    far = tab_ref[h, 2 * REL_CLIP]

    def body(t, acc):
        return jnp.where(idx == t, tab_ref[h, t], acc)

    var = lax.fori_loop(0, 2 * REL_CLIP, body, jnp.full((BAND_SUB, nvar), far, F32))
    full = jnp.concatenate([jnp.full((BAND_SUB, BAND_WIN - nvar), far, F32), var], axis=1)
    rr = lax.broadcasted_iota(jnp.int32, (BAND_SUB, BAND_WIN), 0)
    ww = lax.broadcasted_iota(jnp.int32, (BAND_SUB, BAND_WIN), 1)
    qc = rr // CHUNK + LEFT_CHUNKS
    kc = ww // CHUNK
    allowed = (kc <= qc) & (kc >= qc - LEFT_CHUNKS)
    o_ref[0] = jnp.where(allowed, full, NEG)


def _band_bias(table):
    nh = table.shape[0]
    return pl.pallas_call(
        _band_bias_kernel,
        grid=(nh,),
        in_specs=[pl.BlockSpec(memory_space=pltpu.SMEM)],
        out_specs=pl.BlockSpec((1, BAND_SUB, BAND_WIN), lambda h: (h, 0, 0)),
        out_shape=jax.ShapeDtypeStruct((nh, BAND_SUB, BAND_WIN), F32),
        compiler_params=_cparams(("arbitrary",)),
        name="band_bias",
    )(table)


def _band_sub(qs, kw, vw, bias_ref, col0, lane):
    outs = []
    nkw = kw.shape[0]
    for hh in range(2):
        sel = (lane < B_HEAD_DIM) if hh == 0 else (lane >= B_HEAD_DIM)
        qh = jnp.where(sel, qs, jnp.zeros_like(qs))
        s = lax.dot_general(qh, kw, _NT, preferred_element_type=F32)
        s = s + bias_ref[hh, :, col0:col0 + nkw]
        m = jnp.max(s, axis=-1, keepdims=True)
        p = jnp.exp(s - m)
        l = jnp.sum(p, axis=-1, keepdims=True)
        outs.append(jnp.dot(p.astype(BF16), vw, preferred_element_type=F32) / l)
    return jnp.where(lane < B_HEAD_DIM, outs[0], outs[1])


def _band_attn_kernel(q_ref, kp_ref, kc_ref, vp_ref, vc_ref, bias_ref, o_ref, *, tq):
    i = pl.program_id(2)
    lane = lax.broadcasted_iota(jnp.int32, (BAND_SUB, LANES), 1)
    nsub = tq // BAND_SUB

    def run(first):
        for c in range(nsub):
            qs = q_ref[0, c * BAND_SUB:(c + 1) * BAND_SUB, :]
            hi = (c + 1) * BAND_SUB
            if first or hi >= BAND_WIN:
                lo = max(hi - BAND_WIN, 0)
                kw = kc_ref[0, lo:hi, :]
                vw = vc_ref[0, lo:hi, :]
            else:
                lo = tq - (BAND_WIN - hi)
                kw = jnp.concatenate([kp_ref[0, lo:tq, :], kc_ref[0, 0:hi, :]], axis=0)
                vw = jnp.concatenate([vp_ref[0, lo:tq, :], vc_ref[0, 0:hi, :]], axis=0)
            o = _band_sub(qs, kw, vw, bias_ref, BAND_WIN - kw.shape[0], lane)
            o_ref[0, c * BAND_SUB:(c + 1) * BAND_SUB, :] = o.astype(BF16)

    pl.when(i == 0)(lambda: run(True))
    pl.when(i > 0)(lambda: run(False))


def _band_attn(q, k, v, bias, tq):
    B, T, D = q.shape
    assert tq >= B_WINDOW and T % tq == 0
    cur = lambda b, hp, i: (b, i, hp)
    prev = lambda b, hp, i: (b, jnp.maximum(i - 1, 0), hp)
    blk = lambda m: pl.BlockSpec((1, tq, LANES), m)
    return pl.pallas_call(
        functools.partial(_band_attn_kernel, tq=tq),
        grid=(B, B_HEADS // 2, T // tq),
        in_specs=[blk(cur), blk(prev), blk(cur), blk(prev), blk(cur),
                  pl.BlockSpec((2, BAND_SUB, BAND_WIN), lambda b, hp, i: (hp, 0, 0))],
        out_specs=blk(cur),
        out_shape=jax.ShapeDtypeStruct((B, T, D), BF16),
        compiler_params=_cparams(("parallel", "parallel", "arbitrary")),
        name="band_attn",
    )(q, k, k, v, v, bias)


def _band_step_kernel(q_ref, k_ref, v_ref, bias_ref, o_ref):
    lane = lax.broadcasted_iota(jnp.int32, (BAND_SUB, LANES), 1)
    o = _band_sub(q_ref[0], k_ref[0], v_ref[0], bias_ref, 0, lane)
    o_ref[0] = o.astype(BF16)


def _band_step(q, kwin, vwin, bias):
    B, _, D = q.shape
    return pl.pallas_call(
        _band_step_kernel,
        grid=(B, B_HEADS // 2),
        in_specs=[pl.BlockSpec((1, BAND_SUB, LANES), lambda b, hp: (b, 0, hp)),
                  pl.BlockSpec((1, BAND_WIN, LANES), lambda b, hp: (b, 0, hp)),
                  pl.BlockSpec((1, BAND_WIN, LANES), lambda b, hp: (b, 0, hp)),
                  pl.BlockSpec((2, BAND_SUB, BAND_WIN), lambda b, hp: (hp, 0, 0))],
        out_specs=pl.BlockSpec((1, BAND_SUB, LANES), lambda b, hp: (b, 0, hp)),
        out_shape=jax.ShapeDtypeStruct((B, BAND_SUB, D), BF16),
        compiler_params=_cparams(("parallel", "parallel")),
        name="band_step",
    )(q, kwin, vwin, bias)


def _rope_tables(pos):
    half = ROPE_DIM // 2
    freqs = ROPE_THETA ** (-jnp.arange(half, dtype=F32) / half)
    ang = pos.astype(F32)[:, None] * freqs[None, :]
    c, s = jnp.cos(ang), jnp.sin(ang)
    n = pos.shape[0]
    ones = jnp.ones((n, NOPE_DIM), F32)
    zeros = jnp.zeros((n, NOPE_DIM), F32)
    pad1 = jnp.ones((n, HEAD_PAD - QK_DIM), F32)
    pad0 = jnp.zeros((n, HEAD_PAD - QK_DIM), F32)
    return (jnp.concatenate([ones, c, c, pad1], axis=1),
            jnp.concatenate([zeros, -s, s, pad0], axis=1))


def _pad_heads(w, nheads, width):
    lead = w.shape[:-1]
    w = w.reshape(lead + (nheads, width))
    w = jnp.pad(w, [(0, 0)] * len(lead) + [(0, 0), (0, HEAD_PAD - width)])
    return w.reshape(lead + (nheads * HEAD_PAD,))


def _block(n, pref):
    for t in pref:
        if n % t == 0:
            return t
    return n


def kernel(x_prompt, x_sample, c_prompt, c_sample, cache_a_ckv, cache_a_krope, cache_b_k, cache_b_v, ada_w, ada_b, norm_mix_w, norm_ffn_w, a_w_in, a_q_lora_norm, a_kv_lora_norm, a_w_uq, a_w_ukv, a_q_norm, a_k_norm, a_w_o, kv_ada_w, kv_ada_b, kv_norm_w, b_w_kv, b_k_norm, b_w_q, b_q_norm, b_rel_bias, b_w_o, p_w_q, p_subkeys, p_u, p_v):
    D = D_MODEL
    Bp, Tp, _ = x_prompt.shape
    Bs, Ts, _ = x_sample.shape
    P = cache_a_ckv.shape[2]
    Pb = cache_b_k.shape[1]
    assert Ts == CHUNK and Pb == B_WINDOW and P % CHUNK == 0

    ada_all_w = jnp.concatenate([ada_w[0], ada_w[1], kv_ada_w], axis=1).astype(BF16)
    ada_all_b = jnp.concatenate([ada_b[0], ada_b[1], kv_ada_b])[None, :]
    c_all = jnp.concatenate([c_prompt, c_sample], axis=0)
    mod = _ada(c_all, ada_all_w, ada_all_b)

    w_in = a_w_in[0].astype(BF16)
    qln = a_q_lora_norm[0][None, :]
    kvln = a_kv_lora_norm[0][None, :]
    wuq_pad = _pad_heads(a_w_uq[0], A_HEADS, QK_DIM).astype(BF16)
    wukv = a_w_ukv[0].reshape(KV_LORA, A_HEADS, NOPE_DIM + V_DIM)
    wuk_pad = jnp.pad(wukv[:, :, :NOPE_DIM], ((0, 0), (0, 0), (0, HEAD_PAD - NOPE_DIM))
                      ).reshape(KV_LORA, A_HEADS * HEAD_PAD).astype(BF16)
    wuv = wukv[:, :, NOPE_DIM:].reshape(KV_LORA, A_HEADS * V_DIM).astype(BF16)
    qn_pad = jnp.pad(a_q_norm[0], (0, HEAD_PAD - QK_DIM))[None, :]
    kn_pad = jnp.pad(a_k_norm[0], (0, HEAD_PAD - QK_DIM))[None, :]
    place = jnp.asarray(np.eye(ROPE_DIM, HEAD_PAD, k=NOPE_DIM), BF16)
    a_wo = a_w_o[0].astype(BF16)
    wk_b = b_w_kv[:, :D].astype(BF16)
    wv_b = b_w_kv[:, D:].astype(BF16)
    bkn_pair = jnp.tile(b_k_norm, 2)[None, :]
    bqn_pair = jnp.tile(b_q_norm[0], 2)[None, :]
    bwq = b_w_q[0].astype(BF16)
    bwo = b_w_o[0].astype(BF16)
    bias_tile = _band_bias(b_rel_bias[0])

    def peer_weights(layer):
        wq = p_w_q[layer].reshape(D, P_HEADS, 2 * P_HALF).transpose(1, 0, 2).astype(BF16)
        sk = p_subkeys[layer]
        sk1 = jnp.pad(sk[:, 0], ((0, 0), (0, 0), (0, P_HALF))).astype(BF16)
        sk2 = jnp.pad(sk[:, 1], ((0, 0), (0, 0), (P_HALF, 0))).astype(BF16)
        return wq, sk1, sk2, p_u[layer].astype(BF16), p_v[layer].T.astype(BF16)

    peer_w = [peer_weights(0), peer_weights(1)]
    norm_mix = norm_mix_w[:, None, :]
    norm_ffn = norm_ffn_w[:, None, :]
    kv_nw = kv_norm_w[None, :]

    def run(x, modp, per_token, pos_q, past):
        B, T, _ = x.shape
        if per_token:
            modv = jnp.repeat(modp, T, axis=0)[None]
            xw = x.reshape(1, B * T, D)
        else:
            modv = modp[:, None, :]
            xw = x
        Bw, Tw, _ = xw.shape
        sl = lambda k: modv[:, :, k * D:(k + 1) * D]
        tb = _block(Tw, (512, 256, 128))
        tbr = _block(Tw, (256, 128))
        tbe = _block(Tw, (512, 256, 128))

        cos_q, sin_q = _rope_tables(pos_q)
        ckv, kr, q = _mla_pre(xw, sl(0), sl(1), norm_mix[0], w_in, qln, kvln, wuq_pad, qn_pad,
                              cos_q, sin_q, tb)
        ckv = ckv.reshape(B, T, KV_LORA)
        kr = kr.reshape(B, T, ROPE_DIM)
        q = q.reshape(B, T, A_HEADS * HEAD_PAD)
        if past is None:
            ckv_all, kr_all = ckv, kr
            pos_k = pos_q[:T]
        else:
            ckv_all = jnp.concatenate([past[0], ckv], axis=1)
            kr_all = jnp.concatenate([past[1], kr], axis=1)
            pos_k = jnp.arange(P + T, dtype=jnp.int32)
        S = ckv_all.shape[1]
        cos_k, sin_k = _rope_tables(pos_k)
        tbk = _block(S, (512, 704, 256, 192, 64))
        k, v = _mla_kv(ckv_all, kr_all, wuk_pad, wuv, kn_pad, place, cos_k, sin_k, tbk)
        if past is None:
            ta = _block(T, (512, 256, 128, 64))
            o = _mla_attn(q, k, v, ta, ta, True)
        else:
            o = _mla_attn(q, k, v, T, S, False)
        o = o.reshape(Bw, Tw, A_HEADS * V_DIM)

        wq, sk1, sk2, u, vt = peer_w[0]
        x1, h2, rank, cnt, e2, r = _post_mix(o, xw, sl(2), sl(3), sl(4), norm_ffn[0], a_wo, wq, sk1, sk2, tbr)
        x2 = _peer_dense(h2, u, vt, rank, cnt, e2, r, x1, sl(5), tbe)

        kf, vf, kb, vb = _shared_kv(x2, sl(12), sl(13), kv_nw, wk_b, wv_b, bkn_pair, tb)

        qb = _band_pre(x2, sl(6), sl(7), norm_mix[1], bwq, bqn_pair, tb)
        if past is None:
            ob = _band_attn(qb, kb, vb, bias_tile, B_WINDOW)
            new_bk = kf[:, -B_WINDOW:].reshape(B, B_WINDOW, B_HEADS, B_HEAD_DIM)
            new_bv = vf[:, -B_WINDOW:].reshape(B, B_WINDOW, B_HEADS, B_HEAD_DIM)
        else:
            zq = jnp.zeros((B, CHUNK, D), BF16)
            qpad = jnp.concatenate([zq, qb.reshape(B, T, D)], axis=1)
            kwin = jnp.concatenate([zq, past[2].reshape(B, Pb, D).astype(BF16), kb.reshape(B, T, D)], axis=1)
            vwin = jnp.concatenate([zq, past[3].reshape(B, Pb, D).astype(BF16), vb.reshape(B, T, D)], axis=1)
            ob = _band_step(qpad, kwin, vwin, bias_tile)[:, CHUNK:].reshape(Bw, Tw, D)
            new_bk = jnp.concatenate([past[2], kf.reshape(B, T, B_HEADS, B_HEAD_DIM)], axis=1)[:, -Pb:]
            new_bv = jnp.concatenate([past[3], vf.reshape(B, T, B_HEADS, B_HEAD_DIM)], axis=1)[:, -Pb:]

        wq, sk1, sk2, u, vt = peer_w[1]
        x3, h4, rank, cnt, e2, r = _post_mix(ob, x2, sl(8), sl(9), sl(10), norm_ffn[1], bwo, wq, sk1, sk2, tbr)
        y = _peer_dense(h4, u, vt, rank, cnt, e2, r, x3, sl(11), tbe)
        return y.reshape(B, T, D), ckv[None], kr[None], new_bk, new_bv

    pos_p = jnp.arange(Tp, dtype=jnp.int32)
    pos_s = jnp.tile(P + jnp.arange(Ts, dtype=jnp.int32), Bs)
    y_p, p_ckv, p_kr, p_bk, p_bv = run(x_prompt, mod[:Bp], False, pos_p, None)
    y_s, s_ckv, s_kr, s_bk, s_bv = run(x_sample, mod[Bp:], True, pos_s,
                                       (cache_a_ckv[0], cache_a_krope[0], cache_b_k, cache_b_v))
    return (y_p, y_s, p_ckv, p_kr, p_bk, p_bv, s_ckv, s_kr, s_bk, s_bv)
```

```python
import functools
import math

import numpy as np
import jax
import jax.numpy as jnp
from jax import lax
from jax.experimental import pallas as pl
from jax.experimental.pallas import tpu as pltpu

F32 = jnp.float32
BF16 = jnp.bfloat16

D_MODEL = 1024
CHUNK = 64
A_HEADS = 16
Q_LORA = 384
KV_LORA = 256
NOPE_DIM = 64
ROPE_DIM = 32
V_DIM = 64
QK_DIM = NOPE_DIM + ROPE_DIM
ROPE_THETA = 10000.0
B_HEADS = 16
B_HEAD_DIM = 64
LEFT_CHUNKS = 8
B_WINDOW = LEFT_CHUNKS * CHUNK
REL_CLIP = 128
P_HEADS = 8
N_KEYS = 128
N_EXPERTS = N_KEYS * N_KEYS
P_HALF = 64
P_TOPK = 16
NEG = -1e30
EPS = 1e-6

LANES = 128
BF16_ROWS = 16
HEAD_PAD = 128
BAND_SUB = 2 * CHUNK
BAND_WIN = B_WINDOW + BAND_SUB
VMEM_LIMIT = 56 * 1024 * 1024
PEER_EC = 512
PEER_SPLIT = 2
MXU_N = 256
GATE_LANES = 256

_NT = (((1,), (1,)), ((), ()))


def _cparams(sem):
    return pltpu.CompilerParams(dimension_semantics=sem, vmem_limit_bytes=VMEM_LIMIT)


def _modulate(x, w, shift, scale):
    ms = jnp.mean(x * x, axis=-1, keepdims=True)
    y = x * lax.rsqrt(ms + EPS) * w
    return y * (1.0 + scale) + shift


def _rms(x, w):
    ms = jnp.mean(x * x, axis=-1, keepdims=True)
    return x * lax.rsqrt(ms + EPS) * w


def _mod_spec(arr, tb):
    if arr.shape[1] == 1:
        return pl.BlockSpec((1, 1, arr.shape[2]), lambda b, i, *_: (b, 0, 0))
    return pl.BlockSpec((1, tb, arr.shape[2]), lambda b, i, *_: (b, i, 0))


def _full_spec(arr):
    nd = arr.ndim
    return pl.BlockSpec(arr.shape, lambda *_: (0,) * nd)


def _ada_kernel(c_ref, w_ref, b_ref, o_ref):
    c = c_ref[...]
    a = c / (1.0 + jnp.exp(-c))
    o_ref[...] = jnp.dot(a.astype(BF16), w_ref[...], preferred_element_type=F32) + b_ref[...]


def _ada(c, w, b, tn=2048):
    m, k = c.shape
    n = w.shape[1]
    return pl.pallas_call(
        _ada_kernel,
        grid=(n // tn,),
        in_specs=[pl.BlockSpec((m, k), lambda j: (0, 0)),
                  pl.BlockSpec((k, tn), lambda j: (0, j)),
                  pl.BlockSpec((1, tn), lambda j: (0, j))],
        out_specs=pl.BlockSpec((m, tn), lambda j: (0, j)),
        out_shape=jax.ShapeDtypeStruct((m, n), F32),
        compiler_params=_cparams(("arbitrary",)),
        name="ada_mod",
    )(c, w, b)


def _rope_head(xh, cos, sin, lane):
    rolled = jnp.where(lane < NOPE_DIM + ROPE_DIM // 2,
                       pltpu.roll(xh, LANES - ROPE_DIM // 2, 1),
                       pltpu.roll(xh, ROPE_DIM // 2, 1))
    return xh * cos + rolled * sin


def _mla_pre_kernel(x_ref, sh_ref, sc_ref, nw_ref, win_ref, qln_ref, kvln_ref, wuq_ref, qn_ref,
                    cos_ref, sin_ref, ckv_ref, kr_ref, q_ref):
    x = x_ref[0]
    h = _modulate(x, nw_ref[...], sh_ref[0], sc_ref[0])
    proj = jnp.dot(h.astype(BF16), win_ref[...], preferred_element_type=F32)
    c_q = _rms(proj[:, :Q_LORA], qln_ref[...])
    ckv_ref[0] = _rms(proj[:, Q_LORA:Q_LORA + KV_LORA], kvln_ref[...])
    kr_ref[0] = proj[:, Q_LORA + KV_LORA:]
    q = jnp.dot(c_q.astype(BF16), wuq_ref[...], preferred_element_type=F32)
    cos = cos_ref[...]
    sin = sin_ref[...]
    qn = qn_ref[...]
    lane = lax.broadcasted_iota(jnp.int32, cos.shape, 1)
    scale = QK_DIM ** -0.5
    for hd in range(A_HEADS):
        qh = q[:, hd * HEAD_PAD:(hd + 1) * HEAD_PAD]
        ss = jnp.sum(qh * qh, axis=-1, keepdims=True) * (1.0 / QK_DIM)
        qh = qh * lax.rsqrt(ss + EPS) * qn
        qh = _rope_head(qh, cos, sin, lane)
        q_ref[0, :, hd * HEAD_PAD:(hd + 1) * HEAD_PAD] = (qh * scale).astype(BF16)


def _mla_pre(x, shift, scale, nw, w_in, qln, kvln, wuq_pad, qn_pad, cos_t, sin_t, tb):
    B, T, D = x.shape
    nT = T // tb
    tab_spec = pl.BlockSpec((tb, LANES), lambda b, i: (i, 0))
    return pl.pallas_call(
        _mla_pre_kernel,
        grid=(B, nT),
        in_specs=[pl.BlockSpec((1, tb, D), lambda b, i: (b, i, 0)),
                  _mod_spec(shift, tb), _mod_spec(scale, tb),
                  _full_spec(nw), _full_spec(w_in), _full_spec(qln), _full_spec(kvln),
                  _full_spec(wuq_pad), _full_spec(qn_pad), tab_spec, tab_spec],
        out_specs=[pl.BlockSpec((1, tb, KV_LORA), lambda b, i: (b, i, 0)),
                   pl.BlockSpec((1, tb, ROPE_DIM), lambda b, i: (b, i, 0)),
                   pl.BlockSpec((1, tb, A_HEADS * HEAD_PAD), lambda b, i: (b, i, 0))],
        out_shape=[jax.ShapeDtypeStruct((B, T, KV_LORA), F32),
                   jax.ShapeDtypeStruct((B, T, ROPE_DIM), F32),
                   jax.ShapeDtypeStruct((B, T, A_HEADS * HEAD_PAD), BF16)],
        compiler_params=_cparams(("parallel", "parallel")),
        name="mla_pre",
    )(x, shift, scale, nw, w_in, qln, kvln, wuq_pad, qn_pad, cos_t, sin_t)


def _mla_kv_kernel(ckv_ref, kr_ref, wuk_ref, wuv_ref, vone_ref, kn_ref, place_ref, cos_ref, sin_ref,
                   k_ref, v_ref):
    ckv = ckv_ref[0].astype(BF16)
    kn = jnp.dot(ckv, wuk_ref[...], preferred_element_type=F32)
    v_ref[0] = (jnp.dot(ckv, wuv_ref[...], preferred_element_type=F32) + vone_ref[...]).astype(BF16)
    kr = kr_ref[0]
    kr_hi = kr.astype(BF16)
    kr_lo = (kr - kr_hi.astype(F32)).astype(BF16)
    place = place_ref[...]
    krp = (jnp.dot(kr_hi, place, preferred_element_type=F32)
           + jnp.dot(kr_lo, place, preferred_element_type=F32))
    cos = cos_ref[...]
    sin = sin_ref[...]
    knw = kn_ref[...]
    lane = lax.broadcasted_iota(jnp.int32, cos.shape, 1)
    for hd in range(A_HEADS):
        kh = kn[:, hd * HEAD_PAD:(hd + 1) * HEAD_PAD] + krp
        ss = jnp.sum(kh * kh, axis=-1, keepdims=True) * (1.0 / QK_DIM)
        kh = kh * lax.rsqrt(ss + EPS) * knw
        kh = _rope_head(kh, cos, sin, lane)
        k_ref[0, :, hd * HEAD_PAD:(hd + 1) * HEAD_PAD] = kh.astype(BF16)


def _mla_kv(ckv, kr, wuk_pad, wuv_pad, vone, kn_pad, place, cos_t, sin_t, tb):
    B, S, _ = ckv.shape
    tab_spec = pl.BlockSpec((tb, LANES), lambda b, i: (i, 0))
    wide = pl.BlockSpec((1, tb, A_HEADS * HEAD_PAD), lambda b, i: (b, i, 0))
    return pl.pallas_call(
        _mla_kv_kernel,
        grid=(B, S // tb),
        in_specs=[pl.BlockSpec((1, tb, KV_LORA), lambda b, i: (b, i, 0)),
                  pl.BlockSpec((1, tb, ROPE_DIM), lambda b, i: (b, i, 0)),
                  _full_spec(wuk_pad), _full_spec(wuv_pad), _full_spec(vone), _full_spec(kn_pad),
                  _full_spec(place), tab_spec, tab_spec],
        out_specs=[wide, wide],
        out_shape=[jax.ShapeDtypeStruct((B, S, A_HEADS * HEAD_PAD), BF16),
                   jax.ShapeDtypeStruct((B, S, A_HEADS * HEAD_PAD), BF16)],
        compiler_params=_cparams(("parallel", "parallel")),
        name="mla_kv",
    )(ckv, kr, wuk_pad, wuv_pad, vone, kn_pad, place, cos_t, sin_t)


def _mla_attn_kernel(q_ref, k_ref, v_ref, o_ref, m_sc, acc_sc, *, tq, tk, causal, nk):
    i = pl.program_id(2)
    m_sc[...] = jnp.full(m_sc.shape, -jnp.inf, F32)
    acc_sc[...] = jnp.zeros(acc_sc.shape, F32)

    def block(start, masked):
        k = k_ref[0, pl.ds(start, tk), :]
        v = v_ref[0, pl.ds(start, tk), :]
        if masked:
            qc = lax.broadcasted_iota(jnp.int32, (tq, tk), 0) // CHUNK
            kc = lax.broadcasted_iota(jnp.int32, (tq, tk), 1) // CHUNK
            allowed = kc <= qc
        for hh in range(2):
            qh = q_ref[0, :, hh * HEAD_PAD:(hh + 1) * HEAD_PAD]
            s = lax.dot_general(qh, k[:, hh * HEAD_PAD:(hh + 1) * HEAD_PAD], _NT,
                                preferred_element_type=F32)
            if masked:
                s = jnp.where(allowed, s, NEG)
            m_prev = m_sc[hh]
            m_new = jnp.maximum(m_prev, jnp.max(s, axis=-1, keepdims=True))
            alpha = jnp.exp(m_prev - m_new)
            if tk % LANES == 0:
                m_wide = jnp.concatenate([m_new] * (tk // LANES), axis=1)
            else:
                m_wide = m_new[:, :1]
            p = jnp.exp((s - m_wide).astype(BF16))
            acc_sc[hh] = alpha * acc_sc[hh] + jnp.dot(p, v[:, hh * HEAD_PAD:(hh + 1) * HEAD_PAD],
                                                      preferred_element_type=F32)
            m_sc[hh] = m_new

    def body(j, carry):
        block(pl.multiple_of(j * tk, tk), False)
        return carry

    if causal:
        lax.fori_loop(0, i, body, 0)
        block(pl.multiple_of(i * tk, tk), True)
    else:
        lax.fori_loop(0, nk, body, 0)

    lane = lax.broadcasted_iota(jnp.int32, (tq, LANES), 1)
    outs = []
    for hh in range(2):
        acc = acc_sc[hh]
        outs.append(acc / acc[:, V_DIM:V_DIM + 1])
    o = jnp.where(lane < V_DIM, outs[0], pltpu.roll(outs[1], V_DIM, 1))
    o_ref[0] = o.astype(BF16)


def _mla_attn(q, k, v, tq, tk, causal):
    B, T, _ = q.shape
    S = k.shape[1]
    nq, nk = T // tq, S // tk
    assert T % tq == 0 and S % tk == 0 and (not causal or (tq == tk and T == S))
    kv_spec = pl.BlockSpec((1, S, 2 * HEAD_PAD), lambda b, hp, i: (b, 0, hp))
    return pl.pallas_call(
        functools.partial(_mla_attn_kernel, tq=tq, tk=tk, causal=causal, nk=nk),
        grid=(B, A_HEADS // 2, nq),
        in_specs=[pl.BlockSpec((1, tq, 2 * HEAD_PAD), lambda b, hp, i: (b, i, hp)), kv_spec, kv_spec],
        out_specs=pl.BlockSpec((1, tq, 2 * V_DIM), lambda b, hp, i: (b, i, hp)),
        out_shape=jax.ShapeDtypeStruct((B, T, A_HEADS * V_DIM), BF16),
        scratch_shapes=[pltpu.VMEM((2, tq, LANES), F32), pltpu.VMEM((2, tq, LANES), F32)],
        compiler_params=_cparams(("parallel", "parallel", "arbitrary")),
        name="mla_attn",
    )(q, k, v)


def _top16_rows(chains):
    works = [s for s, _, _ in chains]
    for k in range(P_TOPK):
        for c, (_, t_sc, cols) in enumerate(chains):
            m = jnp.max(works[c], axis=0, keepdims=True)
            t_sc[k:k + 1, cols] = m
            if k + 1 < P_TOPK:
                works[c] = jnp.where(works[c] == m, -jnp.inf, works[c])


def _route_tiles(tiles, t1_sc, t2_sc):
    _top16_rows([(s1, t1_sc, cols) for s1, _, cols in tiles] + [(s2, t2_sc, cols) for _, s2, cols in tiles])
    state = []
    for s1, s2, cols in tiles:
        row8 = lax.broadcasted_iota(jnp.int32, (8, s1.shape[1]), 0)
        t1 = [t1_sc[k:k + 1, cols] for k in range(P_TOPK)]
        t2 = [t2_sc[k:k + 1, cols] for k in range(P_TOPK)]
        t2_lo = t2_sc[0:8, cols]
        t2_hi = t2_sc[8:16, cols]
        t1_hi = t1_sc[8:16, cols]
        cands = [t1[0] + t2_lo, t1[0] + t2_hi, t1_hi + t2[0]]
        for k1 in range(1, 8):
            lim = P_TOPK // (k1 + 1)
            c = t1[k1] + t2_lo
            cands.append(c if lim >= 8 else jnp.where(row8 < lim, c, -jnp.inf))
        top = t1[0] + t2[0]
        state.append(dict(t1=t1, t2=t2, cands=cands, top=top, z=jnp.zeros_like(top), tau=top))
    for k in range(P_TOPK):
        for st in state:
            m = st["cands"][0]
            for c in st["cands"][1:]:
                m = jnp.maximum(m, c)
            m = jnp.max(m, axis=0, keepdims=True)
            st["z"] = st["z"] + jnp.exp(m - st["top"])
            st["tau"] = m
            if k + 1 < P_TOPK:
                st["cands"] = [jnp.where(c == m, -jnp.inf, c) for c in st["cands"]]
    outs = []
    for (s1, s2, cols), st in zip(tiles, state):
        cnt = jnp.zeros_like(s1)
        rank = jnp.zeros_like(s2)
        for k in range(P_TOPK):
            cnt = jnp.where(s1 + st["t2"][k] >= st["tau"], float(k + 1), cnt)
            rank = jnp.where(st["t2"][k] > s2, float(k + 1), rank)
        outs.append((rank, cnt, jnp.exp(s2 - st["t2"][0]), jnp.exp(s1 - st["t1"][0]) / st["z"]))
    return outs


def _post_mix_kernel(o_ref, x_ref, g_ref, sh_ref, sc_ref, nw_ref, wo_ref, wq_ref, sk1_ref, sk2_ref,
                     x1_ref, h2_ref, rank_ref, cnt_ref, e2_ref, r_ref, t1_sc, t2_sc, s_sc):
    mix = jnp.dot(o_ref[0], wo_ref[...], preferred_element_type=F32)
    x1 = x_ref[0] + g_ref[0] * mix
    x1_ref[0] = x1
    h2 = _modulate(x1, nw_ref[...], sh_ref[0], sc_ref[0]).astype(BF16)
    h2_ref[0] = h2
    tb = h2.shape[0]
    q = jnp.dot(h2, wq_ref[...], preferred_element_type=F32).astype(BF16)
    for h in range(P_HEADS):
        qh = q[:, h * 2 * P_HALF:(h + 1) * 2 * P_HALF]
        s_sc[h, 0] = lax.dot_general(sk1_ref[h], qh, _NT, preferred_element_type=F32)
        s_sc[h, 1] = lax.dot_general(sk2_ref[h], qh, _NT, preferred_element_type=F32)

    def head_body(h, carry):
        s1 = s_sc[h, 0]
        s2 = s_sc[h, 1]
        lane_tiles = [slice(nt * LANES, (nt + 1) * LANES) for nt in range(tb // LANES)]
        routed = _route_tiles([(s1[:, cols], s2[:, cols], cols) for cols in lane_tiles], t1_sc, t2_sc)
        for cols, (rank, cnt, e2, r) in zip(lane_tiles, routed):
            rank_ref[h, :, cols] = rank.astype(BF16)
            cnt_ref[h, :, cols] = cnt
            e2_ref[h, :, cols] = e2.astype(BF16)
            r_ref[h, :, cols] = r
        return carry

    lax.fori_loop(0, P_HEADS, head_body, 0)


def _post_mix(o, x, gate, shift, scale, nw, w_o, wq_heads, sk1, sk2, tb):
    B, T, D = x.shape
    nT = T // tb
    n = B * T
    tok = lambda b, i: (b, i, 0)
    rt_spec = pl.BlockSpec((P_HEADS, N_KEYS, tb), lambda b, i: (0, 0, b * nT + i))
    rt_shape = jax.ShapeDtypeStruct((P_HEADS, N_KEYS, n), F32)
    rt_shape_b = jax.ShapeDtypeStruct((P_HEADS, N_KEYS, n), BF16)
    return pl.pallas_call(
        _post_mix_kernel,
        grid=(B, nT),
        in_specs=[pl.BlockSpec((1, tb, o.shape[2]), tok), pl.BlockSpec((1, tb, D), tok),
                  _mod_spec(gate, tb), _mod_spec(shift, tb), _mod_spec(scale, tb),
                  _full_spec(nw), _full_spec(w_o), _full_spec(wq_heads), _full_spec(sk1), _full_spec(sk2)],
        out_specs=[pl.BlockSpec((1, tb, D), tok), pl.BlockSpec((1, tb, D), tok),
                   rt_spec, rt_spec, rt_spec, rt_spec],
        out_shape=[jax.ShapeDtypeStruct((B, T, D), F32), jax.ShapeDtypeStruct((B, T, D), BF16),
                   rt_shape_b, rt_shape, rt_shape_b, rt_shape],
        scratch_shapes=[pltpu.VMEM((P_TOPK, tb), F32), pltpu.VMEM((P_TOPK, tb), F32),
                        pltpu.VMEM((P_HEADS, 2, N_KEYS, tb), F32)],
        compiler_params=_cparams(("parallel", "parallel")),
        name="post_mix_route",
    )(o, x, gate, shift, scale, nw, w_o, wq_heads, sk1, sk2)


def _gelu_tanh(a):
    c = math.sqrt(2.0 / math.pi)
    return 0.5 * a * (1.0 + jnp.tanh(c * (a + 0.044715 * (a * a * a))))


def _peer_dense_kernel(h_ref, u_ref, vt_ref, rank_ref, cnt_ref, e2_ref, r_ref, x_ref, g_ref,
                       o_ref, acc_ref, a0_ref, a1_ref, hc0_ref, hc1_ref, *, ec, ne):
    s = pl.program_id(2)
    per = ec // N_KEYS

    @pl.when(s == 0)
    def _():
        acc_ref[...] = jnp.zeros(acc_ref.shape, F32)
        for ref in (a0_ref, a1_ref, hc0_ref, hc1_ref):
            ref[...] = jnp.zeros(ref.shape, ref.dtype)

    def stages(a_out, a_in, hc_out, hc_in):
        zero = jnp.zeros((), BF16)
        tb = h_ref.shape[1]
        nsplit = PEER_SPLIT
        gate_w = min(GATE_LANES, tb)
        mxu_w = min(MXU_N, tb)
        assert tb % gate_w == 0 and tb % mxu_w == 0

        row_cache = {}

        def routed_rows(ii):
            if ii not in row_cache:
                i1 = jnp.clip((s - 1) * per + ii, 0, N_KEYS - 1)
                row_cache[ii] = [(cnt_ref[h, pl.ds(i1, 1), :].astype(BF16),
                                  r_ref[h, pl.ds(i1, 1), :].astype(BF16)) for h in range(P_HEADS)]
            return row_cache[ii]

        def gate_piece(ii, n):
            cols = slice(n * gate_w, (n + 1) * gate_w)
            rows = slice(ii * N_KEYS, (ii + 1) * N_KEYS)
            w = None
            for h, (cnt_row, r_row) in enumerate(routed_rows(ii)):
                contrib = jnp.where(rank_ref[h, :, cols] < cnt_row[:, cols], e2_ref[h, :, cols], zero) * r_row[:, cols]
                w = contrib if w is None else w + contrib
            hc_out[rows, cols] = _gelu_tanh(a_in[rows, cols]) * w

        def score_piece(q, n):
            cols = slice(n * mxu_w, (n + 1) * mxu_w)
            rows = slice(q * (ec // nsplit), (q + 1) * (ec // nsplit))
            a_out[rows, cols] = lax.dot_general(u_ref[rows, :], h_ref[0, cols, :], _NT,
                                                preferred_element_type=F32).astype(BF16)

        def out_piece(q, n):
            cols = slice(n * mxu_w, (n + 1) * mxu_w)
            rows = slice(q * (D_MODEL // nsplit), (q + 1) * (D_MODEL // nsplit))
            acc_ref[rows, cols] += jnp.dot(vt_ref[rows, :], hc_in[:, cols], preferred_element_type=F32)

        vec = [functools.partial(gate_piece, ii, n) for n in range(tb // gate_w) for ii in range(per)]
        mxu = []
        for n in range(tb // mxu_w):
            for q in range(nsplit):
                mxu.append(functools.partial(score_piece, q, n))
                mxu.append(functools.partial(out_piece, q, n))
        for k in range(max(len(vec), len(mxu))):
            if k < len(mxu):
                mxu[k]()
            if k < len(vec):
                vec[k]()

    even = lax.rem(s, 2) == 0
    pl.when(even)(lambda: stages(a0_ref, a1_ref, hc1_ref, hc0_ref))
    pl.when(jnp.logical_not(even))(lambda: stages(a1_ref, a0_ref, hc0_ref, hc1_ref))

    @pl.when(s == ne + 1)
    def _():
        o_ref[0] = x_ref[0] + g_ref[0] * acc_ref[...].T


def _peer_dense(h2, u, vt, rank, cnt, e2, r, x, gate, tb, ec=PEER_EC):
    B, T, D = x.shape
    nT = T // tb
    ne = N_EXPERTS // ec
    tok = lambda b, i, s: (b, i, 0)
    rt_spec = pl.BlockSpec((P_HEADS, N_KEYS, tb), lambda b, i, s: (0, 0, b * nT + i))
    return pl.pallas_call(
        functools.partial(_peer_dense_kernel, ec=ec, ne=ne),
        grid=(B, nT, ne + 2),
        in_specs=[pl.BlockSpec((1, tb, D), tok),
                  pl.BlockSpec((ec, D), lambda b, i, s: (jnp.minimum(s, ne - 1), 0)),
                  pl.BlockSpec((D, ec), lambda b, i, s: (0, jnp.clip(s - 2, 0, ne - 1))),
                  rt_spec, rt_spec, rt_spec, rt_spec,
                  pl.BlockSpec((1, tb, D), tok), _mod_spec(gate, tb)],
        out_specs=pl.BlockSpec((1, tb, D), tok),
        out_shape=jax.ShapeDtypeStruct((B, T, D), F32),
        scratch_shapes=[pltpu.VMEM((D, tb), F32),
                        pltpu.VMEM((ec, tb), BF16), pltpu.VMEM((ec, tb), BF16),
                        pltpu.VMEM((ec, tb), BF16), pltpu.VMEM((ec, tb), BF16)],
        compiler_params=_cparams(("parallel", "parallel", "arbitrary")),
        name="peer_dense",
    )(h2, u, vt, rank, cnt, e2, r, x, gate)


def _pair_rms(y, w, lane):
    y2 = y * y
    lo = jnp.sum(jnp.where(lane < B_HEAD_DIM, y2, 0.0), axis=-1, keepdims=True)
    hi = jnp.sum(y2, axis=-1, keepdims=True) - lo
    ms = jnp.where(lane < B_HEAD_DIM, lo, hi) * (1.0 / B_HEAD_DIM)
    return y * lax.rsqrt(ms + EPS) * w


def _shared_kv_kernel(x_ref, sh_ref, sc_ref, nw_ref, wk_ref, wv_ref, kn_ref, k_ref, v_ref, kb_ref, vb_ref):
    h = _modulate(x_ref[0], nw_ref[...], sh_ref[0], sc_ref[0]).astype(BF16)
    kraw = jnp.dot(h, wk_ref[...], preferred_element_type=F32)
    v = jnp.dot(h, wv_ref[...], preferred_element_type=F32)
    v_ref[0] = v
    vb_ref[0] = v.astype(BF16)
    knw = kn_ref[...]
    lane = lax.broadcasted_iota(jnp.int32, (h.shape[0], LANES), 1)
    for hp in range(B_HEADS // 2):
        kh = _pair_rms(kraw[:, hp * LANES:(hp + 1) * LANES], knw, lane)
        k_ref[0, :, hp * LANES:(hp + 1) * LANES] = kh
        kb_ref[0, :, hp * LANES:(hp + 1) * LANES] = kh.astype(BF16)


def _shared_kv(x, shift, scale, nw, wk, wv, kn_pair, tb):
    B, T, D = x.shape
    tok = lambda b, i: (b, i, 0)
    blk = pl.BlockSpec((1, tb, D), tok)
    return pl.pallas_call(
        _shared_kv_kernel,
        grid=(B, T // tb),
        in_specs=[blk, _mod_spec(shift, tb), _mod_spec(scale, tb), _full_spec(nw),
                  _full_spec(wk), _full_spec(wv), _full_spec(kn_pair)],
        out_specs=[blk, blk, blk, blk],
        out_shape=[jax.ShapeDtypeStruct((B, T, D), F32), jax.ShapeDtypeStruct((B, T, D), F32),
                   jax.ShapeDtypeStruct((B, T, D), BF16), jax.ShapeDtypeStruct((B, T, D), BF16)],
        compiler_params=_cparams(("parallel", "parallel")),
        name="shared_kv",
    )(x, shift, scale, nw, wk, wv, kn_pair)


def _band_pre_kernel(x_ref, sh_ref, sc_ref, nw_ref, wq_ref, qn_ref, q_ref):
    h = _modulate(x_ref[0], nw_ref[...], sh_ref[0], sc_ref[0]).astype(BF16)
    q = jnp.dot(h, wq_ref[...], preferred_element_type=F32)
    qnw = qn_ref[...]
    lane = lax.broadcasted_iota(jnp.int32, (h.shape[0], LANES), 1)
    scale = B_HEAD_DIM ** -0.5
    for hp in range(B_HEADS // 2):
        qh = _pair_rms(q[:, hp * LANES:(hp + 1) * LANES], qnw, lane)
        q_ref[0, :, hp * LANES:(hp + 1) * LANES] = (qh * scale).astype(BF16)


def _band_pre(x, shift, scale, nw, wq, qn_pair, tb):
    B, T, D = x.shape
    tok = lambda b, i: (b, i, 0)
    blk = pl.BlockSpec((1, tb, D), tok)
    return pl.pallas_call(
        _band_pre_kernel,
        grid=(B, T // tb),
        in_specs=[blk, _mod_spec(shift, tb), _mod_spec(scale, tb), _full_spec(nw),
                  _full_spec(wq), _full_spec(qn_pair)],
        out_specs=blk,
        out_shape=jax.ShapeDtypeStruct((B, T, D), BF16),
        compiler_params=_cparams(("parallel", "parallel")),
        name="band_pre",
    )(x, shift, scale, nw, wq, qn_pair)


def _band_bias_kernel(tab_ref, o_ref):
    h = pl.program_id(0)
    nvar = 2 * LANES
    r = lax.broadcasted_iota(jnp.int32, (BAND_SUB, nvar), 0)
    w = lax.broadcasted_iota(jnp.int32, (BAND_SUB, nvar), 1) + (BAND_WIN - nvar)
    idx = jnp.clip(r + B_WINDOW - w, -REL_CLIP, REL_CLIP) + REL_CLIP
    far = tab_ref[h, 2 * REL_CLIP]

    def body(t, acc):
        return jnp.where(idx == t, tab_ref[h, t], acc)

    var = lax.fori_loop(0, 2 * REL_CLIP, body, jnp.full((BAND_SUB, nvar), far, F32))
    full = jnp.concatenate([jnp.full((BAND_SUB, BAND_WIN - nvar), far, F32), var], axis=1)
    rr = lax.broadcasted_iota(jnp.int32, (BAND_SUB, BAND_WIN), 0)
    ww = lax.broadcasted_iota(jnp.int32, (BAND_SUB, BAND_WIN), 1)
    qc = rr // CHUNK + LEFT_CHUNKS
    kc = ww // CHUNK
    allowed = (kc <= qc) & (kc >= qc - LEFT_CHUNKS)
    o_ref[0] = jnp.where(allowed, full, NEG)


def _band_bias(table):
    nh = table.shape[0]
    return pl.pallas_call(
        _band_bias_kernel,
        grid=(nh,),
        in_specs=[pl.BlockSpec(memory_space=pltpu.SMEM)],
        out_specs=pl.BlockSpec((1, BAND_SUB, BAND_WIN), lambda h: (h, 0, 0)),
        out_shape=jax.ShapeDtypeStruct((nh, BAND_SUB, BAND_WIN), F32),
        compiler_params=_cparams(("arbitrary",)),
        name="band_bias",
    )(table)


def _band_sub(qs, kw, vw, bias_ref, col0, lane):
    outs = []
    nkw = kw.shape[0]
    for hh in range(2):
        sel = (lane < B_HEAD_DIM) if hh == 0 else (lane >= B_HEAD_DIM)
        qh = jnp.where(sel, qs, jnp.zeros_like(qs))
        s = lax.dot_general(qh, kw, _NT, preferred_element_type=F32)
        s = s + bias_ref[hh, :, col0:col0 + nkw]
        m = jnp.max(s, axis=-1, keepdims=True)
        p = jnp.exp(s - m)
        l = jnp.sum(p, axis=-1, keepdims=True)
        outs.append(jnp.dot(p.astype(BF16), vw, preferred_element_type=F32) / l)
    return jnp.where(lane < B_HEAD_DIM, outs[0], outs[1])


def _band_attn_kernel(q_ref, kp_ref, kc_ref, vp_ref, vc_ref, bias_ref, o_ref, *, tq):
    i = pl.program_id(2)
    lane = lax.broadcasted_iota(jnp.int32, (BAND_SUB, LANES), 1)
    nsub = tq // BAND_SUB

    def run(first):
        for c in range(nsub):
            qs = q_ref[0, c * BAND_SUB:(c + 1) * BAND_SUB, :]
            hi = (c + 1) * BAND_SUB
            if first or hi >= BAND_WIN:
                lo = max(hi - BAND_WIN, 0)
                kw = kc_ref[0, lo:hi, :]
                vw = vc_ref[0, lo:hi, :]
            else:
                lo = tq - (BAND_WIN - hi)
                kw = jnp.concatenate([kp_ref[0, lo:tq, :], kc_ref[0, 0:hi, :]], axis=0)
                vw = jnp.concatenate([vp_ref[0, lo:tq, :], vc_ref[0, 0:hi, :]], axis=0)
            o = _band_sub(qs, kw, vw, bias_ref, BAND_WIN - kw.shape[0], lane)
            o_ref[0, c * BAND_SUB:(c + 1) * BAND_SUB, :] = o.astype(BF16)

    pl.when(i == 0)(lambda: run(True))
    pl.when(i > 0)(lambda: run(False))


def _band_attn(q, k, v, bias, tq):
    B, T, D = q.shape
    assert tq >= B_WINDOW and T % tq == 0
    cur = lambda b, hp, i: (b, i, hp)
    prev = lambda b, hp, i: (b, jnp.maximum(i - 1, 0), hp)
    blk = lambda m: pl.BlockSpec((1, tq, LANES), m)
    return pl.pallas_call(
        functools.partial(_band_attn_kernel, tq=tq),
        grid=(B, B_HEADS // 2, T // tq),
        in_specs=[blk(cur), blk(prev), blk(cur), blk(prev), blk(cur),
                  pl.BlockSpec((2, BAND_SUB, BAND_WIN), lambda b, hp, i: (hp, 0, 0))],
        out_specs=blk(cur),
        out_shape=jax.ShapeDtypeStruct((B, T, D), BF16),
        compiler_params=_cparams(("parallel", "parallel", "arbitrary")),
        name="band_attn",
    )(q, k, k, v, v, bias)


def _band_step_kernel(q_ref, k_ref, v_ref, bias_ref, o_ref):
    lane = lax.broadcasted_iota(jnp.int32, (BAND_SUB, LANES), 1)
    o = _band_sub(q_ref[0], k_ref[0], v_ref[0], bias_ref, 0, lane)
    o_ref[0] = o.astype(BF16)


def _band_step(q, kwin, vwin, bias):
    B, _, D = q.shape
    return pl.pallas_call(
        _band_step_kernel,
        grid=(B, B_HEADS // 2),
        in_specs=[pl.BlockSpec((1, BAND_SUB, LANES), lambda b, hp: (b, 0, hp)),
                  pl.BlockSpec((1, BAND_WIN, LANES), lambda b, hp: (b, 0, hp)),
                  pl.BlockSpec((1, BAND_WIN, LANES), lambda b, hp: (b, 0, hp)),
                  pl.BlockSpec((2, BAND_SUB, BAND_WIN), lambda b, hp: (hp, 0, 0))],
        out_specs=pl.BlockSpec((1, BAND_SUB, LANES), lambda b, hp: (b, 0, hp)),
        out_shape=jax.ShapeDtypeStruct((B, BAND_SUB, D), BF16),
        compiler_params=_cparams(("parallel", "parallel")),
        name="band_step",
    )(q, kwin, vwin, bias)


def _rope_tables(pos):
    half = ROPE_DIM // 2
    freqs = ROPE_THETA ** (-jnp.arange(half, dtype=F32) / half)
    ang = pos.astype(F32)[:, None] * freqs[None, :]
    c, s = jnp.cos(ang), jnp.sin(ang)
    n = pos.shape[0]
    ones = jnp.ones((n, NOPE_DIM), F32)
    zeros = jnp.zeros((n, NOPE_DIM), F32)
    pad1 = jnp.ones((n, HEAD_PAD - QK_DIM), F32)
    pad0 = jnp.zeros((n, HEAD_PAD - QK_DIM), F32)
    return (jnp.concatenate([ones, c, c, pad1], axis=1),
            jnp.concatenate([zeros, -s, s, pad0], axis=1))


def _pad_heads(w, nheads, width):
    lead = w.shape[:-1]
    w = w.reshape(lead + (nheads, width))
    w = jnp.pad(w, [(0, 0)] * len(lead) + [(0, 0), (0, HEAD_PAD - width)])
    return w.reshape(lead + (nheads * HEAD_PAD,))


def _block(n, pref):
    for t in pref:
        if n % t == 0:
            return t
    return n


def kernel(x_prompt, x_sample, c_prompt, c_sample, cache_a_ckv, cache_a_krope, cache_b_k, cache_b_v, ada_w, ada_b, norm_mix_w, norm_ffn_w, a_w_in, a_q_lora_norm, a_kv_lora_norm, a_w_uq, a_w_ukv, a_q_norm, a_k_norm, a_w_o, kv_ada_w, kv_ada_b, kv_norm_w, b_w_kv, b_k_norm, b_w_q, b_q_norm, b_rel_bias, b_w_o, p_w_q, p_subkeys, p_u, p_v):
    D = D_MODEL
    Bp, Tp, _ = x_prompt.shape
    Bs, Ts, _ = x_sample.shape
    P = cache_a_ckv.shape[2]
    Pb = cache_b_k.shape[1]
    assert Ts == CHUNK and Pb == B_WINDOW and P % CHUNK == 0

    ada_all_w = jnp.concatenate([ada_w[0], ada_w[1], kv_ada_w], axis=1).astype(BF16)
    ada_all_b = jnp.concatenate([ada_b[0], ada_b[1], kv_ada_b])[None, :]
    c_all = jnp.concatenate([c_prompt, c_sample], axis=0)
    mod = _ada(c_all, ada_all_w, ada_all_b)

    w_in = a_w_in[0].astype(BF16)
    qln = a_q_lora_norm[0][None, :]
    kvln = a_kv_lora_norm[0][None, :]
    wuq_pad = _pad_heads(a_w_uq[0], A_HEADS, QK_DIM).astype(BF16)
    wukv = a_w_ukv[0].reshape(KV_LORA, A_HEADS, NOPE_DIM + V_DIM)
    wuk_pad = jnp.pad(wukv[:, :, :NOPE_DIM], ((0, 0), (0, 0), (0, HEAD_PAD - NOPE_DIM))
                      ).reshape(KV_LORA, A_HEADS * HEAD_PAD).astype(BF16)
    wuv_pad = jnp.pad(wukv[:, :, NOPE_DIM:], ((0, 0), (0, 0), (0, HEAD_PAD - V_DIM))
                      ).reshape(KV_LORA, A_HEADS * HEAD_PAD).astype(BF16)
    vone = jnp.asarray(np.tile(np.eye(1, HEAD_PAD, k=V_DIM), (1, A_HEADS)), F32)
    qn_pad = jnp.pad(a_q_norm[0], (0, HEAD_PAD - QK_DIM))[None, :]
    kn_pad = jnp.pad(a_k_norm[0], (0, HEAD_PAD - QK_DIM))[None, :]
    place = jnp.asarray(np.eye(ROPE_DIM, HEAD_PAD, k=NOPE_DIM), BF16)
    a_wo = a_w_o[0].astype(BF16)
    wk_b = b_w_kv[:, :D].astype(BF16)
    wv_b = b_w_kv[:, D:].astype(BF16)
    bkn_pair = jnp.tile(b_k_norm, 2)[None, :]
    bqn_pair = jnp.tile(b_q_norm[0], 2)[None, :]
    bwq = b_w_q[0].astype(BF16)
    bwo = b_w_o[0].astype(BF16)
    bias_tile = _band_bias(b_rel_bias[0])

    def peer_weights(layer):
        wq = p_w_q[layer].astype(BF16)
        sk = p_subkeys[layer]
        sk1 = jnp.pad(sk[:, 0], ((0, 0), (0, 0), (0, P_HALF))).astype(BF16)
        sk2 = jnp.pad(sk[:, 1], ((0, 0), (0, 0), (P_HALF, 0))).astype(BF16)
        return wq, sk1, sk2, p_u[layer].astype(BF16), p_v[layer].T.astype(BF16)

    peer_w = [peer_weights(0), peer_weights(1)]
    norm_mix = norm_mix_w[:, None, :]
    norm_ffn = norm_ffn_w[:, None, :]
    kv_nw = kv_norm_w[None, :]

    def run(x, modp, per_token, pos_q, past):
        B, T, _ = x.shape
        if per_token:
            modv = jnp.repeat(modp, T, axis=0)[None]
            xw = x.reshape(1, B * T, D)
        else:
            modv = modp[:, None, :]
            xw = x
        Bw, Tw, _ = xw.shape
        sl = lambda k: modv[:, :, k * D:(k + 1) * D]
        tb = _block(Tw, (512, 256, 128))
        tbr = _block(Tw, (256, 128))
        tbe = _block(Tw, (512, 256, 128))

        cos_q, sin_q = _rope_tables(pos_q)
        ckv, kr, q = _mla_pre(xw, sl(0), sl(1), norm_mix[0], w_in, qln, kvln, wuq_pad, qn_pad,
                              cos_q, sin_q, tb)
        ckv = ckv.reshape(B, T, KV_LORA)
        kr = kr.reshape(B, T, ROPE_DIM)
        q = q.reshape(B, T, A_HEADS * HEAD_PAD)
        if past is None:
            ckv_all, kr_all = ckv, kr
            pos_k = pos_q[:T]
        else:
            ckv_all = jnp.concatenate([past[0], ckv], axis=1)
            kr_all = jnp.concatenate([past[1], kr], axis=1)
            pos_k = jnp.arange(P + T, dtype=jnp.int32)
        S = ckv_all.shape[1]
        cos_k, sin_k = _rope_tables(pos_k)
        tbk = _block(S, (512, 704, 256, 192, 64))
        k, v = _mla_kv(ckv_all, kr_all, wuk_pad, wuv_pad, vone, kn_pad, place, cos_k, sin_k, tbk)
        if past is None:
            ta = _block(T, (512, 256, 128, 64))
            o = _mla_attn(q, k, v, ta, ta, True)
        else:
            o = _mla_attn(q, k, v, T, tbk, False)
        o = o.reshape(Bw, Tw, A_HEADS * V_DIM)

        wq, sk1, sk2, u, vt = peer_w[0]
        x1, h2, rank, cnt, e2, r = _post_mix(o, xw, sl(2), sl(3), sl(4), norm_ffn[0], a_wo, wq, sk1, sk2, tbr)
        x2 = _peer_dense(h2, u, vt, rank, cnt, e2, r, x1, sl(5), tbe)

        kf, vf, kb, vb = _shared_kv(x2, sl(12), sl(13), kv_nw, wk_b, wv_b, bkn_pair, tb)

        qb = _band_pre(x2, sl(6), sl(7), norm_mix[1], bwq, bqn_pair, tb)
        if past is None:
            ob = _band_attn(qb, kb, vb, bias_tile, B_WINDOW)
            new_bk = kf[:, -B_WINDOW:].reshape(B, B_WINDOW, B_HEADS, B_HEAD_DIM)
            new_bv = vf[:, -B_WINDOW:].reshape(B, B_WINDOW, B_HEADS, B_HEAD_DIM)
        else:
            zq = jnp.zeros((B, CHUNK, D), BF16)
            qpad = jnp.concatenate([zq, qb.reshape(B, T, D)], axis=1)
            kwin = jnp.concatenate([zq, past[2].reshape(B, Pb, D).astype(BF16), kb.reshape(B, T, D)], axis=1)
            vwin = jnp.concatenate([zq, past[3].reshape(B, Pb, D).astype(BF16), vb.reshape(B, T, D)], axis=1)
            ob = _band_step(qpad, kwin, vwin, bias_tile)[:, CHUNK:].reshape(Bw, Tw, D)
            new_bk = jnp.concatenate([past[2], kf.reshape(B, T, B_HEADS, B_HEAD_DIM)], axis=1)[:, -Pb:]
            new_bv = jnp.concatenate([past[3], vf.reshape(B, T, B_HEADS, B_HEAD_DIM)], axis=1)[:, -Pb:]

        wq, sk1, sk2, u, vt = peer_w[1]
        x3, h4, rank, cnt, e2, r = _post_mix(ob, x2, sl(8), sl(9), sl(10), norm_ffn[1], bwo, wq, sk1, sk2, tbr)
        y = _peer_dense(h4, u, vt, rank, cnt, e2, r, x3, sl(11), tbe)
        return y.reshape(B, T, D), ckv[None], kr[None], new_bk, new_bv

    pos_p = jnp.arange(Tp, dtype=jnp.int32)
    pos_s = jnp.tile(P + jnp.arange(Ts, dtype=jnp.int32), Bs)
    y_p, p_ckv, p_kr, p_bk, p_bv = run(x_prompt, mod[:Bp], False, pos_p, None)
    y_s, s_ckv, s_kr, s_bk, s_bv = run(x_sample, mod[Bp:], True, pos_s,
                                       (cache_a_ckv[0], cache_a_krope[0], cache_b_k, cache_b_v))
    return (y_p, y_s, p_ckv, p_kr, p_bk, p_bv, s_ckv, s_kr, s_bk, s_bv)
```

```python
import functools
import math

import numpy as np
import jax
import jax.numpy as jnp
from jax import lax
from jax.experimental import pallas as pl
from jax.experimental.pallas import tpu as pltpu

F32 = jnp.float32
BF16 = jnp.bfloat16

D_MODEL = 1024
CHUNK = 64
A_HEADS = 16
Q_LORA = 384
KV_LORA = 256
NOPE_DIM = 64
ROPE_DIM = 32
V_DIM = 64
QK_DIM = NOPE_DIM + ROPE_DIM
ROPE_THETA = 10000.0
B_HEADS = 16
B_HEAD_DIM = 64
LEFT_CHUNKS = 8
B_WINDOW = LEFT_CHUNKS * CHUNK
REL_CLIP = 128
P_HEADS = 8
N_KEYS = 128
N_EXPERTS = N_KEYS * N_KEYS
P_HALF = 64
P_TOPK = 16
NEG = -1e30
EPS = 1e-6

LANES = 128
BF16_ROWS = 16
HEAD_PAD = 128
BAND_SUB = 2 * CHUNK
BAND_WIN = B_WINDOW + BAND_SUB
VMEM_LIMIT = 56 * 1024 * 1024
PEER_EC = 512
PEER_SPLIT = 2
MXU_N = 256
GATE_LANES = 256

_NT = (((1,), (1,)), ((), ()))


def _cparams(sem):
    return pltpu.CompilerParams(dimension_semantics=sem, vmem_limit_bytes=VMEM_LIMIT)


def _modulate(x, w, shift, scale):
    ms = jnp.mean(x * x, axis=-1, keepdims=True)
    y = x * lax.rsqrt(ms + EPS) * w
    return y * (1.0 + scale) + shift


def _rms(x, w):
    ms = jnp.mean(x * x, axis=-1, keepdims=True)
    return x * lax.rsqrt(ms + EPS) * w


def _mod_spec(arr, tb):
    if arr.shape[1] == 1:
        return pl.BlockSpec((1, 1, arr.shape[2]), lambda b, i, *_: (b, 0, 0))
    return pl.BlockSpec((1, tb, arr.shape[2]), lambda b, i, *_: (b, i, 0))


def _full_spec(arr):
    nd = arr.ndim
    return pl.BlockSpec(arr.shape, lambda *_: (0,) * nd)


def _ada_kernel(c_ref, w_ref, b_ref, o_ref):
    c = c_ref[...]
    a = c / (1.0 + jnp.exp(-c))
    o_ref[...] = jnp.dot(a.astype(BF16), w_ref[...], preferred_element_type=F32) + b_ref[...]


def _ada(c, w, b, tn=2048):
    m, k = c.shape
    n = w.shape[1]
    return pl.pallas_call(
        _ada_kernel,
        grid=(n // tn,),
        in_specs=[pl.BlockSpec((m, k), lambda j: (0, 0)),
                  pl.BlockSpec((k, tn), lambda j: (0, j)),
                  pl.BlockSpec((1, tn), lambda j: (0, j))],
        out_specs=pl.BlockSpec((m, tn), lambda j: (0, j)),
        out_shape=jax.ShapeDtypeStruct((m, n), F32),
        compiler_params=_cparams(("arbitrary",)),
        name="ada_mod",
    )(c, w, b)


def _rope_head(xh, cos, sin, lane):
    rolled = jnp.where(lane < NOPE_DIM + ROPE_DIM // 2,
                       pltpu.roll(xh, LANES - ROPE_DIM // 2, 1),
                       pltpu.roll(xh, ROPE_DIM // 2, 1))
    return xh * cos + rolled * sin


def _mla_pre_kernel(x_ref, sh_ref, sc_ref, nw_ref, win_ref, qln_ref, kvln_ref, wuq_ref, qn_ref,
                    cos_ref, sin_ref, ckv_ref, kr_ref, q_ref):
    x = x_ref[0]
    h = _modulate(x, nw_ref[...], sh_ref[0], sc_ref[0])
    proj = jnp.dot(h.astype(BF16), win_ref[...], preferred_element_type=F32)
    c_q = _rms(proj[:, :Q_LORA], qln_ref[...])
    ckv_ref[0] = _rms(proj[:, Q_LORA:Q_LORA + KV_LORA], kvln_ref[...])
    kr_ref[0] = proj[:, Q_LORA + KV_LORA:]
    q = jnp.dot(c_q.astype(BF16), wuq_ref[...], preferred_element_type=F32)
    cos = cos_ref[...]
    sin = sin_ref[...]
    qn = qn_ref[...]
    lane = lax.broadcasted_iota(jnp.int32, cos.shape, 1)
    scale = QK_DIM ** -0.5
    for hd in range(A_HEADS):
        qh = q[:, hd * HEAD_PAD:(hd + 1) * HEAD_PAD]
        ss = jnp.sum(qh * qh, axis=-1, keepdims=True) * (1.0 / QK_DIM)
        qh = qh * lax.rsqrt(ss + EPS) * qn
        qh = _rope_head(qh, cos, sin, lane)
        q_ref[0, :, hd * HEAD_PAD:(hd + 1) * HEAD_PAD] = (qh * scale).astype(BF16)


def _mla_pre(x, shift, scale, nw, w_in, qln, kvln, wuq_pad, qn_pad, cos_t, sin_t, tb):
    B, T, D = x.shape
    nT = T // tb
    tab_spec = pl.BlockSpec((tb, LANES), lambda b, i: (i, 0))
    return pl.pallas_call(
        _mla_pre_kernel,
        grid=(B, nT),
        in_specs=[pl.BlockSpec((1, tb, D), lambda b, i: (b, i, 0)),
                  _mod_spec(shift, tb), _mod_spec(scale, tb),
                  _full_spec(nw), _full_spec(w_in), _full_spec(qln), _full_spec(kvln),
                  _full_spec(wuq_pad), _full_spec(qn_pad), tab_spec, tab_spec],
        out_specs=[pl.BlockSpec((1, tb, KV_LORA), lambda b, i: (b, i, 0)),
                   pl.BlockSpec((1, tb, ROPE_DIM), lambda b, i: (b, i, 0)),
                   pl.BlockSpec((1, tb, A_HEADS * HEAD_PAD), lambda b, i: (b, i, 0))],
        out_shape=[jax.ShapeDtypeStruct((B, T, KV_LORA), F32),
                   jax.ShapeDtypeStruct((B, T, ROPE_DIM), F32),
                   jax.ShapeDtypeStruct((B, T, A_HEADS * HEAD_PAD), BF16)],
        compiler_params=_cparams(("parallel", "parallel")),
        name="mla_pre",
    )(x, shift, scale, nw, w_in, qln, kvln, wuq_pad, qn_pad, cos_t, sin_t)


def _mla_kv_kernel(ckv_ref, kr_ref, wuk_ref, wuv_ref, vone_ref, kn_ref, place_ref, cos_ref, sin_ref,
                   k_ref, v_ref):
    ckv = ckv_ref[0].astype(BF16)
    kn = jnp.dot(ckv, wuk_ref[...], preferred_element_type=F32)
    v_ref[0] = (jnp.dot(ckv, wuv_ref[...], preferred_element_type=F32) + vone_ref[...]).astype(BF16)
    kr = kr_ref[0]
    kr_hi = kr.astype(BF16)
    kr_lo = (kr - kr_hi.astype(F32)).astype(BF16)
    place = place_ref[...]
    krp = (jnp.dot(kr_hi, place, preferred_element_type=F32)
           + jnp.dot(kr_lo, place, preferred_element_type=F32))
    cos = cos_ref[...]
    sin = sin_ref[...]
    knw = kn_ref[...]
    lane = lax.broadcasted_iota(jnp.int32, cos.shape, 1)
    for hd in range(A_HEADS):
        kh = kn[:, hd * HEAD_PAD:(hd + 1) * HEAD_PAD] + krp
        ss = jnp.sum(kh * kh, axis=-1, keepdims=True) * (1.0 / QK_DIM)
        kh = kh * lax.rsqrt(ss + EPS) * knw
        kh = _rope_head(kh, cos, sin, lane)
        k_ref[0, :, hd * HEAD_PAD:(hd + 1) * HEAD_PAD] = kh.astype(BF16)


def _mla_kv(ckv, kr, wuk_pad, wuv_pad, vone, kn_pad, place, cos_t, sin_t, tb):
    B, S, _ = ckv.shape
    tab_spec = pl.BlockSpec((tb, LANES), lambda b, i: (i, 0))
    wide = pl.BlockSpec((1, tb, A_HEADS * HEAD_PAD), lambda b, i: (b, i, 0))
    return pl.pallas_call(
        _mla_kv_kernel,
        grid=(B, S // tb),
        in_specs=[pl.BlockSpec((1, tb, KV_LORA), lambda b, i: (b, i, 0)),
                  pl.BlockSpec((1, tb, ROPE_DIM), lambda b, i: (b, i, 0)),
                  _full_spec(wuk_pad), _full_spec(wuv_pad), _full_spec(vone), _full_spec(kn_pad),
                  _full_spec(place), tab_spec, tab_spec],
        out_specs=[wide, wide],
        out_shape=[jax.ShapeDtypeStruct((B, S, A_HEADS * HEAD_PAD), BF16),
                   jax.ShapeDtypeStruct((B, S, A_HEADS * HEAD_PAD), BF16)],
        compiler_params=_cparams(("parallel", "parallel")),
        name="mla_kv",
    )(ckv, kr, wuk_pad, wuv_pad, vone, kn_pad, place, cos_t, sin_t)


def _mla_attn_kernel(q_ref, k_ref, v_ref, o_ref, m_sc, acc_sc, *, tq, tk, causal, nk):
    i = pl.program_id(2)
    m_sc[...] = jnp.full(m_sc.shape, -jnp.inf, F32)
    acc_sc[...] = jnp.zeros(acc_sc.shape, F32)

    def block(start, masked):
        k = k_ref[0, pl.ds(start, tk), :]
        v = v_ref[0, pl.ds(start, tk), :]
        if masked:
            qc = lax.broadcasted_iota(jnp.int32, (tq, tk), 0) // CHUNK
            kc = lax.broadcasted_iota(jnp.int32, (tq, tk), 1) // CHUNK
            allowed = kc <= qc
        for hh in range(2):
            qh = q_ref[0, :, hh * HEAD_PAD:(hh + 1) * HEAD_PAD]
            s = lax.dot_general(qh, k[:, hh * HEAD_PAD:(hh + 1) * HEAD_PAD], _NT,
                                preferred_element_type=F32)
            if masked:
                s = jnp.where(allowed, s, NEG)
            m_prev = m_sc[hh]
            m_new = jnp.maximum(m_prev, jnp.max(s, axis=-1, keepdims=True))
            alpha = jnp.exp(m_prev - m_new)
            if tk % LANES == 0:
                m_wide = jnp.concatenate([m_new] * (tk // LANES), axis=1)
            else:
                m_wide = m_new[:, :1]
            p = jnp.exp((s - m_wide).astype(BF16))
            acc_sc[hh] = alpha * acc_sc[hh] + jnp.dot(p, v[:, hh * HEAD_PAD:(hh + 1) * HEAD_PAD],
                                                      preferred_element_type=F32)
            m_sc[hh] = m_new

    def body(j, carry):
        block(pl.multiple_of(j * tk, tk), False)
        return carry

    if causal:
        lax.fori_loop(0, i, body, 0)
        block(pl.multiple_of(i * tk, tk), True)
    else:
        lax.fori_loop(0, nk, body, 0)

    lane = lax.broadcasted_iota(jnp.int32, (tq, LANES), 1)
    outs = []
    for hh in range(2):
        acc = acc_sc[hh]
        outs.append(acc / acc[:, V_DIM:V_DIM + 1])
    o = jnp.where(lane < V_DIM, outs[0], pltpu.roll(outs[1], V_DIM, 1))
    o_ref[0] = o.astype(BF16)


def _mla_attn(q, k, v, tq, tk, causal):
    B, T, _ = q.shape
    S = k.shape[1]
    nq, nk = T // tq, S // tk
    assert T % tq == 0 and S % tk == 0 and (not causal or (tq == tk and T == S))
    kv_spec = pl.BlockSpec((1, S, 2 * HEAD_PAD), lambda b, hp, i: (b, 0, hp))
    return pl.pallas_call(
        functools.partial(_mla_attn_kernel, tq=tq, tk=tk, causal=causal, nk=nk),
        grid=(B, A_HEADS // 2, nq),
        in_specs=[pl.BlockSpec((1, tq, 2 * HEAD_PAD), lambda b, hp, i: (b, i, hp)), kv_spec, kv_spec],
        out_specs=pl.BlockSpec((1, tq, 2 * V_DIM), lambda b, hp, i: (b, i, hp)),
        out_shape=jax.ShapeDtypeStruct((B, T, A_HEADS * V_DIM), BF16),
        scratch_shapes=[pltpu.VMEM((2, tq, LANES), F32), pltpu.VMEM((2, tq, LANES), F32)],
        compiler_params=_cparams(("parallel", "parallel", "arbitrary")),
        name="mla_attn",
    )(q, k, v)


def _top16_rows(chains):
    works = [s for s, _, _ in chains]
    for k in range(P_TOPK):
        for c, (_, t_sc, cols) in enumerate(chains):
            m = jnp.max(works[c], axis=0, keepdims=True)
            t_sc[k:k + 1, cols] = m
            if k + 1 < P_TOPK:
                works[c] = jnp.where(works[c] == m, -jnp.inf, works[c])


def _route_tiles(tiles, t1_sc, t2_sc):
    _top16_rows([(s1, t1_sc, cols) for s1, _, cols in tiles] + [(s2, t2_sc, cols) for _, s2, cols in tiles])
    state = []
    for s1, s2, cols in tiles:
        row8 = lax.broadcasted_iota(jnp.int32, (8, s1.shape[1]), 0)
        t1 = [t1_sc[k:k + 1, cols] for k in range(P_TOPK)]
        t2 = [t2_sc[k:k + 1, cols] for k in range(P_TOPK)]
        t2_lo = t2_sc[0:8, cols]
        t2_hi = t2_sc[8:16, cols]
        t1_hi = t1_sc[8:16, cols]
        cands = [t1[0] + t2_lo, t1[0] + t2_hi, t1_hi + t2[0]]
        for k1 in range(1, 8):
            lim = P_TOPK // (k1 + 1)
            c = t1[k1] + t2_lo
            cands.append(c if lim >= 8 else jnp.where(row8 < lim, c, -jnp.inf))
        top = t1[0] + t2[0]
        state.append(dict(t1=t1, t2=t2, cands=cands, top=top, z=jnp.zeros_like(top), tau=top))
    for k in range(P_TOPK):
        for st in state:
            m = st["cands"][0]
            for c in st["cands"][1:]:
                m = jnp.maximum(m, c)
            m = jnp.max(m, axis=0, keepdims=True)
            st["z"] = st["z"] + jnp.exp(m - st["top"])
            st["tau"] = m
            if k + 1 < P_TOPK:
                st["cands"] = [jnp.where(c == m, -jnp.inf, c) for c in st["cands"]]
    outs = []
    for (s1, s2, cols), st in zip(tiles, state):
        cnt = jnp.zeros_like(s1)
        rank = jnp.zeros_like(s2)
        for k in range(P_TOPK):
            cnt = jnp.where(s1 + st["t2"][k] >= st["tau"], float(k + 1), cnt)
            rank = jnp.where(st["t2"][k] > s2, float(k + 1), rank)
        n1 = jnp.sum(jnp.where(s1 >= st["t1"][P_TOPK - 1], 1.0, 0.0), axis=0, keepdims=True)
        n2 = jnp.sum(jnp.where(s2 >= st["t2"][P_TOPK - 1], 1.0, 0.0), axis=0, keepdims=True)
        npair = jnp.sum(cnt, axis=0, keepdims=True)
        k = float(P_TOPK)
        tied = jnp.where((n1 != k) | (n2 != k) | (npair != k), 1.0, 0.0)
        outs.append((rank, cnt, jnp.exp(s2 - st["t2"][0]), jnp.exp(s1 - st["t1"][0]) / st["z"], tied))
    return outs


def _route_tile_exact(s1, s2, t1_sc, t2_sc, cols):
    nk, w = s1.shape
    key = lax.broadcasted_iota(jnp.int32, (nk, w), 0).astype(F32)

    def top16(s, t_sc):
        work, rank, tops = s, jnp.full(s.shape, float(P_TOPK), F32), []
        for k in range(P_TOPK):
            m = jnp.max(work, axis=0, keepdims=True)
            first = jnp.min(jnp.where(work == m, key, float(nk)), axis=0, keepdims=True)
            hit = key == first
            rank = jnp.where(hit, float(k), rank)
            work = jnp.where(hit, -jnp.inf, work)
            t_sc[k:k + 1, cols] = m
            tops.append(m)
        return tops, rank

    t1, rank1 = top16(s1, t1_sc)
    t2, rank2 = top16(s2, t2_sc)
    row8 = lax.broadcasted_iota(jnp.int32, (8, w), 0).astype(F32)
    t2_lo = t2_sc[0:8, cols]
    t2_hi = t2_sc[8:16, cols]
    t1_hi = t1_sc[8:16, cols]
    cands = [(t1[0] + t2_lo, row8), (t1[0] + t2_hi, row8 + 8.0), (t1_hi + t2[0], (row8 + 8.0) * P_TOPK)]
    for k1 in range(1, 8):
        lim = P_TOPK // (k1 + 1)
        c = t1[k1] + t2_lo
        cands.append((c if lim >= 8 else jnp.where(row8 < lim, c, -jnp.inf), row8 + float(k1 * P_TOPK)))
    top = t1[0] + t2[0]
    z = jnp.zeros_like(top)
    k1_row = lax.broadcasted_iota(jnp.int32, (P_TOPK, w), 0).astype(F32)
    cnt_k1 = jnp.zeros((P_TOPK, w), F32)
    for k in range(P_TOPK):
        m = cands[0][0]
        for c, _ in cands[1:]:
            m = jnp.maximum(m, c)
        m = jnp.max(m, axis=0, keepdims=True)
        first = None
        for c, pos in cands:
            f = jnp.min(jnp.where(c == m, pos, float(P_TOPK * P_TOPK)), axis=0, keepdims=True)
            first = f if first is None else jnp.minimum(first, f)
        z = z + jnp.exp(m - top)
        cnt_k1 = cnt_k1 + jnp.where(k1_row == jnp.floor(first * (1.0 / P_TOPK)), 1.0, 0.0)
        cands = [(jnp.where(pos == first, -jnp.inf, c), pos) for c, pos in cands]
    t1_sc[:, cols] = cnt_k1
    cnt = jnp.zeros_like(s1)
    for k1 in range(P_TOPK):
        cnt = jnp.where(rank1 == float(k1), t1_sc[k1:k1 + 1, cols], cnt)
    return rank2, cnt, jnp.exp(s1 - t1[0]) / z


def _post_mix_kernel(o_ref, x_ref, g_ref, sh_ref, sc_ref, nw_ref, wo_ref, wq_ref, sk1_ref, sk2_ref,
                     x1_ref, h2_ref, rank_ref, cnt_ref, e2_ref, r_ref, t1_sc, t2_sc, s_sc, tied_sc):
    mix = jnp.dot(o_ref[0], wo_ref[...], preferred_element_type=F32)
    x1 = x_ref[0] + g_ref[0] * mix
    x1_ref[0] = x1
    h2 = _modulate(x1, nw_ref[...], sh_ref[0], sc_ref[0]).astype(BF16)
    h2_ref[0] = h2
    tb = h2.shape[0]
    q = jnp.dot(h2, wq_ref[...], preferred_element_type=F32).astype(BF16)
    for h in range(P_HEADS):
        qh = q[:, h * 2 * P_HALF:(h + 1) * 2 * P_HALF]
        s_sc[h, 0] = lax.dot_general(sk1_ref[h], qh, _NT, preferred_element_type=F32)
        s_sc[h, 1] = lax.dot_general(sk2_ref[h], qh, _NT, preferred_element_type=F32)

    def head_body(h, carry):
        s1 = s_sc[h, 0]
        s2 = s_sc[h, 1]
        lane_tiles = [slice(nt * LANES, (nt + 1) * LANES) for nt in range(tb // LANES)]
        routed = _route_tiles([(s1[:, cols], s2[:, cols], cols) for cols in lane_tiles], t1_sc, t2_sc)
        for cols, (rank, cnt, e2, r, tied) in zip(lane_tiles, routed):
            rank_ref[h, :, cols] = rank.astype(BF16)
            cnt_ref[h, :, cols] = cnt
            e2_ref[h, :, cols] = e2.astype(BF16)
            r_ref[h, :, cols] = r
            tied_sc[h, :, cols] = jnp.broadcast_to(tied, (8, tied.shape[1]))
        return carry

    lax.fori_loop(0, P_HEADS, head_body, 0)

    @pl.when(jnp.max(tied_sc[...]) > 0.0)
    def _():
        def fix_head(h, carry):
            for nt in range(tb // LANES):
                cols = slice(nt * LANES, (nt + 1) * LANES)

                @pl.when(jnp.max(tied_sc[h, :, cols]) > 0.0)
                def _(cols=cols):
                    rank_x, cnt_x, r_x = _route_tile_exact(s_sc[h, 0, :, cols], s_sc[h, 1, :, cols],
                                                           t1_sc, t2_sc, cols)
                    rank_ref[h, :, cols] = rank_x.astype(BF16)
                    cnt_ref[h, :, cols] = cnt_x
                    r_ref[h, :, cols] = r_x
            return carry

        lax.fori_loop(0, P_HEADS, fix_head, 0)


def _post_mix(o, x, gate, shift, scale, nw, w_o, wq_heads, sk1, sk2, tb):
    B, T, D = x.shape
    nT = T // tb
    n = B * T
    tok = lambda b, i: (b, i, 0)
    rt_spec = pl.BlockSpec((P_HEADS, N_KEYS, tb), lambda b, i: (0, 0, b * nT + i))
    rt_shape = jax.ShapeDtypeStruct((P_HEADS, N_KEYS, n), F32)
    rt_shape_b = jax.ShapeDtypeStruct((P_HEADS, N_KEYS, n), BF16)
    return pl.pallas_call(
        _post_mix_kernel,
        grid=(B, nT),
        in_specs=[pl.BlockSpec((1, tb, o.shape[2]), tok), pl.BlockSpec((1, tb, D), tok),
                  _mod_spec(gate, tb), _mod_spec(shift, tb), _mod_spec(scale, tb),
                  _full_spec(nw), _full_spec(w_o), _full_spec(wq_heads), _full_spec(sk1), _full_spec(sk2)],
        out_specs=[pl.BlockSpec((1, tb, D), tok), pl.BlockSpec((1, tb, D), tok),
                   rt_spec, rt_spec, rt_spec, rt_spec],
        out_shape=[jax.ShapeDtypeStruct((B, T, D), F32), jax.ShapeDtypeStruct((B, T, D), BF16),
                   rt_shape_b, rt_shape, rt_shape_b, rt_shape],
        scratch_shapes=[pltpu.VMEM((P_TOPK, tb), F32), pltpu.VMEM((P_TOPK, tb), F32),
                        pltpu.VMEM((P_HEADS, 2, N_KEYS, tb), F32), pltpu.VMEM((P_HEADS, 8, tb), F32)],
        compiler_params=_cparams(("parallel", "parallel")),
        name="post_mix_route",
    )(o, x, gate, shift, scale, nw, w_o, wq_heads, sk1, sk2)


def _gelu_tanh(a):
    c = math.sqrt(2.0 / math.pi)
    return 0.5 * a * (1.0 + jnp.tanh(c * (a + 0.044715 * (a * a * a))))


def _peer_dense_kernel(h_ref, u_ref, vt_ref, rank_ref, cnt_ref, e2_ref, r_ref, x_ref, g_ref,
                       o_ref, acc_ref, a0_ref, a1_ref, hc0_ref, hc1_ref, *, ec, ne):
    s = pl.program_id(2)
    per = ec // N_KEYS

    @pl.when(s == 0)
    def _():
        acc_ref[...] = jnp.zeros(acc_ref.shape, F32)
        for ref in (a0_ref, a1_ref, hc0_ref, hc1_ref):
            ref[...] = jnp.zeros(ref.shape, ref.dtype)

    def stages(a_out, a_in, hc_out, hc_in):
        zero = jnp.zeros((), BF16)
        tb = h_ref.shape[1]
        nsplit = PEER_SPLIT
        gate_w = min(GATE_LANES, tb)
        mxu_w = min(MXU_N, tb)
        assert tb % gate_w == 0 and tb % mxu_w == 0

        row_cache = {}

        def routed_rows(ii):
            if ii not in row_cache:
                i1 = jnp.clip((s - 1) * per + ii, 0, N_KEYS - 1)
                row_cache[ii] = [(cnt_ref[h, pl.ds(i1, 1), :].astype(BF16),
                                  r_ref[h, pl.ds(i1, 1), :].astype(BF16)) for h in range(P_HEADS)]
            return row_cache[ii]

        def gate_piece(ii, n):
            cols = slice(n * gate_w, (n + 1) * gate_w)
            rows = slice(ii * N_KEYS, (ii + 1) * N_KEYS)
            w = None
            for h, (cnt_row, r_row) in enumerate(routed_rows(ii)):
                contrib = jnp.where(rank_ref[h, :, cols] < cnt_row[:, cols], e2_ref[h, :, cols], zero) * r_row[:, cols]
                w = contrib if w is None else w + contrib
            hc_out[rows, cols] = _gelu_tanh(a_in[rows, cols]) * w

        def score_piece(q, n):
            cols = slice(n * mxu_w, (n + 1) * mxu_w)
            rows = slice(q * (ec // nsplit), (q + 1) * (ec // nsplit))
            a_out[rows, cols] = lax.dot_general(u_ref[rows, :], h_ref[0, cols, :], _NT,
                                                preferred_element_type=F32).astype(BF16)

        def out_piece(q, n):
            cols = slice(n * mxu_w, (n + 1) * mxu_w)
            rows = slice(q * (D_MODEL // nsplit), (q + 1) * (D_MODEL // nsplit))
            acc_ref[rows, cols] += jnp.dot(vt_ref[0, rows, :], hc_in[:, cols], preferred_element_type=F32)

        vec = [functools.partial(gate_piece, ii, n) for n in range(tb // gate_w) for ii in range(per)]
        mxu = []
        for n in range(tb // mxu_w):
            for q in range(nsplit):
                mxu.append(functools.partial(score_piece, q, n))
                mxu.append(functools.partial(out_piece, q, n))
        for k in range(max(len(vec), len(mxu))):
            if k < len(mxu):
                mxu[k]()
            if k < len(vec):
                vec[k]()

    even = lax.rem(s, 2) == 0
    pl.when(even)(lambda: stages(a0_ref, a1_ref, hc1_ref, hc0_ref))
    pl.when(jnp.logical_not(even))(lambda: stages(a1_ref, a0_ref, hc0_ref, hc1_ref))

    @pl.when(s == ne + 1)
    def _():
        o_ref[0] = x_ref[0] + g_ref[0] * acc_ref[...].T


def _peer_dense(h2, u, vt, rank, cnt, e2, r, x, gate, tb, ec=PEER_EC):
    B, T, D = x.shape
    nT = T // tb
    ne = N_EXPERTS // ec
    tok = lambda b, i, s: (b, i, 0)
    rt_spec = pl.BlockSpec((P_HEADS, N_KEYS, tb), lambda b, i, s: (0, 0, b * nT + i))
    return pl.pallas_call(
        functools.partial(_peer_dense_kernel, ec=ec, ne=ne),
        grid=(B, nT, ne + 2),
        in_specs=[pl.BlockSpec((1, tb, D), tok),
                  pl.BlockSpec((ec, D), lambda b, i, s: (jnp.minimum(s, ne - 1), 0)),
                  pl.BlockSpec((1, D, ec), lambda b, i, s: (jnp.clip(s - 2, 0, ne - 1), 0, 0)),
                  rt_spec, rt_spec, rt_spec, rt_spec,
                  pl.BlockSpec((1, tb, D), tok), _mod_spec(gate, tb)],
        out_specs=pl.BlockSpec((1, tb, D), tok),
        out_shape=jax.ShapeDtypeStruct((B, T, D), F32),
        scratch_shapes=[pltpu.VMEM((D, tb), F32),
                        pltpu.VMEM((ec, tb), BF16), pltpu.VMEM((ec, tb), BF16),
                        pltpu.VMEM((ec, tb), BF16), pltpu.VMEM((ec, tb), BF16)],
        compiler_params=_cparams(("parallel", "parallel", "arbitrary")),
        name="peer_dense",
    )(h2, u, vt, rank, cnt, e2, r, x, gate)


def _pair_rms(y, w, lane):
    y2 = y * y
    lo = jnp.sum(jnp.where(lane < B_HEAD_DIM, y2, 0.0), axis=-1, keepdims=True)
    hi = jnp.sum(y2, axis=-1, keepdims=True) - lo
    ms = jnp.where(lane < B_HEAD_DIM, lo, hi) * (1.0 / B_HEAD_DIM)
    return y * lax.rsqrt(ms + EPS) * w


def _shared_kv_kernel(x_ref, sh_ref, sc_ref, nw_ref, wk_ref, wv_ref, kn_ref, k_ref, v_ref, kb_ref, vb_ref):
    h = _modulate(x_ref[0], nw_ref[...], sh_ref[0], sc_ref[0]).astype(BF16)
    kraw = jnp.dot(h, wk_ref[...], preferred_element_type=F32)
    v = jnp.dot(h, wv_ref[...], preferred_element_type=F32)
    v_ref[0] = v
    vb_ref[0] = v.astype(BF16)
    knw = kn_ref[...]
    lane = lax.broadcasted_iota(jnp.int32, (h.shape[0], LANES), 1)
    for hp in range(B_HEADS // 2):
        kh = _pair_rms(kraw[:, hp * LANES:(hp + 1) * LANES], knw, lane)
        k_ref[0, :, hp * LANES:(hp + 1) * LANES] = kh
        kb_ref[0, :, hp * LANES:(hp + 1) * LANES] = kh.astype(BF16)


def _shared_kv(x, shift, scale, nw, wk, wv, kn_pair, tb):
    B, T, D = x.shape
    tok = lambda b, i: (b, i, 0)
    blk = pl.BlockSpec((1, tb, D), tok)
    return pl.pallas_call(
        _shared_kv_kernel,
        grid=(B, T // tb),
        in_specs=[blk, _mod_spec(shift, tb), _mod_spec(scale, tb), _full_spec(nw),
                  _full_spec(wk), _full_spec(wv), _full_spec(kn_pair)],
        out_specs=[blk, blk, blk, blk],
        out_shape=[jax.ShapeDtypeStruct((B, T, D), F32), jax.ShapeDtypeStruct((B, T, D), F32),
                   jax.ShapeDtypeStruct((B, T, D), BF16), jax.ShapeDtypeStruct((B, T, D), BF16)],
        compiler_params=_cparams(("parallel", "parallel")),
        name="shared_kv",
    )(x, shift, scale, nw, wk, wv, kn_pair)


def _band_pre_kernel(x_ref, sh_ref, sc_ref, nw_ref, wq_ref, qn_ref, q_ref):
    h = _modulate(x_ref[0], nw_ref[...], sh_ref[0], sc_ref[0]).astype(BF16)
    q = jnp.dot(h, wq_ref[...], preferred_element_type=F32)
    qnw = qn_ref[...]
    lane = lax.broadcasted_iota(jnp.int32, (h.shape[0], LANES), 1)
    scale = B_HEAD_DIM ** -0.5
    for hp in range(B_HEADS // 2):
        qh = _pair_rms(q[:, hp * LANES:(hp + 1) * LANES], qnw, lane)
        q_ref[0, :, hp * LANES:(hp + 1) * LANES] = (qh * scale).astype(BF16)


def _band_pre(x, shift, scale, nw, wq, qn_pair, tb):
    B, T, D = x.shape
    tok = lambda b, i: (b, i, 0)
    blk = pl.BlockSpec((1, tb, D), tok)
    return pl.pallas_call(
        _band_pre_kernel,
        grid=(B, T // tb),
        in_specs=[blk, _mod_spec(shift, tb), _mod_spec(scale, tb), _full_spec(nw),
                  _full_spec(wq), _full_spec(qn_pair)],
        out_specs=blk,
        out_shape=jax.ShapeDtypeStruct((B, T, D), BF16),
        compiler_params=_cparams(("parallel", "parallel")),
        name="band_pre",
    )(x, shift, scale, nw, wq, qn_pair)


def _band_bias_kernel(tab_ref, o_ref):
    h = pl.program_id(0)
    nvar = 2 * LANES
    r = lax.broadcasted_iota(jnp.int32, (BAND_SUB, nvar), 0)
    w = lax.broadcasted_iota(jnp.int32, (BAND_SUB, nvar), 1) + (BAND_WIN - nvar)
    idx = jnp.clip(r + B_WINDOW - w, -REL_CLIP, REL_CLIP) + REL_CLIP
    far = tab_ref[h, 2 * REL_CLIP]

    def body(t, acc):
        return jnp.where(idx == t, tab_ref[h, t], acc)

    var = lax.fori_loop(0, 2 * REL_CLIP, body, jnp.full((BAND_SUB, nvar), far, F32))
    full = jnp.concatenate([jnp.full((BAND_SUB, BAND_WIN - nvar), far, F32), var], axis=1)
    rr = lax.broadcasted_iota(jnp.int32, (BAND_SUB, BAND_WIN), 0)
    ww = lax.broadcasted_iota(jnp.int32, (BAND_SUB, BAND_WIN), 1)
    qc = rr // CHUNK + LEFT_CHUNKS
    kc = ww // CHUNK
    allowed = (kc <= qc) & (kc >= qc - LEFT_CHUNKS)
    o_ref[0] = jnp.where(allowed, full, NEG)


def _band_bias(table):
    nh = table.shape[0]
    return pl.pallas_call(
        _band_bias_kernel,
        grid=(nh,),
        in_specs=[pl.BlockSpec(memory_space=pltpu.SMEM)],
        out_specs=pl.BlockSpec((1, BAND_SUB, BAND_WIN), lambda h: (h, 0, 0)),
        out_shape=jax.ShapeDtypeStruct((nh, BAND_SUB, BAND_WIN), F32),
        compiler_params=_cparams(("arbitrary",)),
        name="band_bias",
    )(table)


def _band_windows(windows, bias_ref, lane):
    chains = [(w, hh) for w in range(len(windows)) for hh in range(2)]
    scores = []
    for w, hh in chains:
        qs, kw, _ = windows[w]
        nkw = kw.shape[0]
        sel = (lane < B_HEAD_DIM) if hh == 0 else (lane >= B_HEAD_DIM)
        qh = jnp.where(sel, qs, jnp.zeros_like(qs))
        s = lax.dot_general(qh, kw, _NT, preferred_element_type=F32)
        scores.append(s + bias_ref[hh, :, BAND_WIN - nkw:BAND_WIN])
    maxes = [jnp.max(s, axis=-1, keepdims=True) for s in scores]
    probs = [jnp.exp((s - m).astype(BF16)) for s, m in zip(scores, maxes)]
    vext = []
    for _, _, vw in windows:
        ones_lane = (lax.broadcasted_iota(jnp.int32, vw.shape, 1) == 0).astype(BF16)
        vext.append(jnp.concatenate([vw, ones_lane], axis=1))
    outs = []
    for (w, hh), p in zip(chains, probs):
        pv = jnp.dot(p, vext[w], preferred_element_type=F32)
        outs.append(pv[:, :LANES] / pv[:, LANES:LANES + 1])
    return [jnp.where(lane < B_HEAD_DIM, outs[2 * w], outs[2 * w + 1]) for w in range(len(windows))]


def _band_attn_kernel(q_ref, kp_ref, kc_ref, vp_ref, vc_ref, bias_ref, o_ref, *, tq):
    i = pl.program_id(2)
    lane = lax.broadcasted_iota(jnp.int32, (BAND_SUB, LANES), 1)
    nsub = tq // BAND_SUB

    def run(first):
        windows = []
        for c in range(nsub):
            qs = q_ref[0, c * BAND_SUB:(c + 1) * BAND_SUB, :]
            hi = (c + 1) * BAND_SUB
            if first or hi >= BAND_WIN:
                lo = max(hi - BAND_WIN, 0)
                kw = kc_ref[0, lo:hi, :]
                vw = vc_ref[0, lo:hi, :]
            else:
                lo = tq - (BAND_WIN - hi)
                kw = jnp.concatenate([kp_ref[0, lo:tq, :], kc_ref[0, 0:hi, :]], axis=0)
                vw = jnp.concatenate([vp_ref[0, lo:tq, :], vc_ref[0, 0:hi, :]], axis=0)
            windows.append((qs, kw, vw))
        for c, o in enumerate(_band_windows(windows, bias_ref, lane)):
            o_ref[0, c * BAND_SUB:(c + 1) * BAND_SUB, :] = o.astype(BF16)

    pl.when(i == 0)(lambda: run(True))
    pl.when(i > 0)(lambda: run(False))


def _band_attn(q, k, v, bias, tq):
    B, T, D = q.shape
    assert tq >= B_WINDOW and T % tq == 0
    cur = lambda b, hp, i: (b, i, hp)
    prev = lambda b, hp, i: (b, jnp.maximum(i - 1, 0), hp)
    blk = lambda m: pl.BlockSpec((1, tq, LANES), m)
    return pl.pallas_call(
        functools.partial(_band_attn_kernel, tq=tq),
        grid=(B, B_HEADS // 2, T // tq),
        in_specs=[blk(cur), blk(prev), blk(cur), blk(prev), blk(cur),
                  pl.BlockSpec((2, BAND_SUB, BAND_WIN), lambda b, hp, i: (hp, 0, 0))],
        out_specs=blk(cur),
        out_shape=jax.ShapeDtypeStruct((B, T, D), BF16),
        compiler_params=_cparams(("parallel", "parallel", "arbitrary")),
        name="band_attn",
    )(q, k, k, v, v, bias)


def _band_step_kernel(q_ref, k_ref, v_ref, bias_ref, o_ref):
    lane = lax.broadcasted_iota(jnp.int32, (BAND_SUB, LANES), 1)
    o, = _band_windows([(q_ref[0], k_ref[0], v_ref[0])], bias_ref, lane)
    o_ref[0] = o.astype(BF16)


def _band_step(q, kwin, vwin, bias):
    B, _, D = q.shape
    return pl.pallas_call(
        _band_step_kernel,
        grid=(B, B_HEADS // 2),
        in_specs=[pl.BlockSpec((1, BAND_SUB, LANES), lambda b, hp: (b, 0, hp)),
                  pl.BlockSpec((1, BAND_WIN, LANES), lambda b, hp: (b, 0, hp)),
                  pl.BlockSpec((1, BAND_WIN, LANES), lambda b, hp: (b, 0, hp)),
                  pl.BlockSpec((2, BAND_SUB, BAND_WIN), lambda b, hp: (hp, 0, 0))],
        out_specs=pl.BlockSpec((1, BAND_SUB, LANES), lambda b, hp: (b, 0, hp)),
        out_shape=jax.ShapeDtypeStruct((B, BAND_SUB, D), BF16),
        compiler_params=_cparams(("parallel", "parallel")),
        name="band_step",
    )(q, kwin, vwin, bias)


def _rope_tables(pos):
    half = ROPE_DIM // 2
    freqs = ROPE_THETA ** (-jnp.arange(half, dtype=F32) / half)
    ang = pos.astype(F32)[:, None] * freqs[None, :]
    c, s = jnp.cos(ang), jnp.sin(ang)
    n = pos.shape[0]
    ones = jnp.ones((n, NOPE_DIM), F32)
    zeros = jnp.zeros((n, NOPE_DIM), F32)
    pad1 = jnp.ones((n, HEAD_PAD - QK_DIM), F32)
    pad0 = jnp.zeros((n, HEAD_PAD - QK_DIM), F32)
    return (jnp.concatenate([ones, c, c, pad1], axis=1),
            jnp.concatenate([zeros, -s, s, pad0], axis=1))


def _pad_heads(w, nheads, width):
    lead = w.shape[:-1]
    w = w.reshape(lead + (nheads, width))
    w = jnp.pad(w, [(0, 0)] * len(lead) + [(0, 0), (0, HEAD_PAD - width)])
    return w.reshape(lead + (nheads * HEAD_PAD,))


def _block(n, pref):
    for t in pref:
        if n % t == 0:
            return t
    return n


def kernel(x_prompt, x_sample, c_prompt, c_sample, cache_a_ckv, cache_a_krope, cache_b_k, cache_b_v, ada_w, ada_b, norm_mix_w, norm_ffn_w, a_w_in, a_q_lora_norm, a_kv_lora_norm, a_w_uq, a_w_ukv, a_q_norm, a_k_norm, a_w_o, kv_ada_w, kv_ada_b, kv_norm_w, b_w_kv, b_k_norm, b_w_q, b_q_norm, b_rel_bias, b_w_o, p_w_q, p_subkeys, p_u, p_v):
    D = D_MODEL
    Bp, Tp, _ = x_prompt.shape
    Bs, Ts, _ = x_sample.shape
    P = cache_a_ckv.shape[2]
    Pb = cache_b_k.shape[1]
    assert Ts == CHUNK and Pb == B_WINDOW and P % CHUNK == 0

    ada_all_w = jnp.concatenate([ada_w[0], ada_w[1], kv_ada_w], axis=1).astype(BF16)
    ada_all_b = jnp.concatenate([ada_b[0], ada_b[1], kv_ada_b])[None, :]
    c_all = jnp.concatenate([c_prompt, c_sample], axis=0)
    mod = _ada(c_all, ada_all_w, ada_all_b)

    w_in = a_w_in[0].astype(BF16)
    qln = a_q_lora_norm[0][None, :]
    kvln = a_kv_lora_norm[0][None, :]
    wuq_pad = _pad_heads(a_w_uq[0], A_HEADS, QK_DIM).astype(BF16)
    wukv = a_w_ukv[0].reshape(KV_LORA, A_HEADS, NOPE_DIM + V_DIM)
    wuk_pad = jnp.pad(wukv[:, :, :NOPE_DIM], ((0, 0), (0, 0), (0, HEAD_PAD - NOPE_DIM))
                      ).reshape(KV_LORA, A_HEADS * HEAD_PAD).astype(BF16)
    wuv_pad = jnp.pad(wukv[:, :, NOPE_DIM:], ((0, 0), (0, 0), (0, HEAD_PAD - V_DIM))
                      ).reshape(KV_LORA, A_HEADS * HEAD_PAD).astype(BF16)
    vone = jnp.asarray(np.tile(np.eye(1, HEAD_PAD, k=V_DIM), (1, A_HEADS)), F32)
    qn_pad = jnp.pad(a_q_norm[0], (0, HEAD_PAD - QK_DIM))[None, :]
    kn_pad = jnp.pad(a_k_norm[0], (0, HEAD_PAD - QK_DIM))[None, :]
    place = jnp.asarray(np.eye(ROPE_DIM, HEAD_PAD, k=NOPE_DIM), BF16)
    a_wo = a_w_o[0].astype(BF16)
    wk_b = b_w_kv[:, :D].astype(BF16)
    wv_b = b_w_kv[:, D:].astype(BF16)
    bkn_pair = jnp.tile(b_k_norm, 2)[None, :]
    bqn_pair = jnp.tile(b_q_norm[0], 2)[None, :]
    bwq = b_w_q[0].astype(BF16)
    bwo = b_w_o[0].astype(BF16)
    bias_tile = _band_bias(b_rel_bias[0])

    def peer_weights(layer):
        wq = p_w_q[layer].astype(BF16)
        sk = p_subkeys[layer]
        sk1 = jnp.pad(sk[:, 0], ((0, 0), (0, 0), (0, P_HALF))).astype(BF16)
        sk2 = jnp.pad(sk[:, 1], ((0, 0), (0, 0), (P_HALF, 0))).astype(BF16)
        vt = p_v[layer].astype(BF16).reshape(N_EXPERTS // PEER_EC, PEER_EC, D).transpose(0, 2, 1)
        return wq, sk1, sk2, p_u[layer].astype(BF16), vt

    peer_w = [peer_weights(0), peer_weights(1)]
    norm_mix = norm_mix_w[:, None, :]
    norm_ffn = norm_ffn_w[:, None, :]
    kv_nw = kv_norm_w[None, :]

    def run(x, modp, per_token, pos_q, past):
        B, T, _ = x.shape
        if per_token:
            modv = jnp.repeat(modp, T, axis=0)[None]
            xw = x.reshape(1, B * T, D)
        else:
            modv = modp[:, None, :]
            xw = x
        Bw, Tw, _ = xw.shape
        sl = lambda k: modv[:, :, k * D:(k + 1) * D]
        tb = _block(Tw, (512, 256, 128))
        tbr = _block(Tw, (256, 128))
        tbe = _block(Tw, (512, 256, 128))

        cos_q, sin_q = _rope_tables(pos_q)
        ckv, kr, q = _mla_pre(xw, sl(0), sl(1), norm_mix[0], w_in, qln, kvln, wuq_pad, qn_pad,
                              cos_q, sin_q, tb)
        ckv = ckv.reshape(B, T, KV_LORA)
        kr = kr.reshape(B, T, ROPE_DIM)
        q = q.reshape(B, T, A_HEADS * HEAD_PAD)
        if past is None:
            ckv_all, kr_all = ckv, kr
            pos_k = pos_q[:T]
        else:
            ckv_all = jnp.concatenate([past[0], ckv], axis=1)
            kr_all = jnp.concatenate([past[1], kr], axis=1)
            pos_k = jnp.arange(P + T, dtype=jnp.int32)
        S = ckv_all.shape[1]
        cos_k, sin_k = _rope_tables(pos_k)
        tbk = _block(S, (512, 704, 256, 192, 64))
        k, v = _mla_kv(ckv_all, kr_all, wuk_pad, wuv_pad, vone, kn_pad, place, cos_k, sin_k, tbk)
        if past is None:
            ta = _block(T, (512, 256, 128, 64))
            o = _mla_attn(q, k, v, ta, ta, True)
        else:
            o = _mla_attn(q, k, v, T, tbk, False)
        o = o.reshape(Bw, Tw, A_HEADS * V_DIM)

        wq, sk1, sk2, u, vt = peer_w[0]
        x1, h2, rank, cnt, e2, r = _post_mix(o, xw, sl(2), sl(3), sl(4), norm_ffn[0], a_wo, wq, sk1, sk2, tbr)
        x2 = _peer_dense(h2, u, vt, rank, cnt, e2, r, x1, sl(5), tbe)

        kf, vf, kb, vb = _shared_kv(x2, sl(12), sl(13), kv_nw, wk_b, wv_b, bkn_pair, tb)

        qb = _band_pre(x2, sl(6), sl(7), norm_mix[1], bwq, bqn_pair, tb)
        if past is None:
            ob = _band_attn(qb, kb, vb, bias_tile, B_WINDOW)
            new_bk = kf[:, -B_WINDOW:].reshape(B, B_WINDOW, B_HEADS, B_HEAD_DIM)
            new_bv = vf[:, -B_WINDOW:].reshape(B, B_WINDOW, B_HEADS, B_HEAD_DIM)
        else:
            zq = jnp.zeros((B, CHUNK, D), BF16)
            qpad = jnp.concatenate([zq, qb.reshape(B, T, D)], axis=1)
            kwin = jnp.concatenate([zq, past[2].reshape(B, Pb, D).astype(BF16), kb.reshape(B, T, D)], axis=1)
            vwin = jnp.concatenate([zq, past[3].reshape(B, Pb, D).astype(BF16), vb.reshape(B, T, D)], axis=1)
            ob = _band_step(qpad, kwin, vwin, bias_tile)[:, CHUNK:].reshape(Bw, Tw, D)
            new_bk = jnp.concatenate([past[2], kf.reshape(B, T, B_HEADS, B_HEAD_DIM)], axis=1)[:, -Pb:]
            new_bv = jnp.concatenate([past[3], vf.reshape(B, T, B_HEADS, B_HEAD_DIM)], axis=1)[:, -Pb:]

        wq, sk1, sk2, u, vt = peer_w[1]
        x3, h4, rank, cnt, e2, r = _post_mix(ob, x2, sl(8), sl(9), sl(10), norm_ffn[1], bwo, wq, sk1, sk2, tbr)
        y = _peer_dense(h4, u, vt, rank, cnt, e2, r, x3, sl(11), tbe)
        return y.reshape(B, T, D), ckv[None], kr[None], new_bk, new_bv

    pos_p = jnp.arange(Tp, dtype=jnp.int32)
    pos_s = jnp.tile(P + jnp.arange(Ts, dtype=jnp.int32), Bs)
    y_p, p_ckv, p_kr, p_bk, p_bv = run(x_prompt, mod[:Bp], False, pos_p, None)
    y_s, s_ckv, s_kr, s_bk, s_bv = run(x_sample, mod[Bp:], True, pos_s,
                                       (cache_a_ckv[0], cache_a_krope[0], cache_b_k, cache_b_v))
    return (y_p, y_s, p_ckv, p_kr, p_bk, p_bv, s_ckv, s_kr, s_bk, s_bv)
```

```python
import functools
import math

import numpy as np
import jax
import jax.numpy as jnp
from jax import lax
from jax.experimental import pallas as pl
from jax.experimental.pallas import tpu as pltpu

F32 = jnp.float32
BF16 = jnp.bfloat16

D_MODEL = 1024
CHUNK = 64
A_HEADS = 16
Q_LORA = 384
KV_LORA = 256
NOPE_DIM = 64
ROPE_DIM = 32
V_DIM = 64
QK_DIM = NOPE_DIM + ROPE_DIM
ROPE_THETA = 10000.0
B_HEADS = 16
B_HEAD_DIM = 64
LEFT_CHUNKS = 8
B_WINDOW = LEFT_CHUNKS * CHUNK
REL_CLIP = 128
P_HEADS = 8
N_KEYS = 128
N_EXPERTS = N_KEYS * N_KEYS
P_HALF = 64
P_TOPK = 16
NEG = -1e30
EPS = 1e-6

LANES = 128
BF16_ROWS = 16
HEAD_PAD = 128
BAND_SUB = 2 * CHUNK
BAND_WIN = B_WINDOW + BAND_SUB
VMEM_LIMIT = 56 * 1024 * 1024
PEER_EC = 512
PEER_SPLIT = 2
MXU_N = 256
GATE_LANES = 256

_NT = (((1,), (1,)), ((), ()))


def _cparams(sem):
    return pltpu.CompilerParams(dimension_semantics=sem, vmem_limit_bytes=VMEM_LIMIT)


def _modulate(x, w, shift, scale):
    ms = jnp.mean(x * x, axis=-1, keepdims=True)
    y = x * lax.rsqrt(ms + EPS) * w
    return y * (1.0 + scale) + shift


def _rms(x, w):
    ms = jnp.mean(x * x, axis=-1, keepdims=True)
    return x * lax.rsqrt(ms + EPS) * w


def _mod_spec(arr, tb):
    if arr.shape[1] == 1:
        return pl.BlockSpec((1, 1, arr.shape[2]), lambda b, i, *_: (b, 0, 0))
    return pl.BlockSpec((1, tb, arr.shape[2]), lambda b, i, *_: (b, i, 0))


def _full_spec(arr):
    nd = arr.ndim
    return pl.BlockSpec(arr.shape, lambda *_: (0,) * nd)


def _ada_kernel(c_ref, w_ref, b_ref, o_ref):
    c = c_ref[...]
    a = c / (1.0 + jnp.exp(-c))
    o_ref[...] = jnp.dot(a.astype(BF16), w_ref[...], preferred_element_type=F32) + b_ref[...]


def _ada(c, w, b, tn=2048):
    m, k = c.shape
    n = w.shape[1]
    return pl.pallas_call(
        _ada_kernel,
        grid=(n // tn,),
        in_specs=[pl.BlockSpec((m, k), lambda j: (0, 0)),
                  pl.BlockSpec((k, tn), lambda j: (0, j)),
                  pl.BlockSpec((1, tn), lambda j: (0, j))],
        out_specs=pl.BlockSpec((m, tn), lambda j: (0, j)),
        out_shape=jax.ShapeDtypeStruct((m, n), F32),
        compiler_params=_cparams(("arbitrary",)),
        name="ada_mod",
    )(c, w, b)


def _rope_head(xh, cos, sin, lane):
    rolled = jnp.where(lane < NOPE_DIM + ROPE_DIM // 2,
                       pltpu.roll(xh, LANES - ROPE_DIM // 2, 1),
                       pltpu.roll(xh, ROPE_DIM // 2, 1))
    return xh * cos + rolled * sin


def _mla_pre_kernel(x_ref, sh_ref, sc_ref, nw_ref, win_ref, qln_ref, kvln_ref, wuq_ref, qn_ref,
                    cos_ref, sin_ref, ckv_ref, kr_ref, q_ref):
    x = x_ref[0]
    h = _modulate(x, nw_ref[...], sh_ref[0], sc_ref[0])
    proj = jnp.dot(h.astype(BF16), win_ref[...], preferred_element_type=F32)
    c_q = _rms(proj[:, :Q_LORA], qln_ref[...])
    ckv_ref[0] = _rms(proj[:, Q_LORA:Q_LORA + KV_LORA], kvln_ref[...])
    kr_ref[0] = proj[:, Q_LORA + KV_LORA:]
    q = jnp.dot(c_q.astype(BF16), wuq_ref[...], preferred_element_type=F32)
    cos = cos_ref[...]
    sin = sin_ref[...]
    qn = qn_ref[...]
    lane = lax.broadcasted_iota(jnp.int32, cos.shape, 1)
    scale = QK_DIM ** -0.5
    for hd in range(A_HEADS):
        qh = q[:, hd * HEAD_PAD:(hd + 1) * HEAD_PAD]
        ss = jnp.sum(qh * qh, axis=-1, keepdims=True) * (1.0 / QK_DIM)
        qh = qh * lax.rsqrt(ss + EPS) * qn
        qh = _rope_head(qh, cos, sin, lane)
        q_ref[0, :, hd * HEAD_PAD:(hd + 1) * HEAD_PAD] = (qh * scale).astype(BF16)


def _mla_pre(x, shift, scale, nw, w_in, qln, kvln, wuq_pad, qn_pad, cos_t, sin_t, tb):
    B, T, D = x.shape
    nT = T // tb
    tab_spec = pl.BlockSpec((tb, LANES), lambda b, i: (i, 0))
    return pl.pallas_call(
        _mla_pre_kernel,
        grid=(B, nT),
        in_specs=[pl.BlockSpec((1, tb, D), lambda b, i: (b, i, 0)),
                  _mod_spec(shift, tb), _mod_spec(scale, tb),
                  _full_spec(nw), _full_spec(w_in), _full_spec(qln), _full_spec(kvln),
                  _full_spec(wuq_pad), _full_spec(qn_pad), tab_spec, tab_spec],
        out_specs=[pl.BlockSpec((1, tb, KV_LORA), lambda b, i: (b, i, 0)),
                   pl.BlockSpec((1, tb, ROPE_DIM), lambda b, i: (b, i, 0)),
                   pl.BlockSpec((1, tb, A_HEADS * HEAD_PAD), lambda b, i: (b, i, 0))],
        out_shape=[jax.ShapeDtypeStruct((B, T, KV_LORA), F32),
                   jax.ShapeDtypeStruct((B, T, ROPE_DIM), F32),
                   jax.ShapeDtypeStruct((B, T, A_HEADS * HEAD_PAD), BF16)],
        compiler_params=_cparams(("parallel", "parallel")),
        name="mla_pre",
    )(x, shift, scale, nw, w_in, qln, kvln, wuq_pad, qn_pad, cos_t, sin_t)


def _mla_kv_kernel(ckv_ref, kr_ref, wuk_ref, wuv_ref, vone_ref, kn_ref, place_ref, cos_ref, sin_ref,
                   k_ref, v_ref):
    ckv = ckv_ref[0].astype(BF16)
    kn = jnp.dot(ckv, wuk_ref[...], preferred_element_type=F32)
    v_ref[0] = (jnp.dot(ckv, wuv_ref[...], preferred_element_type=F32) + vone_ref[...]).astype(BF16)
    kr = kr_ref[0]
    kr_hi = kr.astype(BF16)
    kr_lo = (kr - kr_hi.astype(F32)).astype(BF16)
    place = place_ref[...]
    krp = (jnp.dot(kr_hi, place, preferred_element_type=F32)
           + jnp.dot(kr_lo, place, preferred_element_type=F32))
    cos = cos_ref[...]
    sin = sin_ref[...]
    knw = kn_ref[...]
    lane = lax.broadcasted_iota(jnp.int32, cos.shape, 1)
    for hd in range(A_HEADS):
        kh = kn[:, hd * HEAD_PAD:(hd + 1) * HEAD_PAD] + krp
        ss = jnp.sum(kh * kh, axis=-1, keepdims=True) * (1.0 / QK_DIM)
        kh = kh * lax.rsqrt(ss + EPS) * knw
        kh = _rope_head(kh, cos, sin, lane)
        k_ref[0, :, hd * HEAD_PAD:(hd + 1) * HEAD_PAD] = kh.astype(BF16)


def _mla_kv(ckv, kr, wuk_pad, wuv_pad, vone, kn_pad, place, cos_t, sin_t, tb):
    B, S, _ = ckv.shape
    tab_spec = pl.BlockSpec((tb, LANES), lambda b, i: (i, 0))
    wide = pl.BlockSpec((1, tb, A_HEADS * HEAD_PAD), lambda b, i: (b, i, 0))
    return pl.pallas_call(
        _mla_kv_kernel,
        grid=(B, S // tb),
        in_specs=[pl.BlockSpec((1, tb, KV_LORA), lambda b, i: (b, i, 0)),
                  pl.BlockSpec((1, tb, ROPE_DIM), lambda b, i: (b, i, 0)),
                  _full_spec(wuk_pad), _full_spec(wuv_pad), _full_spec(vone), _full_spec(kn_pad),
                  _full_spec(place), tab_spec, tab_spec],
        out_specs=[wide, wide],
        out_shape=[jax.ShapeDtypeStruct((B, S, A_HEADS * HEAD_PAD), BF16),
                   jax.ShapeDtypeStruct((B, S, A_HEADS * HEAD_PAD), BF16)],
        compiler_params=_cparams(("parallel", "parallel")),
        name="mla_kv",
    )(ckv, kr, wuk_pad, wuv_pad, vone, kn_pad, place, cos_t, sin_t)


def _mla_attn_kernel(q_ref, k_ref, v_ref, o_ref, m_sc, acc_sc, *, tq, tk, causal, nk):
    i = pl.program_id(2)
    m_sc[...] = jnp.full(m_sc.shape, -jnp.inf, F32)
    acc_sc[...] = jnp.zeros(acc_sc.shape, F32)

    def block(start, masked):
        k = k_ref[0, pl.ds(start, tk), :]
        v = v_ref[0, pl.ds(start, tk), :]
        if masked:
            qc = lax.broadcasted_iota(jnp.int32, (tq, tk), 0) // CHUNK
            kc = lax.broadcasted_iota(jnp.int32, (tq, tk), 1) // CHUNK
            allowed = kc <= qc
        for hh in range(2):
            qh = q_ref[0, :, hh * HEAD_PAD:(hh + 1) * HEAD_PAD]
            s = lax.dot_general(qh, k[:, hh * HEAD_PAD:(hh + 1) * HEAD_PAD], _NT,
                                preferred_element_type=F32)
            if masked:
                s = jnp.where(allowed, s, NEG)
            m_prev = m_sc[hh]
            m_new = jnp.maximum(m_prev, jnp.max(s, axis=-1, keepdims=True))
            alpha = jnp.exp(m_prev - m_new)
            if tk % LANES == 0:
                m_wide = jnp.concatenate([m_new] * (tk // LANES), axis=1)
            else:
                m_wide = m_new[:, :1]
            p = jnp.exp((s - m_wide).astype(BF16))
            acc_sc[hh] = alpha * acc_sc[hh] + jnp.dot(p, v[:, hh * HEAD_PAD:(hh + 1) * HEAD_PAD],
                                                      preferred_element_type=F32)
            m_sc[hh] = m_new

    def body(j, carry):
        block(pl.multiple_of(j * tk, tk), False)
        return carry

    if causal:
        lax.fori_loop(0, i, body, 0)
        block(pl.multiple_of(i * tk, tk), True)
    else:
        lax.fori_loop(0, nk, body, 0)

    lane = lax.broadcasted_iota(jnp.int32, (tq, LANES), 1)
    outs = []
    for hh in range(2):
        acc = acc_sc[hh]
        outs.append(acc / acc[:, V_DIM:V_DIM + 1])
    o = jnp.where(lane < V_DIM, outs[0], pltpu.roll(outs[1], V_DIM, 1))
    o_ref[0] = o.astype(BF16)


def _mla_attn(q, k, v, tq, tk, causal):
    B, T, _ = q.shape
    S = k.shape[1]
    nq, nk = T // tq, S // tk
    assert T % tq == 0 and S % tk == 0 and (not causal or (tq == tk and T == S))
    kv_spec = pl.BlockSpec((1, S, 2 * HEAD_PAD), lambda b, hp, i: (b, 0, hp))
    return pl.pallas_call(
        functools.partial(_mla_attn_kernel, tq=tq, tk=tk, causal=causal, nk=nk),
        grid=(B, A_HEADS // 2, nq),
        in_specs=[pl.BlockSpec((1, tq, 2 * HEAD_PAD), lambda b, hp, i: (b, i, hp)), kv_spec, kv_spec],
        out_specs=pl.BlockSpec((1, tq, 2 * V_DIM), lambda b, hp, i: (b, i, hp)),
        out_shape=jax.ShapeDtypeStruct((B, T, A_HEADS * V_DIM), BF16),
        scratch_shapes=[pltpu.VMEM((2, tq, LANES), F32), pltpu.VMEM((2, tq, LANES), F32)],
        compiler_params=_cparams(("parallel", "parallel", "arbitrary")),
        name="mla_attn",
    )(q, k, v)


def _top16_rows(chains):
    works = [s for s, _, _ in chains]
    for k in range(P_TOPK):
        for c, (_, t_sc, cols) in enumerate(chains):
            m = jnp.max(works[c], axis=0, keepdims=True)
            t_sc[k:k + 1, cols] = m
            if k + 1 < P_TOPK:
                works[c] = jnp.where(works[c] == m, -jnp.inf, works[c])


def _route_tiles(tiles, t1_sc, t2_sc):
    _top16_rows([(s1, t1_sc, cols) for s1, _, cols in tiles] + [(s2, t2_sc, cols) for _, s2, cols in tiles])
    state = []
    for s1, s2, cols in tiles:
        row8 = lax.broadcasted_iota(jnp.int32, (8, s1.shape[1]), 0)
        t1 = [t1_sc[k:k + 1, cols] for k in range(P_TOPK)]
        t2 = [t2_sc[k:k + 1, cols] for k in range(P_TOPK)]
        t2_lo = t2_sc[0:8, cols]
        t2_hi = t2_sc[8:16, cols]
        t1_hi = t1_sc[8:16, cols]
        cands = [t1[0] + t2_lo, t1[0] + t2_hi, t1_hi + t2[0]]
        for k1 in range(1, 8):
            lim = P_TOPK // (k1 + 1)
            c = t1[k1] + t2_lo
            cands.append(c if lim >= 8 else jnp.where(row8 < lim, c, -jnp.inf))
        top = t1[0] + t2[0]
        state.append(dict(t1=t1, t2=t2, cands=cands, top=top, z=jnp.zeros_like(top), tau=top))
    for k in range(P_TOPK):
        for st in state:
            m = st["cands"][0]
            for c in st["cands"][1:]:
                m = jnp.maximum(m, c)
            m = jnp.max(m, axis=0, keepdims=True)
            st["z"] = st["z"] + jnp.exp(m - st["top"])
            st["tau"] = m
            if k + 1 < P_TOPK:
                st["cands"] = [jnp.where(c == m, -jnp.inf, c) for c in st["cands"]]
    outs = []
    for (s1, s2, cols), st in zip(tiles, state):
        cnt = jnp.zeros_like(s1)
        rank = jnp.zeros_like(s2)
        for k in range(P_TOPK):
            cnt = jnp.where(s1 + st["t2"][k] >= st["tau"], float(k + 1), cnt)
            rank = jnp.where(st["t2"][k] > s2, float(k + 1), rank)
        n1 = jnp.sum(jnp.where(s1 >= st["t1"][P_TOPK - 1], 1.0, 0.0), axis=0, keepdims=True)
        n2 = jnp.sum(jnp.where(s2 >= st["t2"][P_TOPK - 1], 1.0, 0.0), axis=0, keepdims=True)
        npair = jnp.sum(cnt, axis=0, keepdims=True)
        k = float(P_TOPK)
        tied = jnp.where((n1 != k) | (n2 != k) | (npair != k), 1.0, 0.0)
        outs.append((rank, cnt, jnp.exp(s2 - st["t2"][0]), jnp.exp(s1 - st["t1"][0]) / st["z"], tied))
    return outs


def _route_tile_exact(s1, s2, t1_sc, t2_sc, cols):
    nk, w = s1.shape
    key = lax.broadcasted_iota(jnp.int32, (nk, w), 0).astype(F32)

    def top16(s, t_sc):
        work, rank, tops = s, jnp.full(s.shape, float(P_TOPK), F32), []
        for k in range(P_TOPK):
            m = jnp.max(work, axis=0, keepdims=True)
            first = jnp.min(jnp.where(work == m, key, float(nk)), axis=0, keepdims=True)
            hit = key == first
            rank = jnp.where(hit, float(k), rank)
            work = jnp.where(hit, -jnp.inf, work)
            t_sc[k:k + 1, cols] = m
            tops.append(m)
        return tops, rank

    t1, rank1 = top16(s1, t1_sc)
    t2, rank2 = top16(s2, t2_sc)
    row8 = lax.broadcasted_iota(jnp.int32, (8, w), 0).astype(F32)
    t2_lo = t2_sc[0:8, cols]
    t2_hi = t2_sc[8:16, cols]
    t1_hi = t1_sc[8:16, cols]
    cands = [(t1[0] + t2_lo, row8), (t1[0] + t2_hi, row8 + 8.0), (t1_hi + t2[0], (row8 + 8.0) * P_TOPK)]
    for k1 in range(1, 8):
        lim = P_TOPK // (k1 + 1)
        c = t1[k1] + t2_lo
        cands.append((c if lim >= 8 else jnp.where(row8 < lim, c, -jnp.inf), row8 + float(k1 * P_TOPK)))
    top = t1[0] + t2[0]
    z = jnp.zeros_like(top)
    k1_row = lax.broadcasted_iota(jnp.int32, (P_TOPK, w), 0).astype(F32)
    cnt_k1 = jnp.zeros((P_TOPK, w), F32)
    for k in range(P_TOPK):
        m = cands[0][0]
        for c, _ in cands[1:]:
            m = jnp.maximum(m, c)
        m = jnp.max(m, axis=0, keepdims=True)
        first = None
        for c, pos in cands:
            f = jnp.min(jnp.where(c == m, pos, float(P_TOPK * P_TOPK)), axis=0, keepdims=True)
            first = f if first is None else jnp.minimum(first, f)
        z = z + jnp.exp(m - top)
        cnt_k1 = cnt_k1 + jnp.where(k1_row == jnp.floor(first * (1.0 / P_TOPK)), 1.0, 0.0)
        cands = [(jnp.where(pos == first, -jnp.inf, c), pos) for c, pos in cands]
    t1_sc[:, cols] = cnt_k1
    cnt = jnp.zeros_like(s1)
    for k1 in range(P_TOPK):
        cnt = jnp.where(rank1 == float(k1), t1_sc[k1:k1 + 1, cols], cnt)
    return rank2, cnt, jnp.exp(s1 - t1[0]) / z


def _post_mix_kernel(o_ref, x_ref, g_ref, sh_ref, sc_ref, nw_ref, wo_ref, wq_ref, sk1_ref, sk2_ref,
                     x1_ref, h2_ref, rank_ref, cnt_ref, e2_ref, r_ref, t1_sc, t2_sc, s_sc, tied_sc):
    mix = jnp.dot(o_ref[0], wo_ref[...], preferred_element_type=F32)
    x1 = x_ref[0] + g_ref[0] * mix
    x1_ref[0] = x1
    h2 = _modulate(x1, nw_ref[...], sh_ref[0], sc_ref[0]).astype(BF16)
    h2_ref[0] = h2
    tb = h2.shape[0]
    q = jnp.dot(h2, wq_ref[...], preferred_element_type=F32).astype(BF16)
    for h in range(P_HEADS):
        qh = q[:, h * 2 * P_HALF:(h + 1) * 2 * P_HALF]
        s_sc[h, 0] = lax.dot_general(sk1_ref[h], qh, _NT, preferred_element_type=F32)
        s_sc[h, 1] = lax.dot_general(sk2_ref[h], qh, _NT, preferred_element_type=F32)

    def head_body(h, carry):
        s1 = s_sc[h, 0]
        s2 = s_sc[h, 1]
        lane_tiles = [slice(nt * LANES, (nt + 1) * LANES) for nt in range(tb // LANES)]
        routed = _route_tiles([(s1[:, cols], s2[:, cols], cols) for cols in lane_tiles], t1_sc, t2_sc)
        for cols, (rank, cnt, e2, r, tied) in zip(lane_tiles, routed):
            rank_ref[h, :, cols] = rank.astype(BF16)
            cnt_ref[h, :, cols] = cnt
            e2_ref[h, :, cols] = e2.astype(BF16)
            r_ref[h, :, cols] = r
            tied_sc[h, :, cols] = jnp.broadcast_to(tied, (8, tied.shape[1]))
        return carry

    lax.fori_loop(0, P_HEADS, head_body, 0)

    @pl.when(jnp.max(tied_sc[...]) > 0.0)
    def _():
        def fix_head(h, carry):
            for nt in range(tb // LANES):
                cols = slice(nt * LANES, (nt + 1) * LANES)

                @pl.when(jnp.max(tied_sc[h, :, cols]) > 0.0)
                def _(cols=cols):
                    rank_x, cnt_x, r_x = _route_tile_exact(s_sc[h, 0, :, cols], s_sc[h, 1, :, cols],
                                                           t1_sc, t2_sc, cols)
                    rank_ref[h, :, cols] = rank_x.astype(BF16)
                    cnt_ref[h, :, cols] = cnt_x
                    r_ref[h, :, cols] = r_x
            return carry

        lax.fori_loop(0, P_HEADS, fix_head, 0)


def _post_mix(o, x, gate, shift, scale, nw, w_o, wq_heads, sk1, sk2, tb):
    B, T, D = x.shape
    nT = T // tb
    n = B * T
    tok = lambda b, i: (b, i, 0)
    rt_spec = pl.BlockSpec((P_HEADS, N_KEYS, tb), lambda b, i: (0, 0, b * nT + i))
    rt_shape = jax.ShapeDtypeStruct((P_HEADS, N_KEYS, n), F32)
    rt_shape_b = jax.ShapeDtypeStruct((P_HEADS, N_KEYS, n), BF16)
    return pl.pallas_call(
        _post_mix_kernel,
        grid=(B, nT),
        in_specs=[pl.BlockSpec((1, tb, o.shape[2]), tok), pl.BlockSpec((1, tb, D), tok),
                  _mod_spec(gate, tb), _mod_spec(shift, tb), _mod_spec(scale, tb),
                  _full_spec(nw), _full_spec(w_o), _full_spec(wq_heads), _full_spec(sk1), _full_spec(sk2)],
        out_specs=[pl.BlockSpec((1, tb, D), tok), pl.BlockSpec((1, tb, D), tok),
                   rt_spec, rt_spec, rt_spec, rt_spec],
        out_shape=[jax.ShapeDtypeStruct((B, T, D), F32), jax.ShapeDtypeStruct((B, T, D), BF16),
                   rt_shape_b, rt_shape, rt_shape_b, rt_shape],
        scratch_shapes=[pltpu.VMEM((P_TOPK, tb), F32), pltpu.VMEM((P_TOPK, tb), F32),
                        pltpu.VMEM((P_HEADS, 2, N_KEYS, tb), F32), pltpu.VMEM((P_HEADS, 8, tb), F32)],
        compiler_params=_cparams(("parallel", "parallel")),
        name="post_mix_route",
    )(o, x, gate, shift, scale, nw, w_o, wq_heads, sk1, sk2)


def _gelu_tanh(a):
    c = math.sqrt(2.0 / math.pi)
    return 0.5 * a * (1.0 + jnp.tanh(c * (a + 0.044715 * (a * a * a))))


def _peer_dense_kernel(h_ref, u_ref, vt_ref, rank_ref, cnt_ref, e2_ref, r_ref, x_ref, g_ref,
                       o_ref, acc_ref, a0_ref, a1_ref, hc0_ref, hc1_ref, *, ec, ne):
    g = pl.program_id(0)
    per = ec // N_KEYS
    s = lax.rem(jnp.maximum(g - 1, 0), ne) + 1
    c3 = lax.rem(jnp.maximum(g - 2, 0), ne)

    @pl.when(g == 0)
    def _():
        for ref in (a0_ref, a1_ref, hc0_ref, hc1_ref):
            ref[...] = jnp.zeros(ref.shape, ref.dtype)

    @pl.when(c3 == 0)
    def _():
        acc_ref[...] = jnp.zeros(acc_ref.shape, F32)

    def stages(a_out, a_in, hc_out, hc_in):
        zero = jnp.zeros((), BF16)
        tb = h_ref.shape[1]
        nsplit = PEER_SPLIT
        gate_w = min(GATE_LANES, tb)
        mxu_w = min(MXU_N, tb)
        assert tb % gate_w == 0 and tb % mxu_w == 0

        row_cache = {}

        def routed_rows(ii):
            if ii not in row_cache:
                i1 = jnp.clip((s - 1) * per + ii, 0, N_KEYS - 1)
                row_cache[ii] = [(cnt_ref[h, pl.ds(i1, 1), :].astype(BF16),
                                  r_ref[h, pl.ds(i1, 1), :].astype(BF16)) for h in range(P_HEADS)]
            return row_cache[ii]

        def gate_piece(ii, n):
            cols = slice(n * gate_w, (n + 1) * gate_w)
            rows = slice(ii * N_KEYS, (ii + 1) * N_KEYS)
            w = None
            for h, (cnt_row, r_row) in enumerate(routed_rows(ii)):
                contrib = jnp.where(rank_ref[h, :, cols] < cnt_row[:, cols], e2_ref[h, :, cols], zero) * r_row[:, cols]
                w = contrib if w is None else w + contrib
            hc_out[rows, cols] = _gelu_tanh(a_in[rows, cols]) * w

        def score_piece(q, n):
            cols = slice(n * mxu_w, (n + 1) * mxu_w)
            rows = slice(q * (ec // nsplit), (q + 1) * (ec // nsplit))
            a_out[rows, cols] = lax.dot_general(u_ref[rows, :], h_ref[0, cols, :], _NT,
                                                preferred_element_type=F32).astype(BF16)

        def out_piece(q, n):
            cols = slice(n * mxu_w, (n + 1) * mxu_w)
            rows = slice(q * (D_MODEL // nsplit), (q + 1) * (D_MODEL // nsplit))
            acc_ref[rows, cols] += jnp.dot(vt_ref[0, rows, :], hc_in[:, cols], preferred_element_type=F32)

        vec = [functools.partial(gate_piece, ii, n) for n in range(tb // gate_w) for ii in range(per)]
        mxu = []
        for n in range(tb // mxu_w):
            for q in range(nsplit):
                mxu.append(functools.partial(score_piece, q, n))
                mxu.append(functools.partial(out_piece, q, n))
        for k in range(max(len(vec), len(mxu))):
            if k < len(mxu):
                mxu[k]()
            if k < len(vec):
                vec[k]()

    even = lax.rem(g, 2) == 0
    pl.when(even)(lambda: stages(a0_ref, a1_ref, hc1_ref, hc0_ref))
    pl.when(jnp.logical_not(even))(lambda: stages(a1_ref, a0_ref, hc0_ref, hc1_ref))

    @pl.when((c3 == ne - 1) & (g >= 2))
    def _():
        o_ref[0] = x_ref[0] + g_ref[0] * acc_ref[...].T


def _peer_dense(h2, u, vt, rank, cnt, e2, r, x, gate, tb, ec=PEER_EC):
    B, T, D = x.shape
    nT = T // tb
    nblk = B * nT
    ne = N_EXPERTS // ec
    assert vt.shape == (ne, D, ec)
    blk1 = lambda g: jnp.minimum(g // ne, nblk - 1)
    blk2 = lambda g: jnp.minimum(jnp.maximum(g - 1, 0) // ne, nblk - 1)
    blk3 = lambda g: jnp.maximum(g - 2, 0) // ne
    tok1 = lambda g: (blk1(g) // nT, blk1(g) % nT, 0)
    tok3 = lambda g: (blk3(g) // nT, blk3(g) % nT, 0)
    rt_spec = pl.BlockSpec((P_HEADS, N_KEYS, tb), lambda g: (0, 0, blk2(g)))
    if gate.shape[1] == 1:
        gate_spec = pl.BlockSpec((1, 1, D), lambda g: (blk3(g) // nT, 0, 0))
    else:
        gate_spec = pl.BlockSpec((1, tb, D), tok3)
    return pl.pallas_call(
        functools.partial(_peer_dense_kernel, ec=ec, ne=ne),
        grid=(nblk * ne + 2,),
        in_specs=[pl.BlockSpec((1, tb, D), tok1),
                  pl.BlockSpec((ec, D), lambda g: (g % ne, 0)),
                  pl.BlockSpec((1, D, ec), lambda g: (jnp.maximum(g - 2, 0) % ne, 0, 0)),
                  rt_spec, rt_spec, rt_spec, rt_spec,
                  pl.BlockSpec((1, tb, D), tok3), gate_spec],
        out_specs=pl.BlockSpec((1, tb, D), tok3),
        out_shape=jax.ShapeDtypeStruct((B, T, D), F32),
        scratch_shapes=[pltpu.VMEM((D, tb), F32),
                        pltpu.VMEM((ec, tb), BF16), pltpu.VMEM((ec, tb), BF16),
                        pltpu.VMEM((ec, tb), BF16), pltpu.VMEM((ec, tb), BF16)],
        compiler_params=_cparams(("arbitrary",)),
        name="peer_dense",
    )(h2, u, vt, rank, cnt, e2, r, x, gate)


def _pair_rms(y, w, lane):
    y2 = y * y
    lo = jnp.sum(jnp.where(lane < B_HEAD_DIM, y2, 0.0), axis=-1, keepdims=True)
    hi = jnp.sum(y2, axis=-1, keepdims=True) - lo
    ms = jnp.where(lane < B_HEAD_DIM, lo, hi) * (1.0 / B_HEAD_DIM)
    return y * lax.rsqrt(ms + EPS) * w


def _shared_kv_kernel(x_ref, sh_ref, sc_ref, nw_ref, wk_ref, wv_ref, kn_ref, k_ref, v_ref, kb_ref, vb_ref):
    h = _modulate(x_ref[0], nw_ref[...], sh_ref[0], sc_ref[0]).astype(BF16)
    kraw = jnp.dot(h, wk_ref[...], preferred_element_type=F32)
    v = jnp.dot(h, wv_ref[...], preferred_element_type=F32)
    v_ref[0] = v
    vb_ref[0] = v.astype(BF16)
    knw = kn_ref[...]
    lane = lax.broadcasted_iota(jnp.int32, (h.shape[0], LANES), 1)
    for hp in range(B_HEADS // 2):
        kh = _pair_rms(kraw[:, hp * LANES:(hp + 1) * LANES], knw, lane)
        k_ref[0, :, hp * LANES:(hp + 1) * LANES] = kh
        kb_ref[0, :, hp * LANES:(hp + 1) * LANES] = kh.astype(BF16)


def _shared_kv(x, shift, scale, nw, wk, wv, kn_pair, tb):
    B, T, D = x.shape
    tok = lambda b, i: (b, i, 0)
    blk = pl.BlockSpec((1, tb, D), tok)
    return pl.pallas_call(
        _shared_kv_kernel,
        grid=(B, T // tb),
        in_specs=[blk, _mod_spec(shift, tb), _mod_spec(scale, tb), _full_spec(nw),
                  _full_spec(wk), _full_spec(wv), _full_spec(kn_pair)],
        out_specs=[blk, blk, blk, blk],
        out_shape=[jax.ShapeDtypeStruct((B, T, D), F32), jax.ShapeDtypeStruct((B, T, D), F32),
                   jax.ShapeDtypeStruct((B, T, D), BF16), jax.ShapeDtypeStruct((B, T, D), BF16)],
        compiler_params=_cparams(("parallel", "parallel")),
        name="shared_kv",
    )(x, shift, scale, nw, wk, wv, kn_pair)


def _band_pre_kernel(x_ref, sh_ref, sc_ref, nw_ref, wq_ref, qn_ref, q_ref):
    h = _modulate(x_ref[0], nw_ref[...], sh_ref[0], sc_ref[0]).astype(BF16)
    q = jnp.dot(h, wq_ref[...], preferred_element_type=F32)
    qnw = qn_ref[...]
    lane = lax.broadcasted_iota(jnp.int32, (h.shape[0], LANES), 1)
    scale = B_HEAD_DIM ** -0.5
    for hp in range(B_HEADS // 2):
        qh = _pair_rms(q[:, hp * LANES:(hp + 1) * LANES], qnw, lane)
        q_ref[0, :, hp * LANES:(hp + 1) * LANES] = (qh * scale).astype(BF16)


def _band_pre(x, shift, scale, nw, wq, qn_pair, tb):
    B, T, D = x.shape
    tok = lambda b, i: (b, i, 0)
    blk = pl.BlockSpec((1, tb, D), tok)
    return pl.pallas_call(
        _band_pre_kernel,
        grid=(B, T // tb),
        in_specs=[blk, _mod_spec(shift, tb), _mod_spec(scale, tb), _full_spec(nw),
                  _full_spec(wq), _full_spec(qn_pair)],
        out_specs=blk,
        out_shape=jax.ShapeDtypeStruct((B, T, D), BF16),
        compiler_params=_cparams(("parallel", "parallel")),
        name="band_pre",
    )(x, shift, scale, nw, wq, qn_pair)


def _band_bias_kernel(tab_ref, o_ref):
    h = pl.program_id(0)
    nvar = 2 * LANES
    r = lax.broadcasted_iota(jnp.int32, (BAND_SUB, nvar), 0)
    w = lax.broadcasted_iota(jnp.int32, (BAND_SUB, nvar), 1) + (BAND_WIN - nvar)
    idx = jnp.clip(r + B_WINDOW - w, -REL_CLIP, REL_CLIP) + REL_CLIP
    far = tab_ref[h, 2 * REL_CLIP]

    def body(t, acc):
        return jnp.where(idx == t, tab_ref[h, t], acc)

    var = lax.fori_loop(0, 2 * REL_CLIP, body, jnp.full((BAND_SUB, nvar), far, F32))
    full = jnp.concatenate([jnp.full((BAND_SUB, BAND_WIN - nvar), far, F32), var], axis=1)
    rr = lax.broadcasted_iota(jnp.int32, (BAND_SUB, BAND_WIN), 0)
    ww = lax.broadcasted_iota(jnp.int32, (BAND_SUB, BAND_WIN), 1)
    qc = rr // CHUNK + LEFT_CHUNKS
    kc = ww // CHUNK
    allowed = (kc <= qc) & (kc >= qc - LEFT_CHUNKS)
    o_ref[0] = jnp.where(allowed, full, NEG)


def _band_bias(table):
    nh = table.shape[0]
    return pl.pallas_call(
        _band_bias_kernel,
        grid=(nh,),
        in_specs=[pl.BlockSpec(memory_space=pltpu.SMEM)],
        out_specs=pl.BlockSpec((1, BAND_SUB, BAND_WIN), lambda h: (h, 0, 0)),
        out_shape=jax.ShapeDtypeStruct((nh, BAND_SUB, BAND_WIN), F32),
        compiler_params=_cparams(("arbitrary",)),
        name="band_bias",
    )(table)


def _band_windows(windows, bias_ref, lane):
    chains = [(w, hh) for w in range(len(windows)) for hh in range(2)]
    scores = []
    for w, hh in chains:
        qs, kw, _ = windows[w]
        nkw = kw.shape[0]
        sel = (lane < B_HEAD_DIM) if hh == 0 else (lane >= B_HEAD_DIM)
        qh = jnp.where(sel, qs, jnp.zeros_like(qs))
        s = lax.dot_general(qh, kw, _NT, preferred_element_type=F32)
        scores.append(s + bias_ref[hh, :, BAND_WIN - nkw:BAND_WIN])
    maxes = [jnp.max(s, axis=-1, keepdims=True) for s in scores]
    probs = [jnp.exp((s - m).astype(BF16)) for s, m in zip(scores, maxes)]
    vext = []
    for _, _, vw in windows:
        ones_lane = (lax.broadcasted_iota(jnp.int32, vw.shape, 1) == 0).astype(BF16)
        vext.append(jnp.concatenate([vw, ones_lane], axis=1))
    outs = []
    for (w, hh), p in zip(chains, probs):
        pv = jnp.dot(p, vext[w], preferred_element_type=F32)
        outs.append(pv[:, :LANES] / pv[:, LANES:LANES + 1])
    return [jnp.where(lane < B_HEAD_DIM, outs[2 * w], outs[2 * w + 1]) for w in range(len(windows))]


def _band_attn_kernel(q_ref, kp_ref, kc_ref, vp_ref, vc_ref, bias_ref, o_ref, *, tq):
    i = pl.program_id(2)
    lane = lax.broadcasted_iota(jnp.int32, (BAND_SUB, LANES), 1)
    nsub = tq // BAND_SUB

    def run(first):
        windows = []
        for c in range(nsub):
            qs = q_ref[0, c * BAND_SUB:(c + 1) * BAND_SUB, :]
            hi = (c + 1) * BAND_SUB
            if first or hi >= BAND_WIN:
                lo = max(hi - BAND_WIN, 0)
                kw = kc_ref[0, lo:hi, :]
                vw = vc_ref[0, lo:hi, :]
            else:
                lo = tq - (BAND_WIN - hi)
                kw = jnp.concatenate([kp_ref[0, lo:tq, :], kc_ref[0, 0:hi, :]], axis=0)
                vw = jnp.concatenate([vp_ref[0, lo:tq, :], vc_ref[0, 0:hi, :]], axis=0)
            windows.append((qs, kw, vw))
        for c, o in enumerate(_band_windows(windows, bias_ref, lane)):
            o_ref[0, c * BAND_SUB:(c + 1) * BAND_SUB, :] = o.astype(BF16)

    pl.when(i == 0)(lambda: run(True))
    pl.when(i > 0)(lambda: run(False))


def _band_attn(q, k, v, bias, tq):
    B, T, D = q.shape
    assert tq >= B_WINDOW and T % tq == 0
    cur = lambda b, hp, i: (b, i, hp)
    prev = lambda b, hp, i: (b, jnp.maximum(i - 1, 0), hp)
    blk = lambda m: pl.BlockSpec((1, tq, LANES), m)
    return pl.pallas_call(
        functools.partial(_band_attn_kernel, tq=tq),
        grid=(B, B_HEADS // 2, T // tq),
        in_specs=[blk(cur), blk(prev), blk(cur), blk(prev), blk(cur),
                  pl.BlockSpec((2, BAND_SUB, BAND_WIN), lambda b, hp, i: (hp, 0, 0))],
        out_specs=blk(cur),
        out_shape=jax.ShapeDtypeStruct((B, T, D), BF16),
        compiler_params=_cparams(("parallel", "parallel", "arbitrary")),
        name="band_attn",
    )(q, k, k, v, v, bias)


def _band_step_kernel(q_ref, k_ref, v_ref, bias_ref, o_ref):
    lane = lax.broadcasted_iota(jnp.int32, (BAND_SUB, LANES), 1)
    o, = _band_windows([(q_ref[0], k_ref[0], v_ref[0])], bias_ref, lane)
    o_ref[0] = o.astype(BF16)


def _band_step(q, kwin, vwin, bias):
    B, _, D = q.shape
    return pl.pallas_call(
        _band_step_kernel,
        grid=(B, B_HEADS // 2),
        in_specs=[pl.BlockSpec((1, BAND_SUB, LANES), lambda b, hp: (b, 0, hp)),
                  pl.BlockSpec((1, BAND_WIN, LANES), lambda b, hp: (b, 0, hp)),
                  pl.BlockSpec((1, BAND_WIN, LANES), lambda b, hp: (b, 0, hp)),
                  pl.BlockSpec((2, BAND_SUB, BAND_WIN), lambda b, hp: (hp, 0, 0))],
        out_specs=pl.BlockSpec((1, BAND_SUB, LANES), lambda b, hp: (b, 0, hp)),
        out_shape=jax.ShapeDtypeStruct((B, BAND_SUB, D), BF16),
        compiler_params=_cparams(("parallel", "parallel")),
        name="band_step",
    )(q, kwin, vwin, bias)


def _rope_tables(pos):
    half = ROPE_DIM // 2
    freqs = ROPE_THETA ** (-jnp.arange(half, dtype=F32) / half)
    ang = pos.astype(F32)[:, None] * freqs[None, :]
    c, s = jnp.cos(ang), jnp.sin(ang)
    n = pos.shape[0]
    ones = jnp.ones((n, NOPE_DIM), F32)
    zeros = jnp.zeros((n, NOPE_DIM), F32)
    pad1 = jnp.ones((n, HEAD_PAD - QK_DIM), F32)
    pad0 = jnp.zeros((n, HEAD_PAD - QK_DIM), F32)
    return (jnp.concatenate([ones, c, c, pad1], axis=1),
            jnp.concatenate([zeros, -s, s, pad0], axis=1))


def _pad_heads(w, nheads, width):
    lead = w.shape[:-1]
    w = w.reshape(lead + (nheads, width))
    w = jnp.pad(w, [(0, 0)] * len(lead) + [(0, 0), (0, HEAD_PAD - width)])
    return w.reshape(lead + (nheads * HEAD_PAD,))


def _block(n, pref):
    for t in pref:
        if n % t == 0:
            return t
    return n


def kernel(x_prompt, x_sample, c_prompt, c_sample, cache_a_ckv, cache_a_krope, cache_b_k, cache_b_v, ada_w, ada_b, norm_mix_w, norm_ffn_w, a_w_in, a_q_lora_norm, a_kv_lora_norm, a_w_uq, a_w_ukv, a_q_norm, a_k_norm, a_w_o, kv_ada_w, kv_ada_b, kv_norm_w, b_w_kv, b_k_norm, b_w_q, b_q_norm, b_rel_bias, b_w_o, p_w_q, p_subkeys, p_u, p_v):
    D = D_MODEL
    Bp, Tp, _ = x_prompt.shape
    Bs, Ts, _ = x_sample.shape
    P = cache_a_ckv.shape[2]
    Pb = cache_b_k.shape[1]
    assert Ts == CHUNK and Pb == B_WINDOW and P % CHUNK == 0

    ada_all_w = jnp.concatenate([ada_w[0], ada_w[1], kv_ada_w], axis=1).astype(BF16)
    ada_all_b = jnp.concatenate([ada_b[0], ada_b[1], kv_ada_b])[None, :]
    c_all = jnp.concatenate([c_prompt, c_sample], axis=0)
    mod = _ada(c_all, ada_all_w, ada_all_b)

    w_in = a_w_in[0].astype(BF16)
    qln = a_q_lora_norm[0][None, :]
    kvln = a_kv_lora_norm[0][None, :]
    wuq_pad = _pad_heads(a_w_uq[0], A_HEADS, QK_DIM).astype(BF16)
    wukv = a_w_ukv[0].reshape(KV_LORA, A_HEADS, NOPE_DIM + V_DIM)
    wuk_pad = jnp.pad(wukv[:, :, :NOPE_DIM], ((0, 0), (0, 0), (0, HEAD_PAD - NOPE_DIM))
                      ).reshape(KV_LORA, A_HEADS * HEAD_PAD).astype(BF16)
    wuv_pad = jnp.pad(wukv[:, :, NOPE_DIM:], ((0, 0), (0, 0), (0, HEAD_PAD - V_DIM))
                      ).reshape(KV_LORA, A_HEADS * HEAD_PAD).astype(BF16)
    vone = jnp.asarray(np.tile(np.eye(1, HEAD_PAD, k=V_DIM), (1, A_HEADS)), F32)
    qn_pad = jnp.pad(a_q_norm[0], (0, HEAD_PAD - QK_DIM))[None, :]
    kn_pad = jnp.pad(a_k_norm[0], (0, HEAD_PAD - QK_DIM))[None, :]
    place = jnp.asarray(np.eye(ROPE_DIM, HEAD_PAD, k=NOPE_DIM), BF16)
    a_wo = a_w_o[0].astype(BF16)
    wk_b = b_w_kv[:, :D].astype(BF16)
    wv_b = b_w_kv[:, D:].astype(BF16)
    bkn_pair = jnp.tile(b_k_norm, 2)[None, :]
    bqn_pair = jnp.tile(b_q_norm[0], 2)[None, :]
    bwq = b_w_q[0].astype(BF16)
    bwo = b_w_o[0].astype(BF16)
    bias_tile = _band_bias(b_rel_bias[0])

    def peer_weights(layer):
        wq = p_w_q[layer].astype(BF16)
        sk = p_subkeys[layer]
        sk1 = jnp.pad(sk[:, 0], ((0, 0), (0, 0), (0, P_HALF))).astype(BF16)
        sk2 = jnp.pad(sk[:, 1], ((0, 0), (0, 0), (P_HALF, 0))).astype(BF16)
        vt = p_v[layer].astype(BF16).reshape(N_EXPERTS // PEER_EC, PEER_EC, D).transpose(0, 2, 1)
        return wq, sk1, sk2, p_u[layer].astype(BF16), vt

    peer_w = [peer_weights(0), peer_weights(1)]
    norm_mix = norm_mix_w[:, None, :]
    norm_ffn = norm_ffn_w[:, None, :]
    kv_nw = kv_norm_w[None, :]

    def run(x, modp, per_token, pos_q, past):
        B, T, _ = x.shape
        if per_token:
            modv = jnp.repeat(modp, T, axis=0)[None]
            xw = x.reshape(1, B * T, D)
        else:
            modv = modp[:, None, :]
            xw = x
        Bw, Tw, _ = xw.shape
        sl = lambda k: modv[:, :, k * D:(k + 1) * D]
        tb = _block(Tw, (512, 256, 128))
        tbr = _block(Tw, (256, 128))
        tbe = _block(Tw, (512, 256, 128))

        cos_q, sin_q = _rope_tables(pos_q)
        ckv, kr, q = _mla_pre(xw, sl(0), sl(1), norm_mix[0], w_in, qln, kvln, wuq_pad, qn_pad,
                              cos_q, sin_q, tb)
        ckv = ckv.reshape(B, T, KV_LORA)
        kr = kr.reshape(B, T, ROPE_DIM)
        q = q.reshape(B, T, A_HEADS * HEAD_PAD)
        if past is None:
            ckv_all, kr_all = ckv, kr
            pos_k = pos_q[:T]
        else:
            ckv_all = jnp.concatenate([past[0], ckv], axis=1)
            kr_all = jnp.concatenate([past[1], kr], axis=1)
            pos_k = jnp.arange(P + T, dtype=jnp.int32)
        S = ckv_all.shape[1]
        cos_k, sin_k = _rope_tables(pos_k)
        tbk = _block(S, (512, 704, 256, 192, 64))
        k, v = _mla_kv(ckv_all, kr_all, wuk_pad, wuv_pad, vone, kn_pad, place, cos_k, sin_k, tbk)
        if past is None:
            ta = _block(T, (512, 256, 128, 64))
            o = _mla_attn(q, k, v, ta, ta, True)
        else:
            o = _mla_attn(q, k, v, T, tbk, False)
        o = o.reshape(Bw, Tw, A_HEADS * V_DIM)

        wq, sk1, sk2, u, vt = peer_w[0]
        x1, h2, rank, cnt, e2, r = _post_mix(o, xw, sl(2), sl(3), sl(4), norm_ffn[0], a_wo, wq, sk1, sk2, tbr)
        x2 = _peer_dense(h2, u, vt, rank, cnt, e2, r, x1, sl(5), tbe)

        kf, vf, kb, vb = _shared_kv(x2, sl(12), sl(13), kv_nw, wk_b, wv_b, bkn_pair, tb)

        qb = _band_pre(x2, sl(6), sl(7), norm_mix[1], bwq, bqn_pair, tb)
        if past is None:
            ob = _band_attn(qb, kb, vb, bias_tile, B_WINDOW)
            new_bk = kf[:, -B_WINDOW:].reshape(B, B_WINDOW, B_HEADS, B_HEAD_DIM)
            new_bv = vf[:, -B_WINDOW:].reshape(B, B_WINDOW, B_HEADS, B_HEAD_DIM)
        else:
            zq = jnp.zeros((B, CHUNK, D), BF16)
            qpad = jnp.concatenate([zq, qb.reshape(B, T, D)], axis=1)
            kwin = jnp.concatenate([zq, past[2].reshape(B, Pb, D).astype(BF16), kb.reshape(B, T, D)], axis=1)
            vwin = jnp.concatenate([zq, past[3].reshape(B, Pb, D).astype(BF16), vb.reshape(B, T, D)], axis=1)
            ob = _band_step(qpad, kwin, vwin, bias_tile)[:, CHUNK:].reshape(Bw, Tw, D)
            new_bk = jnp.concatenate([past[2], kf.reshape(B, T, B_HEADS, B_HEAD_DIM)], axis=1)[:, -Pb:]
            new_bv = jnp.concatenate([past[3], vf.reshape(B, T, B_HEADS, B_HEAD_DIM)], axis=1)[:, -Pb:]

        wq, sk1, sk2, u, vt = peer_w[1]
        x3, h4, rank, cnt, e2, r = _post_mix(ob, x2, sl(8), sl(9), sl(10), norm_ffn[1], bwo, wq, sk1, sk2, tbr)
        y = _peer_dense(h4, u, vt, rank, cnt, e2, r, x3, sl(11), tbe)
        return y.reshape(B, T, D), ckv[None], kr[None], new_bk, new_bv

    pos_p = jnp.arange(Tp, dtype=jnp.int32)
    pos_s = jnp.tile(P + jnp.arange(Ts, dtype=jnp.int32), Bs)
    y_p, p_ckv, p_kr, p_bk, p_bv = run(x_prompt, mod[:Bp], False, pos_p, None)
    y_s, s_ckv, s_kr, s_bk, s_bv = run(x_sample, mod[Bp:], True, pos_s,
                                       (cache_a_ckv[0], cache_a_krope[0], cache_b_k, cache_b_v))
    return (y_p, y_s, p_ckv, p_kr, p_bk, p_bv, s_ckv, s_kr, s_bk, s_bv)
```

```python
import functools
import math

import numpy as np
import jax
import jax.numpy as jnp
from jax import lax
from jax.experimental import pallas as pl
from jax.experimental.pallas import tpu as pltpu

F32 = jnp.float32
BF16 = jnp.bfloat16

D_MODEL = 1024
CHUNK = 64
A_HEADS = 16
Q_LORA = 384
KV_LORA = 256
NOPE_DIM = 64
ROPE_DIM = 32
V_DIM = 64
QK_DIM = NOPE_DIM + ROPE_DIM
ROPE_THETA = 10000.0
B_HEADS = 16
B_HEAD_DIM = 64
LEFT_CHUNKS = 8
B_WINDOW = LEFT_CHUNKS * CHUNK
REL_CLIP = 128
P_HEADS = 8
N_KEYS = 128
N_EXPERTS = N_KEYS * N_KEYS
P_HALF = 64
P_TOPK = 16
NEG = -1e30
EPS = 1e-6

LANES = 128
BF16_ROWS = 16
HEAD_PAD = 128
BAND_SUB = 2 * CHUNK
BAND_WIN = B_WINDOW + BAND_SUB
VMEM_LIMIT = 56 * 1024 * 1024
PEER_EC = 512
PEER_SPLIT = 2
MXU_N = 256
GATE_LANES = 256

_NT = (((1,), (1,)), ((), ()))


def _cparams(sem):
    return pltpu.CompilerParams(dimension_semantics=sem, vmem_limit_bytes=VMEM_LIMIT)


def _modulate(x, w, shift, scale):
    ms = jnp.mean(x * x, axis=-1, keepdims=True)
    y = x * lax.rsqrt(ms + EPS) * w
    return y * (1.0 + scale) + shift


def _rms(x, w):
    ms = jnp.mean(x * x, axis=-1, keepdims=True)
    return x * lax.rsqrt(ms + EPS) * w


def _mod_spec(arr, tb):
    if arr.shape[1] == 1:
        return pl.BlockSpec((1, 1, arr.shape[2]), lambda b, i, *_: (b, 0, 0))
    return pl.BlockSpec((1, tb, arr.shape[2]), lambda b, i, *_: (b, i, 0))


def _full_spec(arr):
    nd = arr.ndim
    return pl.BlockSpec(arr.shape, lambda *_: (0,) * nd)


def _ada_kernel(c_ref, w_ref, b_ref, o_ref):
    c = c_ref[...]
    a = c / (1.0 + jnp.exp(-c))
    o_ref[...] = jnp.dot(a.astype(BF16), w_ref[...], preferred_element_type=F32) + b_ref[...]


def _ada(c, w, b, tn=2048):
    m, k = c.shape
    n = w.shape[1]
    return pl.pallas_call(
        _ada_kernel,
        grid=(n // tn,),
        in_specs=[pl.BlockSpec((m, k), lambda j: (0, 0)),
                  pl.BlockSpec((k, tn), lambda j: (0, j)),
                  pl.BlockSpec((1, tn), lambda j: (0, j))],
        out_specs=pl.BlockSpec((m, tn), lambda j: (0, j)),
        out_shape=jax.ShapeDtypeStruct((m, n), F32),
        compiler_params=_cparams(("arbitrary",)),
        name="ada_mod",
    )(c, w, b)


def _rope_head(xh, cos, sin, lane):
    rolled = jnp.where(lane < NOPE_DIM + ROPE_DIM // 2,
                       pltpu.roll(xh, LANES - ROPE_DIM // 2, 1),
                       pltpu.roll(xh, ROPE_DIM // 2, 1))
    return xh * cos + rolled * sin


def _mla_pre_kernel(x_ref, sh_ref, sc_ref, nw_ref, win_ref, qln_ref, kvln_ref, wuq_ref, qn_ref,
                    cos_ref, sin_ref, ckv_ref, kr_ref, q_ref):
    x = x_ref[0]
    h = _modulate(x, nw_ref[...], sh_ref[0], sc_ref[0])
    proj = jnp.dot(h.astype(BF16), win_ref[...], preferred_element_type=F32)
    c_q = _rms(proj[:, :Q_LORA], qln_ref[...])
    ckv_ref[0] = _rms(proj[:, Q_LORA:Q_LORA + KV_LORA], kvln_ref[...])
    kr_ref[0] = proj[:, Q_LORA + KV_LORA:]
    q = jnp.dot(c_q.astype(BF16), wuq_ref[...], preferred_element_type=F32)
    cos = cos_ref[...]
    sin = sin_ref[...]
    qn = qn_ref[...]
    lane = lax.broadcasted_iota(jnp.int32, cos.shape, 1)
    scale = QK_DIM ** -0.5
    for hd in range(A_HEADS):
        qh = q[:, hd * HEAD_PAD:(hd + 1) * HEAD_PAD]
        ss = jnp.sum(qh * qh, axis=-1, keepdims=True) * (1.0 / QK_DIM)
        qh = qh * lax.rsqrt(ss + EPS) * qn
        qh = _rope_head(qh, cos, sin, lane)
        q_ref[0, :, hd * HEAD_PAD:(hd + 1) * HEAD_PAD] = (qh * scale).astype(BF16)


def _mla_pre(x, shift, scale, nw, w_in, qln, kvln, wuq_pad, qn_pad, cos_t, sin_t, tb):
    B, T, D = x.shape
    nT = T // tb
    tab_spec = pl.BlockSpec((tb, LANES), lambda b, i: (i, 0))
    return pl.pallas_call(
        _mla_pre_kernel,
        grid=(B, nT),
        in_specs=[pl.BlockSpec((1, tb, D), lambda b, i: (b, i, 0)),
                  _mod_spec(shift, tb), _mod_spec(scale, tb),
                  _full_spec(nw), _full_spec(w_in), _full_spec(qln), _full_spec(kvln),
                  _full_spec(wuq_pad), _full_spec(qn_pad), tab_spec, tab_spec],
        out_specs=[pl.BlockSpec((1, tb, KV_LORA), lambda b, i: (b, i, 0)),
                   pl.BlockSpec((1, tb, ROPE_DIM), lambda b, i: (b, i, 0)),
                   pl.BlockSpec((1, tb, A_HEADS * HEAD_PAD), lambda b, i: (b, i, 0))],
        out_shape=[jax.ShapeDtypeStruct((B, T, KV_LORA), F32),
                   jax.ShapeDtypeStruct((B, T, ROPE_DIM), F32),
                   jax.ShapeDtypeStruct((B, T, A_HEADS * HEAD_PAD), BF16)],
        compiler_params=_cparams(("parallel", "parallel")),
        name="mla_pre",
    )(x, shift, scale, nw, w_in, qln, kvln, wuq_pad, qn_pad, cos_t, sin_t)


def _mla_kv_kernel(ckv_ref, kr_ref, wuk_ref, wuv_ref, vone_ref, kn_ref, place_ref, cos_ref, sin_ref,
                   k_ref, v_ref):
    ckv = ckv_ref[0].astype(BF16)
    kn = jnp.dot(ckv, wuk_ref[...], preferred_element_type=F32)
    v_ref[0] = (jnp.dot(ckv, wuv_ref[...], preferred_element_type=F32) + vone_ref[...]).astype(BF16)
    kr = kr_ref[0]
    kr_hi = kr.astype(BF16)
    kr_lo = (kr - kr_hi.astype(F32)).astype(BF16)
    place = place_ref[...]
    krp = (jnp.dot(kr_hi, place, preferred_element_type=F32)
           + jnp.dot(kr_lo, place, preferred_element_type=F32))
    cos = cos_ref[...]
    sin = sin_ref[...]
    knw = kn_ref[...]
    lane = lax.broadcasted_iota(jnp.int32, cos.shape, 1)
    for hd in range(A_HEADS):
        kh = kn[:, hd * HEAD_PAD:(hd + 1) * HEAD_PAD] + krp
        ss = jnp.sum(kh * kh, axis=-1, keepdims=True) * (1.0 / QK_DIM)
        kh = kh * lax.rsqrt(ss + EPS) * knw
        kh = _rope_head(kh, cos, sin, lane)
        k_ref[0, :, hd * HEAD_PAD:(hd + 1) * HEAD_PAD] = kh.astype(BF16)


def _mla_kv(ckv, kr, wuk_pad, wuv_pad, vone, kn_pad, place, cos_t, sin_t, tb):
    B, S, _ = ckv.shape
    tab_spec = pl.BlockSpec((tb, LANES), lambda b, i: (i, 0))
    wide = pl.BlockSpec((1, tb, A_HEADS * HEAD_PAD), lambda b, i: (b, i, 0))
    return pl.pallas_call(
        _mla_kv_kernel,
        grid=(B, S // tb),
        in_specs=[pl.BlockSpec((1, tb, KV_LORA), lambda b, i: (b, i, 0)),
                  pl.BlockSpec((1, tb, ROPE_DIM), lambda b, i: (b, i, 0)),
                  _full_spec(wuk_pad), _full_spec(wuv_pad), _full_spec(vone), _full_spec(kn_pad),
                  _full_spec(place), tab_spec, tab_spec],
        out_specs=[wide, wide],
        out_shape=[jax.ShapeDtypeStruct((B, S, A_HEADS * HEAD_PAD), BF16),
                   jax.ShapeDtypeStruct((B, S, A_HEADS * HEAD_PAD), BF16)],
        compiler_params=_cparams(("parallel", "parallel")),
        name="mla_kv",
    )(ckv, kr, wuk_pad, wuv_pad, vone, kn_pad, place, cos_t, sin_t)


def _mla_attn_kernel(q_ref, k_ref, v_ref, o_ref, m_sc, acc_sc, *, tq, tk, causal, nk):
    i = pl.program_id(2)
    m_sc[...] = jnp.full(m_sc.shape, -jnp.inf, F32)
    acc_sc[...] = jnp.zeros(acc_sc.shape, F32)

    def block(start, masked):
        k = k_ref[0, pl.ds(start, tk), :]
        v = v_ref[0, pl.ds(start, tk), :]
        if masked:
            qc = lax.broadcasted_iota(jnp.int32, (tq, tk), 0) // CHUNK
            kc = lax.broadcasted_iota(jnp.int32, (tq, tk), 1) // CHUNK
            allowed = kc <= qc
        for hh in range(2):
            qh = q_ref[0, :, hh * HEAD_PAD:(hh + 1) * HEAD_PAD]
            s = lax.dot_general(qh, k[:, hh * HEAD_PAD:(hh + 1) * HEAD_PAD], _NT,
                                preferred_element_type=F32)
            if masked:
                s = jnp.where(allowed, s, NEG)
            m_prev = m_sc[hh]
            m_new = jnp.maximum(m_prev, jnp.max(s, axis=-1, keepdims=True))
            alpha = jnp.exp(m_prev - m_new)
            if tk % LANES == 0:
                m_wide = jnp.concatenate([m_new] * (tk // LANES), axis=1)
            else:
                m_wide = m_new[:, :1]
            p = jnp.exp((s - m_wide).astype(BF16))
            acc_sc[hh] = alpha * acc_sc[hh] + jnp.dot(p, v[:, hh * HEAD_PAD:(hh + 1) * HEAD_PAD],
                                                      preferred_element_type=F32)
            m_sc[hh] = m_new

    def body(j, carry):
        block(pl.multiple_of(j * tk, tk), False)
        return carry

    if causal:
        lax.fori_loop(0, i, body, 0)
        block(pl.multiple_of(i * tk, tk), True)
    else:
        lax.fori_loop(0, nk, body, 0)

    lane = lax.broadcasted_iota(jnp.int32, (tq, LANES), 1)
    outs = []
    for hh in range(2):
        acc = acc_sc[hh]
        outs.append(acc / acc[:, V_DIM:V_DIM + 1])
    o = jnp.where(lane < V_DIM, outs[0], pltpu.roll(outs[1], V_DIM, 1))
    o_ref[0] = o.astype(BF16)


def _mla_attn(q, k, v, tq, tk, causal):
    B, T, _ = q.shape
    S = k.shape[1]
    nq, nk = T // tq, S // tk
    assert T % tq == 0 and S % tk == 0 and (not causal or (tq == tk and T == S))
    kv_spec = pl.BlockSpec((1, S, 2 * HEAD_PAD), lambda b, hp, i: (b, 0, hp))
    return pl.pallas_call(
        functools.partial(_mla_attn_kernel, tq=tq, tk=tk, causal=causal, nk=nk),
        grid=(B, A_HEADS // 2, nq),
        in_specs=[pl.BlockSpec((1, tq, 2 * HEAD_PAD), lambda b, hp, i: (b, i, hp)), kv_spec, kv_spec],
        out_specs=pl.BlockSpec((1, tq, 2 * V_DIM), lambda b, hp, i: (b, i, hp)),
        out_shape=jax.ShapeDtypeStruct((B, T, A_HEADS * V_DIM), BF16),
        scratch_shapes=[pltpu.VMEM((2, tq, LANES), F32), pltpu.VMEM((2, tq, LANES), F32)],
        compiler_params=_cparams(("parallel", "parallel", "arbitrary")),
        name="mla_attn",
    )(q, k, v)


def _odd_even_merge_sort_pairs(n):
    pairs = []

    def merge(lo, m, r):
        step = r * 2
        if step < m:
            merge(lo, m, step)
            merge(lo + r, m, step)
            pairs.extend((i, i + r) for i in range(lo + r, lo + m - r, step))
        else:
            pairs.append((lo, lo + r))

    def sort(lo, m):
        if m > 1:
            sort(lo, m // 2)
            sort(lo + m // 2, m // 2)
            merge(lo, m, 1)

    sort(0, n)
    return pairs


_SORT16 = _odd_even_merge_sort_pairs(P_TOPK)
_BITONIC16 = [(i, i + d) for d in (8, 4, 2, 1) for i in range(P_TOPK) if not i & d]
SUBLANES = 8


def _top16_rows(chains):
    assert all(s.shape[0] == P_TOPK * SUBLANES for s, _, _ in chains)
    vs = [[s[SUBLANES * j:SUBLANES * (j + 1), :] for j in range(P_TOPK)] for s, _, _ in chains]

    def compare_exchange(pairs):
        for i, j in pairs:
            for v in vs:
                v[i], v[j] = jnp.maximum(v[i], v[j]), jnp.minimum(v[i], v[j])

    compare_exchange(_SORT16)
    for shift in (4, 2, 1):
        for v in vs:
            other = [pltpu.roll(x, shift, 0) for x in v]
            v[:] = [jnp.maximum(v[k], other[P_TOPK - 1 - k]) for k in range(P_TOPK)]
        compare_exchange(_BITONIC16)
    for v, (_, t_sc, cols) in zip(vs, chains):
        for k in range(P_TOPK):
            t_sc[k:k + 1, cols] = v[k][0:1, :]


def _route_tiles(tiles, t1_sc, t2_sc):
    _top16_rows([(s1, t1_sc, cols) for s1, _, cols in tiles] + [(s2, t2_sc, cols) for _, s2, cols in tiles])
    state = []
    for s1, s2, cols in tiles:
        row8 = lax.broadcasted_iota(jnp.int32, (8, s1.shape[1]), 0)
        t1 = [t1_sc[k:k + 1, cols] for k in range(P_TOPK)]
        t2 = [t2_sc[k:k + 1, cols] for k in range(P_TOPK)]
        t2_lo = t2_sc[0:8, cols]
        t2_hi = t2_sc[8:16, cols]
        t1_hi = t1_sc[8:16, cols]
        cands = [t1[0] + t2_lo, t1[0] + t2_hi, t1_hi + t2[0]]
        for k1 in range(1, 8):
            lim = P_TOPK // (k1 + 1)
            c = t1[k1] + t2_lo
            cands.append(c if lim >= 8 else jnp.where(row8 < lim, c, -jnp.inf))
        top = t1[0] + t2[0]
        state.append(dict(t1=t1, t2=t2, cands=cands, top=top, z=jnp.zeros_like(top), tau=top))
    for k in range(P_TOPK):
        for st in state:
            m = st["cands"][0]
            for c in st["cands"][1:]:
                m = jnp.maximum(m, c)
            m = jnp.max(m, axis=0, keepdims=True)
            st["z"] = st["z"] + jnp.exp(m - st["top"])
            st["tau"] = m
            if k + 1 < P_TOPK:
                st["cands"] = [jnp.where(c == m, -jnp.inf, c) for c in st["cands"]]
    outs = []
    for (s1, s2, cols), st in zip(tiles, state):
        cnt = jnp.zeros_like(s1)
        rank = jnp.zeros_like(s2)
        for k in range(P_TOPK):
            cnt = jnp.where(s1 + st["t2"][k] >= st["tau"], float(k + 1), cnt)
            rank = jnp.where(st["t2"][k] > s2, float(k + 1), rank)
        n1 = jnp.sum(jnp.where(s1 >= st["t1"][P_TOPK - 1], 1.0, 0.0), axis=0, keepdims=True)
        n2 = jnp.sum(jnp.where(s2 >= st["t2"][P_TOPK - 1], 1.0, 0.0), axis=0, keepdims=True)
        npair = jnp.sum(cnt, axis=0, keepdims=True)
        k = float(P_TOPK)
        tied = jnp.where((n1 != k) | (n2 != k) | (npair != k), 1.0, 0.0)
        outs.append((rank, cnt, jnp.exp(s2 - st["t2"][0]), jnp.exp(s1 - st["t1"][0]) / st["z"], tied))
    return outs


def _route_tile_exact(s1, s2, t1_sc, t2_sc, cols):
    nk, w = s1.shape
    key = lax.broadcasted_iota(jnp.int32, (nk, w), 0).astype(F32)

    def top16(s, t_sc):
        work, rank, tops = s, jnp.full(s.shape, float(P_TOPK), F32), []
        for k in range(P_TOPK):
            m = jnp.max(work, axis=0, keepdims=True)
            first = jnp.min(jnp.where(work == m, key, float(nk)), axis=0, keepdims=True)
            hit = key == first
            rank = jnp.where(hit, float(k), rank)
            work = jnp.where(hit, -jnp.inf, work)
            t_sc[k:k + 1, cols] = m
            tops.append(m)
        return tops, rank

    t1, rank1 = top16(s1, t1_sc)
    t2, rank2 = top16(s2, t2_sc)
    row8 = lax.broadcasted_iota(jnp.int32, (8, w), 0).astype(F32)
    t2_lo = t2_sc[0:8, cols]
    t2_hi = t2_sc[8:16, cols]
    t1_hi = t1_sc[8:16, cols]
    cands = [(t1[0] + t2_lo, row8), (t1[0] + t2_hi, row8 + 8.0), (t1_hi + t2[0], (row8 + 8.0) * P_TOPK)]
    for k1 in range(1, 8):
        lim = P_TOPK // (k1 + 1)
        c = t1[k1] + t2_lo
        cands.append((c if lim >= 8 else jnp.where(row8 < lim, c, -jnp.inf), row8 + float(k1 * P_TOPK)))
    top = t1[0] + t2[0]
    z = jnp.zeros_like(top)
    k1_row = lax.broadcasted_iota(jnp.int32, (P_TOPK, w), 0).astype(F32)
    cnt_k1 = jnp.zeros((P_TOPK, w), F32)
    for k in range(P_TOPK):
        m = cands[0][0]
        for c, _ in cands[1:]:
            m = jnp.maximum(m, c)
        m = jnp.max(m, axis=0, keepdims=True)
        first = None
        for c, pos in cands:
            f = jnp.min(jnp.where(c == m, pos, float(P_TOPK * P_TOPK)), axis=0, keepdims=True)
            first = f if first is None else jnp.minimum(first, f)
        z = z + jnp.exp(m - top)
        cnt_k1 = cnt_k1 + jnp.where(k1_row == jnp.floor(first * (1.0 / P_TOPK)), 1.0, 0.0)
        cands = [(jnp.where(pos == first, -jnp.inf, c), pos) for c, pos in cands]
    t1_sc[:, cols] = cnt_k1
    cnt = jnp.zeros_like(s1)
    for k1 in range(P_TOPK):
        cnt = jnp.where(rank1 == float(k1), t1_sc[k1:k1 + 1, cols], cnt)
    return rank2, cnt, jnp.exp(s1 - t1[0]) / z


def _post_mix_kernel(o_ref, x_ref, g_ref, sh_ref, sc_ref, nw_ref, wo_ref, wq_ref, sk1_ref, sk2_ref,
                     x1_ref, h2_ref, rank_ref, cnt_ref, e2_ref, r_ref, t1_sc, t2_sc, s_sc, tied_sc):
    mix = jnp.dot(o_ref[0], wo_ref[...], preferred_element_type=F32)
    x1 = x_ref[0] + g_ref[0] * mix
    x1_ref[0] = x1
    h2 = _modulate(x1, nw_ref[...], sh_ref[0], sc_ref[0]).astype(BF16)
    h2_ref[0] = h2
    tb = h2.shape[0]
    q = jnp.dot(h2, wq_ref[...], preferred_element_type=F32).astype(BF16)
    for h in range(P_HEADS):
        qh = q[:, h * 2 * P_HALF:(h + 1) * 2 * P_HALF]
        s_sc[h, 0] = lax.dot_general(sk1_ref[h], qh, _NT, preferred_element_type=F32)
        s_sc[h, 1] = lax.dot_general(sk2_ref[h], qh, _NT, preferred_element_type=F32)

    def head_body(h, carry):
        s1 = s_sc[h, 0]
        s2 = s_sc[h, 1]
        lane_tiles = [slice(nt * LANES, (nt + 1) * LANES) for nt in range(tb // LANES)]
        routed = _route_tiles([(s1[:, cols], s2[:, cols], cols) for cols in lane_tiles], t1_sc, t2_sc)
        for cols, (rank, cnt, e2, r, tied) in zip(lane_tiles, routed):
            rank_ref[h, :, cols] = rank.astype(BF16)
            cnt_ref[h, :, cols] = cnt
            e2_ref[h, :, cols] = e2.astype(BF16)
            r_ref[h, :, cols] = r
            tied_sc[h, :, cols] = jnp.broadcast_to(tied, (8, tied.shape[1]))
        return carry

    lax.fori_loop(0, P_HEADS, head_body, 0)

    @pl.when(jnp.max(tied_sc[...]) > 0.0)
    def _():
        def fix_head(h, carry):
            for nt in range(tb // LANES):
                cols = slice(nt * LANES, (nt + 1) * LANES)

                @pl.when(jnp.max(tied_sc[h, :, cols]) > 0.0)
                def _(cols=cols):
                    rank_x, cnt_x, r_x = _route_tile_exact(s_sc[h, 0, :, cols], s_sc[h, 1, :, cols],
                                                           t1_sc, t2_sc, cols)
                    rank_ref[h, :, cols] = rank_x.astype(BF16)
                    cnt_ref[h, :, cols] = cnt_x
                    r_ref[h, :, cols] = r_x
            return carry

        lax.fori_loop(0, P_HEADS, fix_head, 0)


def _post_mix(o, x, gate, shift, scale, nw, w_o, wq_heads, sk1, sk2, tb):
    B, T, D = x.shape
    nT = T // tb
    n = B * T
    tok = lambda b, i: (b, i, 0)
    rt_spec = pl.BlockSpec((P_HEADS, N_KEYS, tb), lambda b, i: (0, 0, b * nT + i))
    rt_shape = jax.ShapeDtypeStruct((P_HEADS, N_KEYS, n), F32)
    rt_shape_b = jax.ShapeDtypeStruct((P_HEADS, N_KEYS, n), BF16)
    return pl.pallas_call(
        _post_mix_kernel,
        grid=(B, nT),
        in_specs=[pl.BlockSpec((1, tb, o.shape[2]), tok), pl.BlockSpec((1, tb, D), tok),
                  _mod_spec(gate, tb), _mod_spec(shift, tb), _mod_spec(scale, tb),
                  _full_spec(nw), _full_spec(w_o), _full_spec(wq_heads), _full_spec(sk1), _full_spec(sk2)],
        out_specs=[pl.BlockSpec((1, tb, D), tok), pl.BlockSpec((1, tb, D), tok),
                   rt_spec, rt_spec, rt_spec, rt_spec],
        out_shape=[jax.ShapeDtypeStruct((B, T, D), F32), jax.ShapeDtypeStruct((B, T, D), BF16),
                   rt_shape_b, rt_shape, rt_shape_b, rt_shape],
        scratch_shapes=[pltpu.VMEM((P_TOPK, tb), F32), pltpu.VMEM((P_TOPK, tb), F32),
                        pltpu.VMEM((P_HEADS, 2, N_KEYS, tb), F32), pltpu.VMEM((P_HEADS, 8, tb), F32)],
        compiler_params=_cparams(("parallel", "parallel")),
        name="post_mix_route",
    )(o, x, gate, shift, scale, nw, w_o, wq_heads, sk1, sk2)


def _gelu_tanh(a):
    c = math.sqrt(2.0 / math.pi)
    return 0.5 * a * (1.0 + jnp.tanh(c * (a + 0.044715 * (a * a * a))))


def _peer_dense_kernel(h_ref, u_ref, vt_ref, rank_ref, cnt_ref, e2_ref, r_ref, x_ref, g_ref,
                       o_ref, acc_ref, a0_ref, a1_ref, hc0_ref, hc1_ref, *, ec, ne):
    g = pl.program_id(0)
    per = ec // N_KEYS
    s = lax.rem(jnp.maximum(g - 1, 0), ne) + 1
    c3 = lax.rem(jnp.maximum(g - 2, 0), ne)

    @pl.when(g == 0)
    def _():
        for ref in (a0_ref, a1_ref, hc0_ref, hc1_ref):
            ref[...] = jnp.zeros(ref.shape, ref.dtype)

    @pl.when(c3 == 0)
    def _():
        acc_ref[...] = jnp.zeros(acc_ref.shape, F32)

    def stages(a_out, a_in, hc_out, hc_in):
        zero = jnp.zeros((), BF16)
        tb = h_ref.shape[1]
        nsplit = PEER_SPLIT
        gate_w = min(GATE_LANES, tb)
        mxu_w = min(MXU_N, tb)
        assert tb % gate_w == 0 and tb % mxu_w == 0

        row_cache = {}

        def routed_rows(ii):
            if ii not in row_cache:
                i1 = jnp.clip((s - 1) * per + ii, 0, N_KEYS - 1)
                row_cache[ii] = [(cnt_ref[h, pl.ds(i1, 1), :].astype(BF16),
                                  r_ref[h, pl.ds(i1, 1), :].astype(BF16)) for h in range(P_HEADS)]
            return row_cache[ii]

        def gate_piece(ii, n):
            cols = slice(n * gate_w, (n + 1) * gate_w)
            rows = slice(ii * N_KEYS, (ii + 1) * N_KEYS)
            w = None
            for h, (cnt_row, r_row) in enumerate(routed_rows(ii)):
                contrib = jnp.where(rank_ref[h, :, cols] < cnt_row[:, cols], e2_ref[h, :, cols], zero) * r_row[:, cols]
                w = contrib if w is None else w + contrib
            hc_out[rows, cols] = _gelu_tanh(a_in[rows, cols]) * w

        def score_piece(q, n):
            cols = slice(n * mxu_w, (n + 1) * mxu_w)
            rows = slice(q * (ec // nsplit), (q + 1) * (ec // nsplit))
            a_out[rows, cols] = lax.dot_general(u_ref[rows, :], h_ref[0, cols, :], _NT,
                                                preferred_element_type=F32).astype(BF16)

        def out_piece(q, n):
            cols = slice(n * mxu_w, (n + 1) * mxu_w)
            rows = slice(q * (D_MODEL // nsplit), (q + 1) * (D_MODEL // nsplit))
            acc_ref[rows, cols] += jnp.dot(vt_ref[0, rows, :], hc_in[:, cols], preferred_element_type=F32)

        vec = [functools.partial(gate_piece, ii, n) for n in range(tb // gate_w) for ii in range(per)]
        mxu = []
        for n in range(tb // mxu_w):
            for q in range(nsplit):
                mxu.append(functools.partial(score_piece, q, n))
                mxu.append(functools.partial(out_piece, q, n))
        for k in range(max(len(vec), len(mxu))):
            if k < len(mxu):
                mxu[k]()
            if k < len(vec):
                vec[k]()

    even = lax.rem(g, 2) == 0
    pl.when(even)(lambda: stages(a0_ref, a1_ref, hc1_ref, hc0_ref))
    pl.when(jnp.logical_not(even))(lambda: stages(a1_ref, a0_ref, hc0_ref, hc1_ref))

    @pl.when((c3 == ne - 1) & (g >= 2))
    def _():
        o_ref[0] = x_ref[0] + g_ref[0] * acc_ref[...].T


def _peer_dense(h2, u, vt, rank, cnt, e2, r, x, gate, tb, ec=PEER_EC):
    B, T, D = x.shape
    nT = T // tb
    nblk = B * nT
    ne = N_EXPERTS // ec
    assert vt.shape == (ne, D, ec)
    blk1 = lambda g: jnp.minimum(g // ne, nblk - 1)
    blk2 = lambda g: jnp.minimum(jnp.maximum(g - 1, 0) // ne, nblk - 1)
    blk3 = lambda g: jnp.maximum(g - 2, 0) // ne
    tok1 = lambda g: (blk1(g) // nT, blk1(g) % nT, 0)
    tok3 = lambda g: (blk3(g) // nT, blk3(g) % nT, 0)
    rt_spec = pl.BlockSpec((P_HEADS, N_KEYS, tb), lambda g: (0, 0, blk2(g)))
    if gate.shape[1] == 1:
        gate_spec = pl.BlockSpec((1, 1, D), lambda g: (blk3(g) // nT, 0, 0))
    else:
        gate_spec = pl.BlockSpec((1, tb, D), tok3)
    return pl.pallas_call(
        functools.partial(_peer_dense_kernel, ec=ec, ne=ne),
        grid=(nblk * ne + 2,),
        in_specs=[pl.BlockSpec((1, tb, D), tok1),
                  pl.BlockSpec((ec, D), lambda g: (g % ne, 0)),
                  pl.BlockSpec((1, D, ec), lambda g: (jnp.maximum(g - 2, 0) % ne, 0, 0)),
                  rt_spec, rt_spec, rt_spec, rt_spec,
                  pl.BlockSpec((1, tb, D), tok3), gate_spec],
        out_specs=pl.BlockSpec((1, tb, D), tok3),
        out_shape=jax.ShapeDtypeStruct((B, T, D), F32),
        scratch_shapes=[pltpu.VMEM((D, tb), F32),
                        pltpu.VMEM((ec, tb), BF16), pltpu.VMEM((ec, tb), BF16),
                        pltpu.VMEM((ec, tb), BF16), pltpu.VMEM((ec, tb), BF16)],
        compiler_params=_cparams(("arbitrary",)),
        name="peer_dense",
    )(h2, u, vt, rank, cnt, e2, r, x, gate)


def _pair_rms(y, w, lane):
    y2 = y * y
    lo = jnp.sum(jnp.where(lane < B_HEAD_DIM, y2, 0.0), axis=-1, keepdims=True)
    hi = jnp.sum(y2, axis=-1, keepdims=True) - lo
    ms = jnp.where(lane < B_HEAD_DIM, lo, hi) * (1.0 / B_HEAD_DIM)
    return y * lax.rsqrt(ms + EPS) * w


def _shared_kv_kernel(x_ref, sh_ref, sc_ref, nw_ref, wk_ref, wv_ref, kn_ref, k_ref, v_ref, kb_ref, vb_ref,
                      *, first_tail):
    h = _modulate(x_ref[0], nw_ref[...], sh_ref[0], sc_ref[0]).astype(BF16)
    kraw = jnp.dot(h, wk_ref[...], preferred_element_type=F32)
    v = jnp.dot(h, wv_ref[...], preferred_element_type=F32)
    vb_ref[0] = v.astype(BF16)
    knw = kn_ref[...]
    lane = lax.broadcasted_iota(jnp.int32, (h.shape[0], LANES), 1)
    ks = [_pair_rms(kraw[:, hp * LANES:(hp + 1) * LANES], knw, lane) for hp in range(B_HEADS // 2)]
    for hp, kh in enumerate(ks):
        kb_ref[0, :, hp * LANES:(hp + 1) * LANES] = kh.astype(BF16)

    @pl.when(pl.program_id(1) >= first_tail)
    def _():
        v_ref[0] = v
        for hp, kh in enumerate(ks):
            k_ref[0, :, hp * LANES:(hp + 1) * LANES] = kh


def _shared_kv(x, shift, scale, nw, wk, wv, kn_pair, tb, tail):
    B, T, D = x.shape
    nT = T // tb
    assert tail % tb == 0 and tail <= T
    first_tail = nT - tail // tb
    tok = lambda b, i: (b, i, 0)
    blk = pl.BlockSpec((1, tb, D), tok)
    tail_blk = pl.BlockSpec((1, tb, D), lambda b, i: (b, jnp.maximum(i - first_tail, 0), 0))
    return pl.pallas_call(
        functools.partial(_shared_kv_kernel, first_tail=first_tail),
        grid=(B, nT),
        in_specs=[blk, _mod_spec(shift, tb), _mod_spec(scale, tb), _full_spec(nw),
                  _full_spec(wk), _full_spec(wv), _full_spec(kn_pair)],
        out_specs=[tail_blk, tail_blk, blk, blk],
        out_shape=[jax.ShapeDtypeStruct((B, tail, D), F32), jax.ShapeDtypeStruct((B, tail, D), F32),
                   jax.ShapeDtypeStruct((B, T, D), BF16), jax.ShapeDtypeStruct((B, T, D), BF16)],
        compiler_params=_cparams(("parallel", "arbitrary")),
        name="shared_kv",
    )(x, shift, scale, nw, wk, wv, kn_pair)


def _band_pre_kernel(x_ref, sh_ref, sc_ref, nw_ref, wq_ref, qn_ref, q_ref):
    h = _modulate(x_ref[0], nw_ref[...], sh_ref[0], sc_ref[0]).astype(BF16)
    q = jnp.dot(h, wq_ref[...], preferred_element_type=F32)
    qnw = qn_ref[...]
    lane = lax.broadcasted_iota(jnp.int32, (h.shape[0], LANES), 1)
    scale = B_HEAD_DIM ** -0.5
    for hp in range(B_HEADS // 2):
        qh = _pair_rms(q[:, hp * LANES:(hp + 1) * LANES], qnw, lane)
        q_ref[0, :, hp * LANES:(hp + 1) * LANES] = (qh * scale).astype(BF16)


def _band_pre(x, shift, scale, nw, wq, qn_pair, tb):
    B, T, D = x.shape
    tok = lambda b, i: (b, i, 0)
    blk = pl.BlockSpec((1, tb, D), tok)
    return pl.pallas_call(
        _band_pre_kernel,
        grid=(B, T // tb),
        in_specs=[blk, _mod_spec(shift, tb), _mod_spec(scale, tb), _full_spec(nw),
                  _full_spec(wq), _full_spec(qn_pair)],
        out_specs=blk,
        out_shape=jax.ShapeDtypeStruct((B, T, D), BF16),
        compiler_params=_cparams(("parallel", "parallel")),
        name="band_pre",
    )(x, shift, scale, nw, wq, qn_pair)


def _band_bias_kernel(tab_ref, o_ref):
    h = pl.program_id(0)
    nvar = 2 * LANES
    r = lax.broadcasted_iota(jnp.int32, (BAND_SUB, nvar), 0)
    w = lax.broadcasted_iota(jnp.int32, (BAND_SUB, nvar), 1) + (BAND_WIN - nvar)
    idx = jnp.clip(r + B_WINDOW - w, -REL_CLIP, REL_CLIP) + REL_CLIP
    far = tab_ref[h, 2 * REL_CLIP]

    def body(t, acc):
        return jnp.where(idx == t, tab_ref[h, t], acc)

    var = lax.fori_loop(0, 2 * REL_CLIP, body, jnp.full((BAND_SUB, nvar), far, F32))
    full = jnp.concatenate([jnp.full((BAND_SUB, BAND_WIN - nvar), far, F32), var], axis=1)
    rr = lax.broadcasted_iota(jnp.int32, (BAND_SUB, BAND_WIN), 0)
    ww = lax.broadcasted_iota(jnp.int32, (BAND_SUB, BAND_WIN), 1)
    qc = rr // CHUNK + LEFT_CHUNKS
    kc = ww // CHUNK
    allowed = (kc <= qc) & (kc >= qc - LEFT_CHUNKS)
    o_ref[0] = jnp.where(allowed, full, NEG)


def _band_bias(table):
    nh = table.shape[0]
    return pl.pallas_call(
        _band_bias_kernel,
        grid=(nh,),
        in_specs=[pl.BlockSpec(memory_space=pltpu.SMEM)],
        out_specs=pl.BlockSpec((1, BAND_SUB, BAND_WIN), lambda h: (h, 0, 0)),
        out_shape=jax.ShapeDtypeStruct((nh, BAND_SUB, BAND_WIN), F32),
        compiler_params=_cparams(("arbitrary",)),
        name="band_bias",
    )(table)


def _band_windows(windows, bias_ref, lane):
    chains = [(w, hh) for w in range(len(windows)) for hh in range(2)]
    scores = []
    for w, hh in chains:
        qs, kw, _ = windows[w]
        nkw = kw.shape[0]
        sel = (lane < B_HEAD_DIM) if hh == 0 else (lane >= B_HEAD_DIM)
        qh = jnp.where(sel, qs, jnp.zeros_like(qs))
        s = lax.dot_general(qh, kw, _NT, preferred_element_type=F32)
        scores.append(s + bias_ref[hh, :, BAND_WIN - nkw:BAND_WIN])
    maxes = [jnp.max(s, axis=-1, keepdims=True) for s in scores]
    probs = [jnp.exp((s - m).astype(BF16)) for s, m in zip(scores, maxes)]
    vext = []
    for _, _, vw in windows:
        ones_lane = (lax.broadcasted_iota(jnp.int32, vw.shape, 1) == 0).astype(BF16)
        vext.append(jnp.concatenate([vw, ones_lane], axis=1))
    outs = []
    for (w, hh), p in zip(chains, probs):
        pv = jnp.dot(p, vext[w], preferred_element_type=F32)
        outs.append(pv[:, :LANES] / pv[:, LANES:LANES + 1])
    return [jnp.where(lane < B_HEAD_DIM, outs[2 * w], outs[2 * w + 1]) for w in range(len(windows))]


def _band_attn_kernel(q_ref, kp_ref, kc_ref, vp_ref, vc_ref, bias_ref, o_ref, *, tq):
    i = pl.program_id(2)
    lane = lax.broadcasted_iota(jnp.int32, (BAND_SUB, LANES), 1)
    nsub = tq // BAND_SUB

    def run(first):
        windows = []
        for c in range(nsub):
            qs = q_ref[0, c * BAND_SUB:(c + 1) * BAND_SUB, :]
            hi = (c + 1) * BAND_SUB
            if first or hi >= BAND_WIN:
                lo = max(hi - BAND_WIN, 0)
                kw = kc_ref[0, lo:hi, :]
                vw = vc_ref[0, lo:hi, :]
            else:
                lo = tq - (BAND_WIN - hi)
                kw = jnp.concatenate([kp_ref[0, lo:tq, :], kc_ref[0, 0:hi, :]], axis=0)
                vw = jnp.concatenate([vp_ref[0, lo:tq, :], vc_ref[0, 0:hi, :]], axis=0)
            windows.append((qs, kw, vw))
        for c, o in enumerate(_band_windows(windows, bias_ref, lane)):
            o_ref[0, c * BAND_SUB:(c + 1) * BAND_SUB, :] = o.astype(BF16)

    pl.when(i == 0)(lambda: run(True))
    pl.when(i > 0)(lambda: run(False))


def _band_attn(q, k, v, bias, tq):
    B, T, D = q.shape
    assert tq >= B_WINDOW and T % tq == 0
    cur = lambda b, hp, i: (b, i, hp)
    prev = lambda b, hp, i: (b, jnp.maximum(i - 1, 0), hp)
    blk = lambda m: pl.BlockSpec((1, tq, LANES), m)
    return pl.pallas_call(
        functools.partial(_band_attn_kernel, tq=tq),
        grid=(B, B_HEADS // 2, T // tq),
        in_specs=[blk(cur), blk(prev), blk(cur), blk(prev), blk(cur),
                  pl.BlockSpec((2, BAND_SUB, BAND_WIN), lambda b, hp, i: (hp, 0, 0))],
        out_specs=blk(cur),
        out_shape=jax.ShapeDtypeStruct((B, T, D), BF16),
        compiler_params=_cparams(("parallel", "parallel", "arbitrary")),
        name="band_attn",
    )(q, k, k, v, v, bias)


def _band_step_kernel(q_ref, k_ref, v_ref, bias_ref, o_ref):
    lane = lax.broadcasted_iota(jnp.int32, (BAND_SUB, LANES), 1)
    o, = _band_windows([(q_ref[0], k_ref[0], v_ref[0])], bias_ref, lane)
    o_ref[0] = o.astype(BF16)


def _band_step(q, kwin, vwin, bias):
    B, _, D = q.shape
    return pl.pallas_call(
        _band_step_kernel,
        grid=(B, B_HEADS // 2),
        in_specs=[pl.BlockSpec((1, BAND_SUB, LANES), lambda b, hp: (b, 0, hp)),
                  pl.BlockSpec((1, BAND_WIN, LANES), lambda b, hp: (b, 0, hp)),
                  pl.BlockSpec((1, BAND_WIN, LANES), lambda b, hp: (b, 0, hp)),
                  pl.BlockSpec((2, BAND_SUB, BAND_WIN), lambda b, hp: (hp, 0, 0))],
        out_specs=pl.BlockSpec((1, BAND_SUB, LANES), lambda b, hp: (b, 0, hp)),
        out_shape=jax.ShapeDtypeStruct((B, BAND_SUB, D), BF16),
        compiler_params=_cparams(("parallel", "parallel")),
        name="band_step",
    )(q, kwin, vwin, bias)


def _rope_tables(pos):
    half = ROPE_DIM // 2
    freqs = ROPE_THETA ** (-jnp.arange(half, dtype=F32) / half)
    ang = pos.astype(F32)[:, None] * freqs[None, :]
    c, s = jnp.cos(ang), jnp.sin(ang)
    n = pos.shape[0]
    ones = jnp.ones((n, NOPE_DIM), F32)
    zeros = jnp.zeros((n, NOPE_DIM), F32)
    pad1 = jnp.ones((n, HEAD_PAD - QK_DIM), F32)
    pad0 = jnp.zeros((n, HEAD_PAD - QK_DIM), F32)
    return (jnp.concatenate([ones, c, c, pad1], axis=1),
            jnp.concatenate([zeros, -s, s, pad0], axis=1))


def _pad_heads(w, nheads, width):
    lead = w.shape[:-1]
    w = w.reshape(lead + (nheads, width))
    w = jnp.pad(w, [(0, 0)] * len(lead) + [(0, 0), (0, HEAD_PAD - width)])
    return w.reshape(lead + (nheads * HEAD_PAD,))


def _block(n, pref):
    for t in pref:
        if n % t == 0:
            return t
    return n


def kernel(x_prompt, x_sample, c_prompt, c_sample, cache_a_ckv, cache_a_krope, cache_b_k, cache_b_v, ada_w, ada_b, norm_mix_w, norm_ffn_w, a_w_in, a_q_lora_norm, a_kv_lora_norm, a_w_uq, a_w_ukv, a_q_norm, a_k_norm, a_w_o, kv_ada_w, kv_ada_b, kv_norm_w, b_w_kv, b_k_norm, b_w_q, b_q_norm, b_rel_bias, b_w_o, p_w_q, p_subkeys, p_u, p_v):
    D = D_MODEL
    Bp, Tp, _ = x_prompt.shape
    Bs, Ts, _ = x_sample.shape
    P = cache_a_ckv.shape[2]
    Pb = cache_b_k.shape[1]
    assert Ts == CHUNK and Pb == B_WINDOW and P % CHUNK == 0

    ada_all_w = jnp.concatenate([ada_w[0], ada_w[1], kv_ada_w], axis=1).astype(BF16)
    ada_all_b = jnp.concatenate([ada_b[0], ada_b[1], kv_ada_b])[None, :]
    c_all = jnp.concatenate([c_prompt, c_sample], axis=0)
    mod = _ada(c_all, ada_all_w, ada_all_b)

    w_in = a_w_in[0].astype(BF16)
    qln = a_q_lora_norm[0][None, :]
    kvln = a_kv_lora_norm[0][None, :]
    wuq_pad = _pad_heads(a_w_uq[0], A_HEADS, QK_DIM).astype(BF16)
    wukv = a_w_ukv[0].reshape(KV_LORA, A_HEADS, NOPE_DIM + V_DIM)
    wuk_pad = jnp.pad(wukv[:, :, :NOPE_DIM], ((0, 0), (0, 0), (0, HEAD_PAD - NOPE_DIM))
                      ).reshape(KV_LORA, A_HEADS * HEAD_PAD).astype(BF16)
    wuv_pad = jnp.pad(wukv[:, :, NOPE_DIM:], ((0, 0), (0, 0), (0, HEAD_PAD - V_DIM))
                      ).reshape(KV_LORA, A_HEADS * HEAD_PAD).astype(BF16)
    vone = jnp.asarray(np.tile(np.eye(1, HEAD_PAD, k=V_DIM), (1, A_HEADS)), F32)
    qn_pad = jnp.pad(a_q_norm[0], (0, HEAD_PAD - QK_DIM))[None, :]
    kn_pad = jnp.pad(a_k_norm[0], (0, HEAD_PAD - QK_DIM))[None, :]
    place = jnp.asarray(np.eye(ROPE_DIM, HEAD_PAD, k=NOPE_DIM), BF16)
    a_wo = a_w_o[0].astype(BF16)
    wk_b = b_w_kv[:, :D].astype(BF16)
    wv_b = b_w_kv[:, D:].astype(BF16)
    bkn_pair = jnp.tile(b_k_norm, 2)[None, :]
    bqn_pair = jnp.tile(b_q_norm[0], 2)[None, :]
    bwq = b_w_q[0].astype(BF16)
    bwo = b_w_o[0].astype(BF16)
    bias_tile = _band_bias(b_rel_bias[0])

    def peer_weights(layer):
        wq = p_w_q[layer].astype(BF16)
        sk = p_subkeys[layer]
        sk1 = jnp.pad(sk[:, 0], ((0, 0), (0, 0), (0, P_HALF))).astype(BF16)
        sk2 = jnp.pad(sk[:, 1], ((0, 0), (0, 0), (P_HALF, 0))).astype(BF16)
        vt = p_v[layer].astype(BF16).reshape(N_EXPERTS // PEER_EC, PEER_EC, D).transpose(0, 2, 1)
        return wq, sk1, sk2, p_u[layer].astype(BF16), vt

    peer_w = [peer_weights(0), peer_weights(1)]
    norm_mix = norm_mix_w[:, None, :]
    norm_ffn = norm_ffn_w[:, None, :]
    kv_nw = kv_norm_w[None, :]

    def run(x, modp, per_token, pos_q, past):
        B, T, _ = x.shape
        if per_token:
            modv = jnp.repeat(modp, T, axis=0)[None]
            xw = x.reshape(1, B * T, D)
        else:
            modv = modp[:, None, :]
            xw = x
        Bw, Tw, _ = xw.shape
        sl = lambda k: modv[:, :, k * D:(k + 1) * D]
        tb = _block(Tw, (512, 256, 128))
        tbr = _block(Tw, (256, 128))
        tbe = _block(Tw, (512, 256, 128))

        cos_q, sin_q = _rope_tables(pos_q)
        ckv, kr, q = _mla_pre(xw, sl(0), sl(1), norm_mix[0], w_in, qln, kvln, wuq_pad, qn_pad,
                              cos_q, sin_q, tb)
        ckv = ckv.reshape(B, T, KV_LORA)
        kr = kr.reshape(B, T, ROPE_DIM)
        q = q.reshape(B, T, A_HEADS * HEAD_PAD)
        if past is None:
            ckv_all, kr_all = ckv, kr
            pos_k = pos_q[:T]
        else:
            ckv_all = jnp.concatenate([past[0], ckv], axis=1)
            kr_all = jnp.concatenate([past[1], kr], axis=1)
            pos_k = jnp.arange(P + T, dtype=jnp.int32)
        S = ckv_all.shape[1]
        cos_k, sin_k = _rope_tables(pos_k)
        tbk = _block(S, (512, 704, 256, 192, 64))
        k, v = _mla_kv(ckv_all, kr_all, wuk_pad, wuv_pad, vone, kn_pad, place, cos_k, sin_k, tbk)
        if past is None:
            ta = _block(T, (512, 256, 128, 64))
            o = _mla_attn(q, k, v, ta, ta, True)
        else:
            o = _mla_attn(q, k, v, T, tbk, False)
        o = o.reshape(Bw, Tw, A_HEADS * V_DIM)

        wq, sk1, sk2, u, vt = peer_w[0]
        x1, h2, rank, cnt, e2, r = _post_mix(o, xw, sl(2), sl(3), sl(4), norm_ffn[0], a_wo, wq, sk1, sk2, tbr)
        x2 = _peer_dense(h2, u, vt, rank, cnt, e2, r, x1, sl(5), tbe)

        tail = B_WINDOW if past is None else Tw
        kf, vf, kb, vb = _shared_kv(x2, sl(12), sl(13), kv_nw, wk_b, wv_b, bkn_pair, tb, tail)

        qb = _band_pre(x2, sl(6), sl(7), norm_mix[1], bwq, bqn_pair, tb)
        if past is None:
            ob = _band_attn(qb, kb, vb, bias_tile, B_WINDOW)
            new_bk = kf.reshape(B, B_WINDOW, B_HEADS, B_HEAD_DIM)
            new_bv = vf.reshape(B, B_WINDOW, B_HEADS, B_HEAD_DIM)
        else:
            zq = jnp.zeros((B, CHUNK, D), BF16)
            qpad = jnp.concatenate([zq, qb.reshape(B, T, D)], axis=1)
            kwin = jnp.concatenate([zq, past[2].reshape(B, Pb, D).astype(BF16), kb.reshape(B, T, D)], axis=1)
            vwin = jnp.concatenate([zq, past[3].reshape(B, Pb, D).astype(BF16), vb.reshape(B, T, D)], axis=1)
            ob = _band_step(qpad, kwin, vwin, bias_tile)[:, CHUNK:].reshape(Bw, Tw, D)
            new_bk = jnp.concatenate([past[2], kf.reshape(B, T, B_HEADS, B_HEAD_DIM)], axis=1)[:, -Pb:]
            new_bv = jnp.concatenate([past[3], vf.reshape(B, T, B_HEADS, B_HEAD_DIM)], axis=1)[:, -Pb:]

        wq, sk1, sk2, u, vt = peer_w[1]
        x3, h4, rank, cnt, e2, r = _post_mix(ob, x2, sl(8), sl(9), sl(10), norm_ffn[1], bwo, wq, sk1, sk2, tbr)
        y = _peer_dense(h4, u, vt, rank, cnt, e2, r, x3, sl(11), tbe)
        return y.reshape(B, T, D), ckv[None], kr[None], new_bk, new_bv

    pos_p = jnp.arange(Tp, dtype=jnp.int32)
    pos_s = jnp.tile(P + jnp.arange(Ts, dtype=jnp.int32), Bs)
    y_p, p_ckv, p_kr, p_bk, p_bv = run(x_prompt, mod[:Bp], False, pos_p, None)
    y_s, s_ckv, s_kr, s_bk, s_bv = run(x_sample, mod[Bp:], True, pos_s,
                                       (cache_a_ckv[0], cache_a_krope[0], cache_b_k, cache_b_v))
    return (y_p, y_s, p_ckv, p_kr, p_bk, p_bv, s_ckv, s_kr, s_bk, s_bv)
```

```python
import functools
import math

import numpy as np
import jax
import jax.numpy as jnp
from jax import lax
from jax.experimental import pallas as pl
from jax.experimental.pallas import tpu as pltpu

F32 = jnp.float32
BF16 = jnp.bfloat16

D_MODEL = 1024
CHUNK = 64
A_HEADS = 16
Q_LORA = 384
KV_LORA = 256
NOPE_DIM = 64
ROPE_DIM = 32
V_DIM = 64
QK_DIM = NOPE_DIM + ROPE_DIM
ROPE_THETA = 10000.0
B_HEADS = 16
B_HEAD_DIM = 64
LEFT_CHUNKS = 8
B_WINDOW = LEFT_CHUNKS * CHUNK
REL_CLIP = 128
P_HEADS = 8
N_KEYS = 128
N_EXPERTS = N_KEYS * N_KEYS
P_HALF = 64
P_TOPK = 16
NEG = -1e30
EPS = 1e-6

LANES = 128
BF16_ROWS = 16
HEAD_PAD = 128
BAND_SUB = 2 * CHUNK
BAND_WIN = B_WINDOW + BAND_SUB
VMEM_LIMIT = 56 * 1024 * 1024
PEER_EC = 512
PEER_SPLIT = 2
MXU_N = 256
GATE_LANES = 256

_NT = (((1,), (1,)), ((), ()))


def _cparams(sem):
    return pltpu.CompilerParams(dimension_semantics=sem, vmem_limit_bytes=VMEM_LIMIT)


def _modulate(x, w, shift, scale):
    ms = jnp.mean(x * x, axis=-1, keepdims=True)
    y = x * lax.rsqrt(ms + EPS) * w
    return y * (1.0 + scale) + shift


def _rms(x, w):
    ms = jnp.mean(x * x, axis=-1, keepdims=True)
    return x * lax.rsqrt(ms + EPS) * w


def _mod_spec(arr, tb):
    if arr.shape[1] == 1:
        return pl.BlockSpec((1, 1, arr.shape[2]), lambda b, i, *_: (b, 0, 0))
    return pl.BlockSpec((1, tb, arr.shape[2]), lambda b, i, *_: (b, i, 0))


def _full_spec(arr):
    nd = arr.ndim
    return pl.BlockSpec(arr.shape, lambda *_: (0,) * nd)


def _ada_kernel(c_ref, w_ref, b_ref, o_ref):
    c = c_ref[...]
    a = c / (1.0 + jnp.exp(-c))
    o_ref[...] = jnp.dot(a.astype(BF16), w_ref[...], preferred_element_type=F32) + b_ref[...]


def _ada(c, w, b, tn=2048):
    m, k = c.shape
    n = w.shape[1]
    return pl.pallas_call(
        _ada_kernel,
        grid=(n // tn,),
        in_specs=[pl.BlockSpec((m, k), lambda j: (0, 0)),
                  pl.BlockSpec((k, tn), lambda j: (0, j)),
                  pl.BlockSpec((1, tn), lambda j: (0, j))],
        out_specs=pl.BlockSpec((m, tn), lambda j: (0, j)),
        out_shape=jax.ShapeDtypeStruct((m, n), F32),
        compiler_params=_cparams(("arbitrary",)),
        name="ada_mod",
    )(c, w, b)


def _rope_head(xh, cos, sin, lane):
    rolled = jnp.where(lane < NOPE_DIM + ROPE_DIM // 2,
                       pltpu.roll(xh, LANES - ROPE_DIM // 2, 1),
                       pltpu.roll(xh, ROPE_DIM // 2, 1))
    return xh * cos + rolled * sin


def _mla_pre_kernel(x_ref, sh_ref, sc_ref, nw_ref, win_ref, qln_ref, kvln_ref, wuq_ref, qn_ref,
                    cos_ref, sin_ref, ckv_ref, kr_ref, q_ref):
    x = x_ref[0]
    h = _modulate(x, nw_ref[...], sh_ref[0], sc_ref[0])
    proj = jnp.dot(h.astype(BF16), win_ref[...], preferred_element_type=F32)
    c_q = _rms(proj[:, :Q_LORA], qln_ref[...])
    ckv_ref[0] = _rms(proj[:, Q_LORA:Q_LORA + KV_LORA], kvln_ref[...])
    kr_ref[0] = proj[:, Q_LORA + KV_LORA:]
    q = jnp.dot(c_q.astype(BF16), wuq_ref[...], preferred_element_type=F32)
    cos = cos_ref[...]
    sin = sin_ref[...]
    qn = qn_ref[...]
    lane = lax.broadcasted_iota(jnp.int32, cos.shape, 1)
    scale = QK_DIM ** -0.5
    for hd in range(A_HEADS):
        qh = q[:, hd * HEAD_PAD:(hd + 1) * HEAD_PAD]
        ss = jnp.sum(qh * qh, axis=-1, keepdims=True) * (1.0 / QK_DIM)
        qh = qh * lax.rsqrt(ss + EPS) * qn
        qh = _rope_head(qh, cos, sin, lane)
        q_ref[0, :, hd * HEAD_PAD:(hd + 1) * HEAD_PAD] = (qh * scale).astype(BF16)


def _mla_pre(x, shift, scale, nw, w_in, qln, kvln, wuq_pad, qn_pad, cos_t, sin_t, tb):
    B, T, D = x.shape
    nT = T // tb
    tab_spec = pl.BlockSpec((tb, LANES), lambda b, i: (i, 0))
    return pl.pallas_call(
        _mla_pre_kernel,
        grid=(B, nT),
        in_specs=[pl.BlockSpec((1, tb, D), lambda b, i: (b, i, 0)),
                  _mod_spec(shift, tb), _mod_spec(scale, tb),
                  _full_spec(nw), _full_spec(w_in), _full_spec(qln), _full_spec(kvln),
                  _full_spec(wuq_pad), _full_spec(qn_pad), tab_spec, tab_spec],
        out_specs=[pl.BlockSpec((1, tb, KV_LORA), lambda b, i: (b, i, 0)),
                   pl.BlockSpec((1, tb, ROPE_DIM), lambda b, i: (b, i, 0)),
                   pl.BlockSpec((1, tb, A_HEADS * HEAD_PAD), lambda b, i: (b, i, 0))],
        out_shape=[jax.ShapeDtypeStruct((B, T, KV_LORA), F32),
                   jax.ShapeDtypeStruct((B, T, ROPE_DIM), F32),
                   jax.ShapeDtypeStruct((B, T, A_HEADS * HEAD_PAD), BF16)],
        compiler_params=_cparams(("parallel", "parallel")),
        name="mla_pre",
    )(x, shift, scale, nw, w_in, qln, kvln, wuq_pad, qn_pad, cos_t, sin_t)


def _mla_kv_kernel(ckv_ref, kr_ref, wuk_ref, wuv_ref, vone_ref, kn_ref, place_ref, cos_ref, sin_ref,
                   k_ref, v_ref):
    ckv = ckv_ref[0].astype(BF16)
    kn = jnp.dot(ckv, wuk_ref[...], preferred_element_type=F32)
    v_ref[0] = (jnp.dot(ckv, wuv_ref[...], preferred_element_type=F32) + vone_ref[...]).astype(BF16)
    kr = kr_ref[0]
    kr_hi = kr.astype(BF16)
    kr_lo = (kr - kr_hi.astype(F32)).astype(BF16)
    place = place_ref[...]
    krp = (jnp.dot(kr_hi, place, preferred_element_type=F32)
           + jnp.dot(kr_lo, place, preferred_element_type=F32))
    cos = cos_ref[...]
    sin = sin_ref[...]
    knw = kn_ref[...]
    lane = lax.broadcasted_iota(jnp.int32, cos.shape, 1)
    for hd in range(A_HEADS):
        kh = kn[:, hd * HEAD_PAD:(hd + 1) * HEAD_PAD] + krp
        ss = jnp.sum(kh * kh, axis=-1, keepdims=True) * (1.0 / QK_DIM)
        kh = kh * lax.rsqrt(ss + EPS) * knw
        kh = _rope_head(kh, cos, sin, lane)
        k_ref[0, :, hd * HEAD_PAD:(hd + 1) * HEAD_PAD] = kh.astype(BF16)


def _mla_kv(ckv, kr, wuk_pad, wuv_pad, vone, kn_pad, place, cos_t, sin_t, tb):
    B, S, _ = ckv.shape
    tab_spec = pl.BlockSpec((tb, LANES), lambda b, i: (i, 0))
    wide = pl.BlockSpec((1, tb, A_HEADS * HEAD_PAD), lambda b, i: (b, i, 0))
    return pl.pallas_call(
        _mla_kv_kernel,
        grid=(B, S // tb),
        in_specs=[pl.BlockSpec((1, tb, KV_LORA), lambda b, i: (b, i, 0)),
                  pl.BlockSpec((1, tb, ROPE_DIM), lambda b, i: (b, i, 0)),
                  _full_spec(wuk_pad), _full_spec(wuv_pad), _full_spec(vone), _full_spec(kn_pad),
                  _full_spec(place), tab_spec, tab_spec],
        out_specs=[wide, wide],
        out_shape=[jax.ShapeDtypeStruct((B, S, A_HEADS * HEAD_PAD), BF16),
                   jax.ShapeDtypeStruct((B, S, A_HEADS * HEAD_PAD), BF16)],
        compiler_params=_cparams(("parallel", "parallel")),
        name="mla_kv",
    )(ckv, kr, wuk_pad, wuv_pad, vone, kn_pad, place, cos_t, sin_t)


def _mla_attn_kernel(q_ref, k_ref, v_ref, o_ref, m_sc, acc_sc, *, tq, tk, causal, nk):
    i = pl.program_id(2)
    m_sc[...] = jnp.full(m_sc.shape, -jnp.inf, F32)
    acc_sc[...] = jnp.zeros(acc_sc.shape, F32)

    def block(start, masked):
        k = k_ref[0, pl.ds(start, tk), :]
        v = v_ref[0, pl.ds(start, tk), :]
        if masked:
            qc = lax.broadcasted_iota(jnp.int32, (tq, tk), 0) // CHUNK
            kc = lax.broadcasted_iota(jnp.int32, (tq, tk), 1) // CHUNK
            allowed = kc <= qc
        for hh in range(2):
            qh = q_ref[0, :, hh * HEAD_PAD:(hh + 1) * HEAD_PAD]
            s = lax.dot_general(qh, k[:, hh * HEAD_PAD:(hh + 1) * HEAD_PAD], _NT,
                                preferred_element_type=F32)
            if masked:
                s = jnp.where(allowed, s, NEG)
            m_prev = m_sc[hh]
            m_new = jnp.maximum(m_prev, jnp.max(s, axis=-1, keepdims=True))
            alpha = jnp.exp(m_prev - m_new)
            if tk % LANES == 0:
                m_wide = jnp.concatenate([m_new] * (tk // LANES), axis=1)
            else:
                m_wide = m_new[:, :1]
            p = jnp.exp((s - m_wide).astype(BF16))
            acc_sc[hh] = alpha * acc_sc[hh] + jnp.dot(p, v[:, hh * HEAD_PAD:(hh + 1) * HEAD_PAD],
                                                      preferred_element_type=F32)
            m_sc[hh] = m_new

    def body(j, carry):
        block(pl.multiple_of(j * tk, tk), False)
        return carry

    if causal:
        lax.fori_loop(0, i, body, 0)
        block(pl.multiple_of(i * tk, tk), True)
    else:
        lax.fori_loop(0, nk, body, 0)

    lane = lax.broadcasted_iota(jnp.int32, (tq, LANES), 1)
    outs = []
    for hh in range(2):
        acc = acc_sc[hh]
        outs.append(acc / acc[:, V_DIM:V_DIM + 1])
    o = jnp.where(lane < V_DIM, outs[0], pltpu.roll(outs[1], V_DIM, 1))
    o_ref[0] = o.astype(BF16)


def _mla_attn(q, k, v, tq, tk, causal):
    B, T, _ = q.shape
    S = k.shape[1]
    nq, nk = T // tq, S // tk
    assert T % tq == 0 and S % tk == 0 and (not causal or (tq == tk and T == S))
    kv_spec = pl.BlockSpec((1, S, 2 * HEAD_PAD), lambda b, hp, i: (b, 0, hp))
    return pl.pallas_call(
        functools.partial(_mla_attn_kernel, tq=tq, tk=tk, causal=causal, nk=nk),
        grid=(B, A_HEADS // 2, nq),
        in_specs=[pl.BlockSpec((1, tq, 2 * HEAD_PAD), lambda b, hp, i: (b, i, hp)), kv_spec, kv_spec],
        out_specs=pl.BlockSpec((1, tq, 2 * V_DIM), lambda b, hp, i: (b, i, hp)),
        out_shape=jax.ShapeDtypeStruct((B, T, A_HEADS * V_DIM), BF16),
        scratch_shapes=[pltpu.VMEM((2, tq, LANES), F32), pltpu.VMEM((2, tq, LANES), F32)],
        compiler_params=_cparams(("parallel", "parallel", "arbitrary")),
        name="mla_attn",
    )(q, k, v)


def _odd_even_merge_sort_pairs(n):
    pairs = []

    def merge(lo, m, r):
        step = r * 2
        if step < m:
            merge(lo, m, step)
            merge(lo + r, m, step)
            pairs.extend((i, i + r) for i in range(lo + r, lo + m - r, step))
        else:
            pairs.append((lo, lo + r))

    def sort(lo, m):
        if m > 1:
            sort(lo, m // 2)
            sort(lo + m // 2, m // 2)
            merge(lo, m, 1)

    sort(0, n)
    return pairs


_SORT16 = _odd_even_merge_sort_pairs(P_TOPK)
_BITONIC16 = [(i, i + d) for d in (8, 4, 2, 1) for i in range(P_TOPK) if not i & d]
SUBLANES = 8


def _top16_rows(chains):
    assert all(s.shape[0] == P_TOPK * SUBLANES for s, _, _ in chains)
    vs = [[s[SUBLANES * j:SUBLANES * (j + 1), :] for j in range(P_TOPK)] for s, _, _ in chains]

    def compare_exchange(pairs):
        for i, j in pairs:
            for v in vs:
                v[i], v[j] = jnp.maximum(v[i], v[j]), jnp.minimum(v[i], v[j])

    compare_exchange(_SORT16)
    for shift in (4, 2, 1):
        for v in vs:
            other = [pltpu.roll(x, shift, 0) for x in v]
            v[:] = [jnp.maximum(v[k], other[P_TOPK - 1 - k]) for k in range(P_TOPK)]
        compare_exchange(_BITONIC16)
    for v, (_, t_sc, cols) in zip(vs, chains):
        for k in range(P_TOPK):
            t_sc[k:k + 1, cols] = v[k][0:1, :]


def _route_tiles(tiles, t1_sc, t2_sc):
    _top16_rows([(s1, t1_sc, cols) for s1, _, cols in tiles] + [(s2, t2_sc, cols) for _, s2, cols in tiles])
    state = []
    for s1, s2, cols in tiles:
        row8 = lax.broadcasted_iota(jnp.int32, (8, s1.shape[1]), 0)
        t1 = [t1_sc[k:k + 1, cols] for k in range(P_TOPK)]
        t2 = [t2_sc[k:k + 1, cols] for k in range(P_TOPK)]
        t2_lo = t2_sc[0:8, cols]
        t2_hi = t2_sc[8:16, cols]
        t1_hi = t1_sc[8:16, cols]
        cands = [t1[0] + t2_lo, t1[0] + t2_hi, t1_hi + t2[0]]
        for k1 in range(1, 8):
            lim = P_TOPK // (k1 + 1)
            c = t1[k1] + t2_lo
            cands.append(c if lim >= 8 else jnp.where(row8 < lim, c, -jnp.inf))
        top = t1[0] + t2[0]
        state.append(dict(t1=t1, t2=t2, cands=cands, top=top, z=jnp.zeros_like(top), tau=top))
    for k in range(P_TOPK):
        for st in state:
            m = st["cands"][0]
            for c in st["cands"][1:]:
                m = jnp.maximum(m, c)
            m = jnp.max(m, axis=0, keepdims=True)
            st["z"] = st["z"] + jnp.exp(m - st["top"])
            st["tau"] = m
            if k + 1 < P_TOPK:
                st["cands"] = [jnp.where(c == m, -jnp.inf, c) for c in st["cands"]]
    outs = []
    for (s1, s2, cols), st in zip(tiles, state):
        cnt = jnp.zeros_like(s1)
        rank = jnp.zeros_like(s2)
        for k in range(P_TOPK):
            cnt = jnp.where(s1 + st["t2"][k] >= st["tau"], float(k + 1), cnt)
            rank = jnp.where(st["t2"][k] > s2, float(k + 1), rank)
        n1 = jnp.sum(jnp.where(s1 >= st["t1"][P_TOPK - 1], 1.0, 0.0), axis=0, keepdims=True)
        n2 = jnp.sum(jnp.where(s2 >= st["t2"][P_TOPK - 1], 1.0, 0.0), axis=0, keepdims=True)
        npair = jnp.sum(cnt, axis=0, keepdims=True)
        k = float(P_TOPK)
        tied = jnp.where((n1 != k) | (n2 != k) | (npair != k), 1.0, 0.0)
        outs.append((rank, cnt, jnp.exp(s2 - st["t2"][0]), jnp.exp(s1 - st["t1"][0]) / st["z"], tied))
    return outs


def _route_tile_exact(s1, s2, t1_sc, t2_sc, cols):
    nk, w = s1.shape
    key = lax.broadcasted_iota(jnp.int32, (nk, w), 0).astype(F32)

    def top16(s, t_sc):
        work, rank, tops = s, jnp.full(s.shape, float(P_TOPK), F32), []
        for k in range(P_TOPK):
            m = jnp.max(work, axis=0, keepdims=True)
            first = jnp.min(jnp.where(work == m, key, float(nk)), axis=0, keepdims=True)
            hit = key == first
            rank = jnp.where(hit, float(k), rank)
            work = jnp.where(hit, -jnp.inf, work)
            t_sc[k:k + 1, cols] = m
            tops.append(m)
        return tops, rank

    t1, rank1 = top16(s1, t1_sc)
    t2, rank2 = top16(s2, t2_sc)
    row8 = lax.broadcasted_iota(jnp.int32, (8, w), 0).astype(F32)
    t2_lo = t2_sc[0:8, cols]
    t2_hi = t2_sc[8:16, cols]
    t1_hi = t1_sc[8:16, cols]
    cands = [(t1[0] + t2_lo, row8), (t1[0] + t2_hi, row8 + 8.0), (t1_hi + t2[0], (row8 + 8.0) * P_TOPK)]
    for k1 in range(1, 8):
        lim = P_TOPK // (k1 + 1)
        c = t1[k1] + t2_lo
        cands.append((c if lim >= 8 else jnp.where(row8 < lim, c, -jnp.inf), row8 + float(k1 * P_TOPK)))
    top = t1[0] + t2[0]
    z = jnp.zeros_like(top)
    k1_row = lax.broadcasted_iota(jnp.int32, (P_TOPK, w), 0).astype(F32)
    cnt_k1 = jnp.zeros((P_TOPK, w), F32)
    for k in range(P_TOPK):
        m = cands[0][0]
        for c, _ in cands[1:]:
            m = jnp.maximum(m, c)
        m = jnp.max(m, axis=0, keepdims=True)
        first = None
        for c, pos in cands:
            f = jnp.min(jnp.where(c == m, pos, float(P_TOPK * P_TOPK)), axis=0, keepdims=True)
            first = f if first is None else jnp.minimum(first, f)
        z = z + jnp.exp(m - top)
        cnt_k1 = cnt_k1 + jnp.where(k1_row == jnp.floor(first * (1.0 / P_TOPK)), 1.0, 0.0)
        cands = [(jnp.where(pos == first, -jnp.inf, c), pos) for c, pos in cands]
    t1_sc[:, cols] = cnt_k1
    cnt = jnp.zeros_like(s1)
    for k1 in range(P_TOPK):
        cnt = jnp.where(rank1 == float(k1), t1_sc[k1:k1 + 1, cols], cnt)
    return rank2, cnt, jnp.exp(s1 - t1[0]) / z


def _post_mix_kernel(o_ref, x_ref, g_ref, sh_ref, sc_ref, nw_ref, wo_ref, wq_ref, sk1_ref, sk2_ref,
                     x1_ref, h2_ref, rank_ref, cnt_ref, e2_ref, r_ref, t1_sc, t2_sc, s_sc, tied_sc):
    mix = jnp.dot(o_ref[0], wo_ref[...], preferred_element_type=F32)
    x1 = x_ref[0] + g_ref[0] * mix
    x1_ref[0] = x1
    h2 = _modulate(x1, nw_ref[...], sh_ref[0], sc_ref[0]).astype(BF16)
    h2_ref[0] = h2
    tb = h2.shape[0]
    q = jnp.dot(h2, wq_ref[...], preferred_element_type=F32).astype(BF16)
    for h in range(P_HEADS):
        qh = q[:, h * 2 * P_HALF:(h + 1) * 2 * P_HALF]
        s_sc[h, 0] = lax.dot_general(sk1_ref[h], qh, _NT, preferred_element_type=F32)
        s_sc[h, 1] = lax.dot_general(sk2_ref[h], qh, _NT, preferred_element_type=F32)

    def head_body(h, carry):
        s1 = s_sc[h, 0]
        s2 = s_sc[h, 1]
        lane_tiles = [slice(nt * LANES, (nt + 1) * LANES) for nt in range(tb // LANES)]
        routed = _route_tiles([(s1[:, cols], s2[:, cols], cols) for cols in lane_tiles], t1_sc, t2_sc)
        for cols, (rank, cnt, e2, r, tied) in zip(lane_tiles, routed):
            rank_ref[h, :, cols] = rank.astype(BF16)
            cnt_ref[h, :, cols] = cnt
            e2_ref[h, :, cols] = e2.astype(BF16)
            r_ref[h, :, cols] = r
            tied_sc[h, :, cols] = jnp.broadcast_to(tied, (8, tied.shape[1]))
        return carry

    lax.fori_loop(0, P_HEADS, head_body, 0)

    @pl.when(jnp.max(tied_sc[...]) > 0.0)
    def _():
        def fix_head(h, carry):
            for nt in range(tb // LANES):
                cols = slice(nt * LANES, (nt + 1) * LANES)

                @pl.when(jnp.max(tied_sc[h, :, cols]) > 0.0)
                def _(cols=cols):
                    rank_x, cnt_x, r_x = _route_tile_exact(s_sc[h, 0, :, cols], s_sc[h, 1, :, cols],
                                                           t1_sc, t2_sc, cols)
                    rank_ref[h, :, cols] = rank_x.astype(BF16)
                    cnt_ref[h, :, cols] = cnt_x
                    r_ref[h, :, cols] = r_x
            return carry

        lax.fori_loop(0, P_HEADS, fix_head, 0)


def _post_mix(o, x, gate, shift, scale, nw, w_o, wq_heads, sk1, sk2, tb):
    B, T, D = x.shape
    nT = T // tb
    n = B * T
    tok = lambda b, i: (b, i, 0)
    rt_spec = pl.BlockSpec((P_HEADS, N_KEYS, tb), lambda b, i: (0, 0, b * nT + i))
    rt_shape = jax.ShapeDtypeStruct((P_HEADS, N_KEYS, n), F32)
    rt_shape_b = jax.ShapeDtypeStruct((P_HEADS, N_KEYS, n), BF16)
    return pl.pallas_call(
        _post_mix_kernel,
        grid=(B, nT),
        in_specs=[pl.BlockSpec((1, tb, o.shape[2]), tok), pl.BlockSpec((1, tb, D), tok),
                  _mod_spec(gate, tb), _mod_spec(shift, tb), _mod_spec(scale, tb),
                  _full_spec(nw), _full_spec(w_o), _full_spec(wq_heads), _full_spec(sk1), _full_spec(sk2)],
        out_specs=[pl.BlockSpec((1, tb, D), tok), pl.BlockSpec((1, tb, D), tok),
                   rt_spec, rt_spec, rt_spec, rt_spec],
        out_shape=[jax.ShapeDtypeStruct((B, T, D), F32), jax.ShapeDtypeStruct((B, T, D), BF16),
                   rt_shape_b, rt_shape, rt_shape_b, rt_shape],
        scratch_shapes=[pltpu.VMEM((P_TOPK, tb), F32), pltpu.VMEM((P_TOPK, tb), F32),
                        pltpu.VMEM((P_HEADS, 2, N_KEYS, tb), F32), pltpu.VMEM((P_HEADS, 8, tb), F32)],
        compiler_params=_cparams(("parallel", "parallel")),
        name="post_mix_route",
    )(o, x, gate, shift, scale, nw, w_o, wq_heads, sk1, sk2)


def _gelu_tanh(a):
    c = math.sqrt(2.0 / math.pi)
    return (0.5 * a) * (1.0 + jnp.tanh(a * ((0.044715 * c) * (a * a) + c)))


def _peer_dense_kernel(h_ref, u_ref, vt_ref, rank_ref, cnt_ref, e2_ref, r_ref, x_ref, g_ref,
                       o_ref, acc_ref, a0_ref, a1_ref, hc0_ref, hc1_ref, *, ec, ne):
    g = pl.program_id(0)
    per = ec // N_KEYS
    s = lax.rem(jnp.maximum(g - 1, 0), ne) + 1
    c3 = lax.rem(jnp.maximum(g - 2, 0), ne)

    @pl.when(g == 0)
    def _():
        for ref in (a0_ref, a1_ref, hc0_ref, hc1_ref):
            ref[...] = jnp.zeros(ref.shape, ref.dtype)

    @pl.when(c3 == 0)
    def _():
        acc_ref[...] = jnp.zeros(acc_ref.shape, F32)

    def stages(a_out, a_in, hc_out, hc_in):
        zero = jnp.zeros((), BF16)
        tb = h_ref.shape[1]
        nsplit = PEER_SPLIT
        gate_w = min(GATE_LANES, tb)
        mxu_w = min(MXU_N, tb)
        assert tb % gate_w == 0 and tb % mxu_w == 0

        row_cache = {}
        reps = N_KEYS // BF16_ROWS

        def routed_rows(ii):
            if ii not in row_cache:
                i1 = jnp.clip((s - 1) * per + ii, 0, N_KEYS - 1)
                wide = lambda ref, h: jnp.broadcast_to(ref[h, pl.ds(i1, 1), :], (BF16_ROWS, tb)).astype(BF16)
                row_cache[ii] = [(wide(cnt_ref, h), wide(r_ref, h)) for h in range(P_HEADS)]
            return row_cache[ii]

        def gate_piece(ii, n):
            cols = slice(n * gate_w, (n + 1) * gate_w)
            rows = slice(ii * N_KEYS, (ii + 1) * N_KEYS)
            w = None
            for h, (cnt_row, r_row) in enumerate(routed_rows(ii)):
                cnt_t = jnp.concatenate([cnt_row[:, cols]] * reps, axis=0)
                r_t = jnp.concatenate([r_row[:, cols]] * reps, axis=0)
                contrib = jnp.where(rank_ref[h, :, cols] < cnt_t, e2_ref[h, :, cols], zero) * r_t
                w = contrib if w is None else w + contrib
            hc_out[rows, cols] = _gelu_tanh(a_in[rows, cols]) * w

        def score_piece(q, n):
            cols = slice(n * mxu_w, (n + 1) * mxu_w)
            rows = slice(q * (ec // nsplit), (q + 1) * (ec // nsplit))
            a_out[rows, cols] = lax.dot_general(u_ref[rows, :], h_ref[0, cols, :], _NT,
                                                preferred_element_type=F32).astype(BF16)

        def out_piece(q, n):
            cols = slice(n * mxu_w, (n + 1) * mxu_w)
            rows = slice(q * (D_MODEL // nsplit), (q + 1) * (D_MODEL // nsplit))
            acc_ref[rows, cols] += jnp.dot(vt_ref[0, rows, :], hc_in[:, cols], preferred_element_type=F32)

        vec = [functools.partial(gate_piece, ii, n) for n in range(tb // gate_w) for ii in range(per)]
        mxu = []
        for n in range(tb // mxu_w):
            for q in range(nsplit):
                mxu.append(functools.partial(score_piece, q, n))
                mxu.append(functools.partial(out_piece, q, n))
        for k in range(max(len(vec), len(mxu))):
            if k < len(mxu):
                mxu[k]()
            if k < len(vec):
                vec[k]()

    even = lax.rem(g, 2) == 0
    pl.when(even)(lambda: stages(a0_ref, a1_ref, hc1_ref, hc0_ref))
    pl.when(jnp.logical_not(even))(lambda: stages(a1_ref, a0_ref, hc0_ref, hc1_ref))

    @pl.when((c3 == ne - 1) & (g >= 2))
    def _():
        o_ref[0] = x_ref[0] + g_ref[0] * acc_ref[...].T


def _peer_dense(h2, u, vt, rank, cnt, e2, r, x, gate, tb, ec=PEER_EC):
    B, T, D = x.shape
    nT = T // tb
    nblk = B * nT
    ne = N_EXPERTS // ec
    assert vt.shape == (ne, D, ec)
    blk1 = lambda g: jnp.minimum(g // ne, nblk - 1)
    blk2 = lambda g: jnp.minimum(jnp.maximum(g - 1, 0) // ne, nblk - 1)
    blk3 = lambda g: jnp.maximum(g - 2, 0) // ne
    tok1 = lambda g: (blk1(g) // nT, blk1(g) % nT, 0)
    tok3 = lambda g: (blk3(g) // nT, blk3(g) % nT, 0)
    rt_spec = pl.BlockSpec((P_HEADS, N_KEYS, tb), lambda g: (0, 0, blk2(g)))
    if gate.shape[1] == 1:
        gate_spec = pl.BlockSpec((1, 1, D), lambda g: (blk3(g) // nT, 0, 0))
    else:
        gate_spec = pl.BlockSpec((1, tb, D), tok3)
    return pl.pallas_call(
        functools.partial(_peer_dense_kernel, ec=ec, ne=ne),
        grid=(nblk * ne + 2,),
        in_specs=[pl.BlockSpec((1, tb, D), tok1),
                  pl.BlockSpec((ec, D), lambda g: (g % ne, 0)),
                  pl.BlockSpec((1, D, ec), lambda g: (jnp.maximum(g - 2, 0) % ne, 0, 0)),
                  rt_spec, rt_spec, rt_spec, rt_spec,
                  pl.BlockSpec((1, tb, D), tok3), gate_spec],
        out_specs=pl.BlockSpec((1, tb, D), tok3),
        out_shape=jax.ShapeDtypeStruct((B, T, D), F32),
        scratch_shapes=[pltpu.VMEM((D, tb), F32),
                        pltpu.VMEM((ec, tb), BF16), pltpu.VMEM((ec, tb), BF16),
                        pltpu.VMEM((ec, tb), BF16), pltpu.VMEM((ec, tb), BF16)],
        compiler_params=_cparams(("arbitrary",)),
        name="peer_dense",
    )(h2, u, vt, rank, cnt, e2, r, x, gate)


def _pair_rms(y, w, lane):
    y2 = y * y
    lo = jnp.sum(jnp.where(lane < B_HEAD_DIM, y2, 0.0), axis=-1, keepdims=True)
    hi = jnp.sum(y2, axis=-1, keepdims=True) - lo
    ms = jnp.where(lane < B_HEAD_DIM, lo, hi) * (1.0 / B_HEAD_DIM)
    return y * lax.rsqrt(ms + EPS) * w


def _shared_kv_kernel(x_ref, sh_ref, sc_ref, nw_ref, wk_ref, wv_ref, kn_ref, k_ref, v_ref, kb_ref, vb_ref,
                      *, first_tail):
    h = _modulate(x_ref[0], nw_ref[...], sh_ref[0], sc_ref[0]).astype(BF16)
    kraw = jnp.dot(h, wk_ref[...], preferred_element_type=F32)
    v = jnp.dot(h, wv_ref[...], preferred_element_type=F32)
    vb_ref[0] = v.astype(BF16)
    knw = kn_ref[...]
    lane = lax.broadcasted_iota(jnp.int32, (h.shape[0], LANES), 1)
    ks = [_pair_rms(kraw[:, hp * LANES:(hp + 1) * LANES], knw, lane) for hp in range(B_HEADS // 2)]
    for hp, kh in enumerate(ks):
        kb_ref[0, :, hp * LANES:(hp + 1) * LANES] = kh.astype(BF16)

    @pl.when(pl.program_id(1) >= first_tail)
    def _():
        v_ref[0] = v
        for hp, kh in enumerate(ks):
            k_ref[0, :, hp * LANES:(hp + 1) * LANES] = kh


def _shared_kv(x, shift, scale, nw, wk, wv, kn_pair, tb, tail):
    B, T, D = x.shape
    nT = T // tb
    assert tail % tb == 0 and tail <= T
    first_tail = nT - tail // tb
    tok = lambda b, i: (b, i, 0)
    blk = pl.BlockSpec((1, tb, D), tok)
    tail_blk = pl.BlockSpec((1, tb, D), lambda b, i: (b, jnp.maximum(i - first_tail, 0), 0))
    return pl.pallas_call(
        functools.partial(_shared_kv_kernel, first_tail=first_tail),
        grid=(B, nT),
        in_specs=[blk, _mod_spec(shift, tb), _mod_spec(scale, tb), _full_spec(nw),
                  _full_spec(wk), _full_spec(wv), _full_spec(kn_pair)],
        out_specs=[tail_blk, tail_blk, blk, blk],
        out_shape=[jax.ShapeDtypeStruct((B, tail, D), F32), jax.ShapeDtypeStruct((B, tail, D), F32),
                   jax.ShapeDtypeStruct((B, T, D), BF16), jax.ShapeDtypeStruct((B, T, D), BF16)],
        compiler_params=_cparams(("parallel", "arbitrary")),
        name="shared_kv",
    )(x, shift, scale, nw, wk, wv, kn_pair)


def _band_pre_kernel(x_ref, sh_ref, sc_ref, nw_ref, wq_ref, qn_ref, q_ref):
    h = _modulate(x_ref[0], nw_ref[...], sh_ref[0], sc_ref[0]).astype(BF16)
    q = jnp.dot(h, wq_ref[...], preferred_element_type=F32)
    qnw = qn_ref[...]
    lane = lax.broadcasted_iota(jnp.int32, (h.shape[0], LANES), 1)
    scale = B_HEAD_DIM ** -0.5
    for hp in range(B_HEADS // 2):
        qh = _pair_rms(q[:, hp * LANES:(hp + 1) * LANES], qnw, lane)
        q_ref[0, :, hp * LANES:(hp + 1) * LANES] = (qh * scale).astype(BF16)


def _band_pre(x, shift, scale, nw, wq, qn_pair, tb):
    B, T, D = x.shape
    tok = lambda b, i: (b, i, 0)
    blk = pl.BlockSpec((1, tb, D), tok)
    return pl.pallas_call(
        _band_pre_kernel,
        grid=(B, T // tb),
        in_specs=[blk, _mod_spec(shift, tb), _mod_spec(scale, tb), _full_spec(nw),
                  _full_spec(wq), _full_spec(qn_pair)],
        out_specs=blk,
        out_shape=jax.ShapeDtypeStruct((B, T, D), BF16),
        compiler_params=_cparams(("parallel", "parallel")),
        name="band_pre",
    )(x, shift, scale, nw, wq, qn_pair)


def _band_bias_kernel(tab_ref, o_ref):
    h = pl.program_id(0)
    nvar = 2 * LANES
    r = lax.broadcasted_iota(jnp.int32, (BAND_SUB, nvar), 0)
    w = lax.broadcasted_iota(jnp.int32, (BAND_SUB, nvar), 1) + (BAND_WIN - nvar)
    idx = jnp.clip(r + B_WINDOW - w, -REL_CLIP, REL_CLIP) + REL_CLIP
    far = tab_ref[h, 2 * REL_CLIP]

    def body(t, acc):
        return jnp.where(idx == t, tab_ref[h, t], acc)

    var = lax.fori_loop(0, 2 * REL_CLIP, body, jnp.full((BAND_SUB, nvar), far, F32))
    full = jnp.concatenate([jnp.full((BAND_SUB, BAND_WIN - nvar), far, F32), var], axis=1)
    rr = lax.broadcasted_iota(jnp.int32, (BAND_SUB, BAND_WIN), 0)
    ww = lax.broadcasted_iota(jnp.int32, (BAND_SUB, BAND_WIN), 1)
    qc = rr // CHUNK + LEFT_CHUNKS
    kc = ww // CHUNK
    allowed = (kc <= qc) & (kc >= qc - LEFT_CHUNKS)
    o_ref[0] = jnp.where(allowed, full, NEG)


def _band_bias(table):
    nh = table.shape[0]
    return pl.pallas_call(
        _band_bias_kernel,
        grid=(nh,),
        in_specs=[pl.BlockSpec(memory_space=pltpu.SMEM)],
        out_specs=pl.BlockSpec((1, BAND_SUB, BAND_WIN), lambda h: (h, 0, 0)),
        out_shape=jax.ShapeDtypeStruct((nh, BAND_SUB, BAND_WIN), F32),
        compiler_params=_cparams(("arbitrary",)),
        name="band_bias",
    )(table)


def _band_windows(windows, bias_ref, lane):
    chains = [(w, hh) for w in range(len(windows)) for hh in range(2)]
    scores = []
    for w, hh in chains:
        qs, kw, _ = windows[w]
        nkw = kw.shape[0]
        sel = (lane < B_HEAD_DIM) if hh == 0 else (lane >= B_HEAD_DIM)
        qh = jnp.where(sel, qs, jnp.zeros_like(qs))
        s = lax.dot_general(qh, kw, _NT, preferred_element_type=F32)
        scores.append(s + bias_ref[hh, :, BAND_WIN - nkw:BAND_WIN])
    maxes = [jnp.max(s, axis=-1, keepdims=True) for s in scores]
    probs = [jnp.exp((s - m).astype(BF16)) for s, m in zip(scores, maxes)]
    vext = []
    for _, _, vw in windows:
        ones_lane = (lax.broadcasted_iota(jnp.int32, vw.shape, 1) == 0).astype(BF16)
        vext.append(jnp.concatenate([vw, ones_lane], axis=1))
    outs = []
    for (w, hh), p in zip(chains, probs):
        pv = jnp.dot(p, vext[w], preferred_element_type=F32)
        outs.append(pv[:, :LANES] / pv[:, LANES:LANES + 1])
    return [jnp.where(lane < B_HEAD_DIM, outs[2 * w], outs[2 * w + 1]) for w in range(len(windows))]


def _band_attn_kernel(q_ref, kp_ref, kc_ref, vp_ref, vc_ref, bias_ref, o_ref, *, tq):
    i = pl.program_id(2)
    lane = lax.broadcasted_iota(jnp.int32, (BAND_SUB, LANES), 1)
    nsub = tq // BAND_SUB

    def run(first):
        windows = []
        for c in range(nsub):
            qs = q_ref[0, c * BAND_SUB:(c + 1) * BAND_SUB, :]
            hi = (c + 1) * BAND_SUB
            if first or hi >= BAND_WIN:
                lo = max(hi - BAND_WIN, 0)
                kw = kc_ref[0, lo:hi, :]
                vw = vc_ref[0, lo:hi, :]
            else:
                lo = tq - (BAND_WIN - hi)
                kw = jnp.concatenate([kp_ref[0, lo:tq, :], kc_ref[0, 0:hi, :]], axis=0)
                vw = jnp.concatenate([vp_ref[0, lo:tq, :], vc_ref[0, 0:hi, :]], axis=0)
            windows.append((qs, kw, vw))
        for c, o in enumerate(_band_windows(windows, bias_ref, lane)):
            o_ref[0, c * BAND_SUB:(c + 1) * BAND_SUB, :] = o.astype(BF16)

    pl.when(i == 0)(lambda: run(True))
    pl.when(i > 0)(lambda: run(False))


def _band_attn(q, k, v, bias, tq):
    B, T, D = q.shape
    assert tq >= B_WINDOW and T % tq == 0
    cur = lambda b, hp, i: (b, i, hp)
    prev = lambda b, hp, i: (b, jnp.maximum(i - 1, 0), hp)
    blk = lambda m: pl.BlockSpec((1, tq, LANES), m)
    return pl.pallas_call(
        functools.partial(_band_attn_kernel, tq=tq),
        grid=(B, B_HEADS // 2, T // tq),
        in_specs=[blk(cur), blk(prev), blk(cur), blk(prev), blk(cur),
                  pl.BlockSpec((2, BAND_SUB, BAND_WIN), lambda b, hp, i: (hp, 0, 0))],
        out_specs=blk(cur),
        out_shape=jax.ShapeDtypeStruct((B, T, D), BF16),
        compiler_params=_cparams(("parallel", "parallel", "arbitrary")),
        name="band_attn",
    )(q, k, k, v, v, bias)


def _band_step_kernel(q_ref, k_ref, v_ref, bias_ref, o_ref):
    lane = lax.broadcasted_iota(jnp.int32, (BAND_SUB, LANES), 1)
    o, = _band_windows([(q_ref[0], k_ref[0], v_ref[0])], bias_ref, lane)
    o_ref[0] = o.astype(BF16)


def _band_step(q, kwin, vwin, bias):
    B, _, D = q.shape
    return pl.pallas_call(
        _band_step_kernel,
        grid=(B, B_HEADS // 2),
        in_specs=[pl.BlockSpec((1, BAND_SUB, LANES), lambda b, hp: (b, 0, hp)),
                  pl.BlockSpec((1, BAND_WIN, LANES), lambda b, hp: (b, 0, hp)),
                  pl.BlockSpec((1, BAND_WIN, LANES), lambda b, hp: (b, 0, hp)),
                  pl.BlockSpec((2, BAND_SUB, BAND_WIN), lambda b, hp: (hp, 0, 0))],
        out_specs=pl.BlockSpec((1, BAND_SUB, LANES), lambda b, hp: (b, 0, hp)),
        out_shape=jax.ShapeDtypeStruct((B, BAND_SUB, D), BF16),
        compiler_params=_cparams(("parallel", "parallel")),
        name="band_step",
    )(q, kwin, vwin, bias)


def _rope_tables(pos):
    half = ROPE_DIM // 2
    freqs = ROPE_THETA ** (-jnp.arange(half, dtype=F32) / half)
    ang = pos.astype(F32)[:, None] * freqs[None, :]
    c, s = jnp.cos(ang), jnp.sin(ang)
    n = pos.shape[0]
    ones = jnp.ones((n, NOPE_DIM), F32)
    zeros = jnp.zeros((n, NOPE_DIM), F32)
    pad1 = jnp.ones((n, HEAD_PAD - QK_DIM), F32)
    pad0 = jnp.zeros((n, HEAD_PAD - QK_DIM), F32)
    return (jnp.concatenate([ones, c, c, pad1], axis=1),
            jnp.concatenate([zeros, -s, s, pad0], axis=1))


def _pad_heads(w, nheads, width):
    lead = w.shape[:-1]
    w = w.reshape(lead + (nheads, width))
    w = jnp.pad(w, [(0, 0)] * len(lead) + [(0, 0), (0, HEAD_PAD - width)])
    return w.reshape(lead + (nheads * HEAD_PAD,))


def _block(n, pref):
    for t in pref:
        if n % t == 0:
            return t
    return n


def kernel(x_prompt, x_sample, c_prompt, c_sample, cache_a_ckv, cache_a_krope, cache_b_k, cache_b_v, ada_w, ada_b, norm_mix_w, norm_ffn_w, a_w_in, a_q_lora_norm, a_kv_lora_norm, a_w_uq, a_w_ukv, a_q_norm, a_k_norm, a_w_o, kv_ada_w, kv_ada_b, kv_norm_w, b_w_kv, b_k_norm, b_w_q, b_q_norm, b_rel_bias, b_w_o, p_w_q, p_subkeys, p_u, p_v):
    D = D_MODEL
    Bp, Tp, _ = x_prompt.shape
    Bs, Ts, _ = x_sample.shape
    P = cache_a_ckv.shape[2]
    Pb = cache_b_k.shape[1]
    assert Ts == CHUNK and Pb == B_WINDOW and P % CHUNK == 0

    ada_all_w = jnp.concatenate([ada_w[0], ada_w[1], kv_ada_w], axis=1).astype(BF16)
    ada_all_b = jnp.concatenate([ada_b[0], ada_b[1], kv_ada_b])[None, :]
    c_all = jnp.concatenate([c_prompt, c_sample], axis=0)
    mod = _ada(c_all, ada_all_w, ada_all_b)

    w_in = a_w_in[0].astype(BF16)
    qln = a_q_lora_norm[0][None, :]
    kvln = a_kv_lora_norm[0][None, :]
    wuq_pad = _pad_heads(a_w_uq[0], A_HEADS, QK_DIM).astype(BF16)
    wukv = a_w_ukv[0].reshape(KV_LORA, A_HEADS, NOPE_DIM + V_DIM)
    wuk_pad = jnp.pad(wukv[:, :, :NOPE_DIM], ((0, 0), (0, 0), (0, HEAD_PAD - NOPE_DIM))
                      ).reshape(KV_LORA, A_HEADS * HEAD_PAD).astype(BF16)
    wuv_pad = jnp.pad(wukv[:, :, NOPE_DIM:], ((0, 0), (0, 0), (0, HEAD_PAD - V_DIM))
                      ).reshape(KV_LORA, A_HEADS * HEAD_PAD).astype(BF16)
    vone = jnp.asarray(np.tile(np.eye(1, HEAD_PAD, k=V_DIM), (1, A_HEADS)), F32)
    qn_pad = jnp.pad(a_q_norm[0], (0, HEAD_PAD - QK_DIM))[None, :]
    kn_pad = jnp.pad(a_k_norm[0], (0, HEAD_PAD - QK_DIM))[None, :]
    place = jnp.asarray(np.eye(ROPE_DIM, HEAD_PAD, k=NOPE_DIM), BF16)
    a_wo = a_w_o[0].astype(BF16)
    wk_b = b_w_kv[:, :D].astype(BF16)
    wv_b = b_w_kv[:, D:].astype(BF16)
    bkn_pair = jnp.tile(b_k_norm, 2)[None, :]
    bqn_pair = jnp.tile(b_q_norm[0], 2)[None, :]
    bwq = b_w_q[0].astype(BF16)
    bwo = b_w_o[0].astype(BF16)
    bias_tile = _band_bias(b_rel_bias[0])

    def peer_weights(layer):
        wq = p_w_q[layer].astype(BF16)
        sk = p_subkeys[layer]
        sk1 = jnp.pad(sk[:, 0], ((0, 0), (0, 0), (0, P_HALF))).astype(BF16)
        sk2 = jnp.pad(sk[:, 1], ((0, 0), (0, 0), (P_HALF, 0))).astype(BF16)
        vt = p_v[layer].astype(BF16).reshape(N_EXPERTS // PEER_EC, PEER_EC, D).transpose(0, 2, 1)
        return wq, sk1, sk2, p_u[layer].astype(BF16), vt

    peer_w = [peer_weights(0), peer_weights(1)]
    norm_mix = norm_mix_w[:, None, :]
    norm_ffn = norm_ffn_w[:, None, :]
    kv_nw = kv_norm_w[None, :]

    def run(x, modp, per_token, pos_q, past):
        B, T, _ = x.shape
        if per_token:
            modv = jnp.repeat(modp, T, axis=0)[None]
            xw = x.reshape(1, B * T, D)
        else:
            modv = modp[:, None, :]
            xw = x
        Bw, Tw, _ = xw.shape
        sl = lambda k: modv[:, :, k * D:(k + 1) * D]
        tb = _block(Tw, (512, 256, 128))
        tbr = _block(Tw, (512, 256, 128))
        tbe = _block(Tw, (512, 256, 128))

        cos_q, sin_q = _rope_tables(pos_q)
        ckv, kr, q = _mla_pre(xw, sl(0), sl(1), norm_mix[0], w_in, qln, kvln, wuq_pad, qn_pad,
                              cos_q, sin_q, tb)
        ckv = ckv.reshape(B, T, KV_LORA)
        kr = kr.reshape(B, T, ROPE_DIM)
        q = q.reshape(B, T, A_HEADS * HEAD_PAD)
        if past is None:
            ckv_all, kr_all = ckv, kr
            pos_k = pos_q[:T]
        else:
            ckv_all = jnp.concatenate([past[0], ckv], axis=1)
            kr_all = jnp.concatenate([past[1], kr], axis=1)
            pos_k = jnp.arange(P + T, dtype=jnp.int32)
        S = ckv_all.shape[1]
        cos_k, sin_k = _rope_tables(pos_k)
        tbk = _block(S, (512, 704, 256, 192, 64))
        k, v = _mla_kv(ckv_all, kr_all, wuk_pad, wuv_pad, vone, kn_pad, place, cos_k, sin_k, tbk)
        if past is None:
            ta = _block(T, (512, 256, 128, 64))
            o = _mla_attn(q, k, v, ta, ta, True)
        else:
            o = _mla_attn(q, k, v, T, tbk, False)
        o = o.reshape(Bw, Tw, A_HEADS * V_DIM)

        wq, sk1, sk2, u, vt = peer_w[0]
        x1, h2, rank, cnt, e2, r = _post_mix(o, xw, sl(2), sl(3), sl(4), norm_ffn[0], a_wo, wq, sk1, sk2, tbr)
        x2 = _peer_dense(h2, u, vt, rank, cnt, e2, r, x1, sl(5), tbe)

        tail = B_WINDOW if past is None else Tw
        kf, vf, kb, vb = _shared_kv(x2, sl(12), sl(13), kv_nw, wk_b, wv_b, bkn_pair, tb, tail)

        qb = _band_pre(x2, sl(6), sl(7), norm_mix[1], bwq, bqn_pair, tb)
        if past is None:
            ob = _band_attn(qb, kb, vb, bias_tile, B_WINDOW)
            new_bk = kf.reshape(B, B_WINDOW, B_HEADS, B_HEAD_DIM)
            new_bv = vf.reshape(B, B_WINDOW, B_HEADS, B_HEAD_DIM)
        else:
            zq = jnp.zeros((B, CHUNK, D), BF16)
            qpad = jnp.concatenate([zq, qb.reshape(B, T, D)], axis=1)
            kwin = jnp.concatenate([zq, past[2].reshape(B, Pb, D).astype(BF16), kb.reshape(B, T, D)], axis=1)
            vwin = jnp.concatenate([zq, past[3].reshape(B, Pb, D).astype(BF16), vb.reshape(B, T, D)], axis=1)
            ob = _band_step(qpad, kwin, vwin, bias_tile)[:, CHUNK:].reshape(Bw, Tw, D)
            new_bk = jnp.concatenate([past[2], kf.reshape(B, T, B_HEADS, B_HEAD_DIM)], axis=1)[:, -Pb:]
            new_bv = jnp.concatenate([past[3], vf.reshape(B, T, B_HEADS, B_HEAD_DIM)], axis=1)[:, -Pb:]

        wq, sk1, sk2, u, vt = peer_w[1]
        x3, h4, rank, cnt, e2, r = _post_mix(ob, x2, sl(8), sl(9), sl(10), norm_ffn[1], bwo, wq, sk1, sk2, tbr)
        y = _peer_dense(h4, u, vt, rank, cnt, e2, r, x3, sl(11), tbe)
        return y.reshape(B, T, D), ckv[None], kr[None], new_bk, new_bv

    pos_p = jnp.arange(Tp, dtype=jnp.int32)
    pos_s = jnp.tile(P + jnp.arange(Ts, dtype=jnp.int32), Bs)
    y_p, p_ckv, p_kr, p_bk, p_bv = run(x_prompt, mod[:Bp], False, pos_p, None)
    y_s, s_ckv, s_kr, s_bk, s_bv = run(x_sample, mod[Bp:], True, pos_s,
                                       (cache_a_ckv[0], cache_a_krope[0], cache_b_k, cache_b_v))
    return (y_p, y_s, p_ckv, p_kr, p_bk, p_bv, s_ckv, s_kr, s_bk, s_bv)
```

```python
import functools
import math

import numpy as np
import jax
import jax.numpy as jnp
from jax import lax
from jax.experimental import pallas as pl
from jax.experimental.pallas import tpu as pltpu

F32 = jnp.float32
BF16 = jnp.bfloat16

D_MODEL = 1024
CHUNK = 64
A_HEADS = 16
Q_LORA = 384
KV_LORA = 256
NOPE_DIM = 64
ROPE_DIM = 32
V_DIM = 64
QK_DIM = NOPE_DIM + ROPE_DIM
ROPE_THETA = 10000.0
B_HEADS = 16
B_HEAD_DIM = 64
LEFT_CHUNKS = 8
B_WINDOW = LEFT_CHUNKS * CHUNK
REL_CLIP = 128
P_HEADS = 8
N_KEYS = 128
N_EXPERTS = N_KEYS * N_KEYS
P_HALF = 64
P_TOPK = 16
NEG = -1e30
EPS = 1e-6

LANES = 128
BF16_ROWS = 16
HEAD_PAD = 128
BAND_SUB = 2 * CHUNK
BAND_WIN = B_WINDOW + BAND_SUB
VMEM_LIMIT = 56 * 1024 * 1024
PEER_EC = 512
PEER_SPLIT = 2
MXU_N = 256
GATE_LANES = 256

_NT = (((1,), (1,)), ((), ()))


def _cparams(sem):
    return pltpu.CompilerParams(dimension_semantics=sem, vmem_limit_bytes=VMEM_LIMIT)


def _modulate(x, w, shift, scale):
    ms = jnp.mean(x * x, axis=-1, keepdims=True)
    y = x * lax.rsqrt(ms + EPS) * w
    return y * (1.0 + scale) + shift


def _rms(x, w):
    ms = jnp.mean(x * x, axis=-1, keepdims=True)
    return x * lax.rsqrt(ms + EPS) * w


def _mod_spec(arr, tb):
    if arr.shape[1] == 1:
        return pl.BlockSpec((1, 1, arr.shape[2]), lambda b, i, *_: (b, 0, 0))
    return pl.BlockSpec((1, tb, arr.shape[2]), lambda b, i, *_: (b, i, 0))


def _full_spec(arr):
    nd = arr.ndim
    return pl.BlockSpec(arr.shape, lambda *_: (0,) * nd)


def _ada_kernel(c_ref, w_ref, b_ref, o_ref):
    c = c_ref[...]
    a = c / (1.0 + jnp.exp(-c))
    o_ref[...] = jnp.dot(a.astype(BF16), w_ref[...], preferred_element_type=F32) + b_ref[...]


def _ada(c, w, b, tn=2048):
    m, k = c.shape
    n = w.shape[1]
    return pl.pallas_call(
        _ada_kernel,
        grid=(n // tn,),
        in_specs=[pl.BlockSpec((m, k), lambda j: (0, 0)),
                  pl.BlockSpec((k, tn), lambda j: (0, j)),
                  pl.BlockSpec((1, tn), lambda j: (0, j))],
        out_specs=pl.BlockSpec((m, tn), lambda j: (0, j)),
        out_shape=jax.ShapeDtypeStruct((m, n), F32),
        compiler_params=_cparams(("arbitrary",)),
        name="ada_mod",
    )(c, w, b)


def _rope_head(xh, cos, sin, lane):
    rolled = jnp.where(lane < NOPE_DIM + ROPE_DIM // 2,
                       pltpu.roll(xh, LANES - ROPE_DIM // 2, 1),
                       pltpu.roll(xh, ROPE_DIM // 2, 1))
    return xh * cos + rolled * sin


def _mla_pre_kernel(x_ref, sh_ref, sc_ref, nw_ref, win_ref, qln_ref, kvln_ref, wuq_ref, qn_ref,
                    cos_ref, sin_ref, ckv_ref, kr_ref, q_ref):
    x = x_ref[0]
    h = _modulate(x, nw_ref[...], sh_ref[0], sc_ref[0])
    proj = jnp.dot(h.astype(BF16), win_ref[...], preferred_element_type=F32)
    c_q = _rms(proj[:, :Q_LORA], qln_ref[...])
    ckv_ref[0] = _rms(proj[:, Q_LORA:Q_LORA + KV_LORA], kvln_ref[...])
    kr_ref[0] = proj[:, Q_LORA + KV_LORA:]
    q = jnp.dot(c_q.astype(BF16), wuq_ref[...], preferred_element_type=F32)
    cos = cos_ref[...]
    sin = sin_ref[...]
    qn = qn_ref[...]
    lane = lax.broadcasted_iota(jnp.int32, cos.shape, 1)
    scale = QK_DIM ** -0.5
    for hd in range(A_HEADS):
        qh = q[:, hd * HEAD_PAD:(hd + 1) * HEAD_PAD]
        ss = jnp.sum(qh * qh, axis=-1, keepdims=True) * (1.0 / QK_DIM)
        qh = qh * lax.rsqrt(ss + EPS) * qn
        qh = _rope_head(qh, cos, sin, lane)
        q_ref[0, :, hd * HEAD_PAD:(hd + 1) * HEAD_PAD] = (qh * scale).astype(BF16)


def _mla_pre(x, shift, scale, nw, w_in, qln, kvln, wuq_pad, qn_pad, cos_t, sin_t, tb):
    B, T, D = x.shape
    nT = T // tb
    tab_spec = pl.BlockSpec((tb, LANES), lambda b, i: (i, 0))
    return pl.pallas_call(
        _mla_pre_kernel,
        grid=(B, nT),
        in_specs=[pl.BlockSpec((1, tb, D), lambda b, i: (b, i, 0)),
                  _mod_spec(shift, tb), _mod_spec(scale, tb),
                  _full_spec(nw), _full_spec(w_in), _full_spec(qln), _full_spec(kvln),
                  _full_spec(wuq_pad), _full_spec(qn_pad), tab_spec, tab_spec],
        out_specs=[pl.BlockSpec((1, tb, KV_LORA), lambda b, i: (b, i, 0)),
                   pl.BlockSpec((1, tb, ROPE_DIM), lambda b, i: (b, i, 0)),
                   pl.BlockSpec((1, tb, A_HEADS * HEAD_PAD), lambda b, i: (b, i, 0))],
        out_shape=[jax.ShapeDtypeStruct((B, T, KV_LORA), F32),
                   jax.ShapeDtypeStruct((B, T, ROPE_DIM), F32),
                   jax.ShapeDtypeStruct((B, T, A_HEADS * HEAD_PAD), BF16)],
        compiler_params=_cparams(("parallel", "parallel")),
        name="mla_pre",
    )(x, shift, scale, nw, w_in, qln, kvln, wuq_pad, qn_pad, cos_t, sin_t)


def _mla_kv_kernel(ckv_ref, kr_ref, wuk_ref, wuv_ref, vone_ref, kn_ref, place_ref, cos_ref, sin_ref,
                   k_ref, v_ref):
    ckv = ckv_ref[0].astype(BF16)
    kn = jnp.dot(ckv, wuk_ref[...], preferred_element_type=F32)
    v_ref[0] = (jnp.dot(ckv, wuv_ref[...], preferred_element_type=F32) + vone_ref[...]).astype(BF16)
    kr = kr_ref[0]
    kr_hi = kr.astype(BF16)
    kr_lo = (kr - kr_hi.astype(F32)).astype(BF16)
    place = place_ref[...]
    krp = (jnp.dot(kr_hi, place, preferred_element_type=F32)
           + jnp.dot(kr_lo, place, preferred_element_type=F32))
    cos = cos_ref[...]
    sin = sin_ref[...]
    knw = kn_ref[...]
    lane = lax.broadcasted_iota(jnp.int32, cos.shape, 1)
    for hd in range(A_HEADS):
        kh = kn[:, hd * HEAD_PAD:(hd + 1) * HEAD_PAD] + krp
        ss = jnp.sum(kh * kh, axis=-1, keepdims=True) * (1.0 / QK_DIM)
        kh = kh * lax.rsqrt(ss + EPS) * knw
        kh = _rope_head(kh, cos, sin, lane)
        k_ref[0, :, hd * HEAD_PAD:(hd + 1) * HEAD_PAD] = kh.astype(BF16)


def _mla_kv(ckv, kr, wuk_pad, wuv_pad, vone, kn_pad, place, cos_t, sin_t, tb):
    B, S, _ = ckv.shape
    tab_spec = pl.BlockSpec((tb, LANES), lambda b, i: (i, 0))
    wide = pl.BlockSpec((1, tb, A_HEADS * HEAD_PAD), lambda b, i: (b, i, 0))
    return pl.pallas_call(
        _mla_kv_kernel,
        grid=(B, S // tb),
        in_specs=[pl.BlockSpec((1, tb, KV_LORA), lambda b, i: (b, i, 0)),
                  pl.BlockSpec((1, tb, ROPE_DIM), lambda b, i: (b, i, 0)),
                  _full_spec(wuk_pad), _full_spec(wuv_pad), _full_spec(vone), _full_spec(kn_pad),
                  _full_spec(place), tab_spec, tab_spec],
        out_specs=[wide, wide],
        out_shape=[jax.ShapeDtypeStruct((B, S, A_HEADS * HEAD_PAD), BF16),
                   jax.ShapeDtypeStruct((B, S, A_HEADS * HEAD_PAD), BF16)],
        compiler_params=_cparams(("parallel", "parallel")),
        name="mla_kv",
    )(ckv, kr, wuk_pad, wuv_pad, vone, kn_pad, place, cos_t, sin_t)


def _mla_attn_kernel(q_ref, k_ref, v_ref, o_ref, m_sc, acc_sc, *, tq, tk, causal, nk):
    i = pl.program_id(2)
    m_sc[...] = jnp.full(m_sc.shape, -jnp.inf, F32)
    acc_sc[...] = jnp.zeros(acc_sc.shape, F32)

    def block(start, masked):
        k = k_ref[0, pl.ds(start, tk), :]
        v = v_ref[0, pl.ds(start, tk), :]
        if masked:
            qc = lax.broadcasted_iota(jnp.int32, (tq, tk), 0) // CHUNK
            kc = lax.broadcasted_iota(jnp.int32, (tq, tk), 1) // CHUNK
            allowed = kc <= qc
        for hh in range(2):
            qh = q_ref[0, :, hh * HEAD_PAD:(hh + 1) * HEAD_PAD]
            s = lax.dot_general(qh, k[:, hh * HEAD_PAD:(hh + 1) * HEAD_PAD], _NT,
                                preferred_element_type=F32)
            if masked:
                s = jnp.where(allowed, s, NEG)
            m_prev = m_sc[hh]
            m_new = jnp.maximum(m_prev, jnp.max(s, axis=-1, keepdims=True))
            alpha = jnp.exp(m_prev - m_new)
            if tk % LANES == 0:
                m_wide = jnp.concatenate([m_new] * (tk // LANES), axis=1)
            else:
                m_wide = m_new[:, :1]
            p = jnp.exp((s - m_wide).astype(BF16))
            acc_sc[hh] = alpha * acc_sc[hh] + jnp.dot(p, v[:, hh * HEAD_PAD:(hh + 1) * HEAD_PAD],
                                                      preferred_element_type=F32)
            m_sc[hh] = m_new

    def body(j, carry):
        block(pl.multiple_of(j * tk, tk), False)
        return carry

    if causal:
        lax.fori_loop(0, i, body, 0)
        block(pl.multiple_of(i * tk, tk), True)
    else:
        lax.fori_loop(0, nk, body, 0)

    lane = lax.broadcasted_iota(jnp.int32, (tq, LANES), 1)
    outs = []
    for hh in range(2):
        acc = acc_sc[hh]
        outs.append(acc / acc[:, V_DIM:V_DIM + 1])
    o = jnp.where(lane < V_DIM, outs[0], pltpu.roll(outs[1], V_DIM, 1))
    o_ref[0] = o.astype(BF16)


def _mla_attn(q, k, v, tq, tk, causal):
    B, T, _ = q.shape
    S = k.shape[1]
    nq, nk = T // tq, S // tk
    assert T % tq == 0 and S % tk == 0 and (not causal or (tq == tk and T == S))
    kv_spec = pl.BlockSpec((1, S, 2 * HEAD_PAD), lambda b, hp, i: (b, 0, hp))
    return pl.pallas_call(
        functools.partial(_mla_attn_kernel, tq=tq, tk=tk, causal=causal, nk=nk),
        grid=(B, A_HEADS // 2, nq),
        in_specs=[pl.BlockSpec((1, tq, 2 * HEAD_PAD), lambda b, hp, i: (b, i, hp)), kv_spec, kv_spec],
        out_specs=pl.BlockSpec((1, tq, 2 * V_DIM), lambda b, hp, i: (b, i, hp)),
        out_shape=jax.ShapeDtypeStruct((B, T, A_HEADS * V_DIM), BF16),
        scratch_shapes=[pltpu.VMEM((2, tq, LANES), F32), pltpu.VMEM((2, tq, LANES), F32)],
        compiler_params=_cparams(("parallel", "parallel", "arbitrary")),
        name="mla_attn",
    )(q, k, v)


def _odd_even_merge_sort_pairs(n):
    pairs = []

    def merge(lo, m, r):
        step = r * 2
        if step < m:
            merge(lo, m, step)
            merge(lo + r, m, step)
            pairs.extend((i, i + r) for i in range(lo + r, lo + m - r, step))
        else:
            pairs.append((lo, lo + r))

    def sort(lo, m):
        if m > 1:
            sort(lo, m // 2)
            sort(lo + m // 2, m // 2)
            merge(lo, m, 1)

    sort(0, n)
    return pairs


_SORT16 = _odd_even_merge_sort_pairs(P_TOPK)
_BITONIC16 = [(i, i + d) for d in (8, 4, 2, 1) for i in range(P_TOPK) if not i & d]
SUBLANES = 8


def _top16_rows(chains):
    assert all(s.shape[0] == P_TOPK * SUBLANES for s, _, _ in chains)
    vs = [[s[SUBLANES * j:SUBLANES * (j + 1), :] for j in range(P_TOPK)] for s, _, _ in chains]

    def compare_exchange(pairs):
        for i, j in pairs:
            for v in vs:
                v[i], v[j] = jnp.maximum(v[i], v[j]), jnp.minimum(v[i], v[j])

    compare_exchange(_SORT16)
    for shift in (4, 2, 1):
        for v in vs:
            other = [pltpu.roll(x, shift, 0) for x in v]
            v[:] = [jnp.maximum(v[k], other[P_TOPK - 1 - k]) for k in range(P_TOPK)]
        compare_exchange(_BITONIC16)
    for v, (_, t_sc, cols) in zip(vs, chains):
        for k in range(P_TOPK):
            t_sc[k:k + 1, cols] = v[k][0:1, :]


def _prefix_count(test, rows):
    assert len(rows) == P_TOPK
    m = [test(rows[4 * q + 3]) for q in range(4)]
    base = jnp.where(m[0], 4.0, 0.0)
    for q in range(1, 4):
        base = jnp.where(m[q], 4.0 * (q + 1), base)
    sub = jnp.zeros_like(base)
    for j in range(3):
        t_sel = jnp.where(m[2], rows[12 + j], jnp.where(m[1], rows[8 + j], jnp.where(m[0], rows[4 + j], rows[j])))
        sub = jnp.where(test(t_sel), float(j + 1), sub)
    return jnp.minimum(base + sub, float(P_TOPK))


def _route_tiles(tiles, t1_sc, t2_sc):
    _top16_rows([(s1, t1_sc, cols) for s1, _, cols in tiles] + [(s2, t2_sc, cols) for _, s2, cols in tiles])
    state = []
    for s1, s2, cols in tiles:
        row8 = lax.broadcasted_iota(jnp.int32, (8, s1.shape[1]), 0)
        t1 = [t1_sc[k:k + 1, cols] for k in range(P_TOPK)]
        t2 = [t2_sc[k:k + 1, cols] for k in range(P_TOPK)]
        t2_lo = t2_sc[0:8, cols]
        t2_hi = t2_sc[8:16, cols]
        t1_hi = t1_sc[8:16, cols]
        cands = [t1[0] + t2_lo, t1[0] + t2_hi, t1_hi + t2[0]]
        for k1 in range(1, 8):
            lim = P_TOPK // (k1 + 1)
            c = t1[k1] + t2_lo
            cands.append(c if lim >= 8 else jnp.where(row8 < lim, c, -jnp.inf))
        top = t1[0] + t2[0]
        state.append(dict(t1=t1, t2=t2, cands=cands, top=top, z=jnp.zeros_like(top), tau=top))
    for k in range(P_TOPK):
        for st in state:
            m = st["cands"][0]
            for c in st["cands"][1:]:
                m = jnp.maximum(m, c)
            m = jnp.max(m, axis=0, keepdims=True)
            st["z"] = st["z"] + jnp.exp(m - st["top"])
            st["tau"] = m
            if k + 1 < P_TOPK:
                st["cands"] = [jnp.where(c == m, -jnp.inf, c) for c in st["cands"]]
    outs = []
    for (s1, s2, cols), st in zip(tiles, state):
        tau = st["tau"]
        cnt = _prefix_count(lambda t: s1 + t >= tau, st["t2"])
        rank = _prefix_count(lambda t: t > s2, st["t2"])
        n1 = jnp.sum(jnp.where(s1 >= st["t1"][P_TOPK - 1], 1.0, 0.0), axis=0, keepdims=True)
        n2 = jnp.sum(jnp.where(s2 >= st["t2"][P_TOPK - 1], 1.0, 0.0), axis=0, keepdims=True)
        npair = jnp.sum(cnt, axis=0, keepdims=True)
        k = float(P_TOPK)
        tied = jnp.where((n1 != k) | (n2 != k) | (npair != k), 1.0, 0.0)
        outs.append((rank, cnt, jnp.exp(s2 - st["t2"][0]), jnp.exp(s1 - st["t1"][0]) / st["z"], tied))
    return outs


def _route_tile_exact(s1, s2, t1_sc, t2_sc, cols):
    nk, w = s1.shape
    key = lax.broadcasted_iota(jnp.int32, (nk, w), 0).astype(F32)

    def top16(s, t_sc):
        work, rank, tops = s, jnp.full(s.shape, float(P_TOPK), F32), []
        for k in range(P_TOPK):
            m = jnp.max(work, axis=0, keepdims=True)
            first = jnp.min(jnp.where(work == m, key, float(nk)), axis=0, keepdims=True)
            hit = key == first
            rank = jnp.where(hit, float(k), rank)
            work = jnp.where(hit, -jnp.inf, work)
            t_sc[k:k + 1, cols] = m
            tops.append(m)
        return tops, rank

    t1, rank1 = top16(s1, t1_sc)
    t2, rank2 = top16(s2, t2_sc)
    row8 = lax.broadcasted_iota(jnp.int32, (8, w), 0).astype(F32)
    t2_lo = t2_sc[0:8, cols]
    t2_hi = t2_sc[8:16, cols]
    t1_hi = t1_sc[8:16, cols]
    cands = [(t1[0] + t2_lo, row8), (t1[0] + t2_hi, row8 + 8.0), (t1_hi + t2[0], (row8 + 8.0) * P_TOPK)]
    for k1 in range(1, 8):
        lim = P_TOPK // (k1 + 1)
        c = t1[k1] + t2_lo
        cands.append((c if lim >= 8 else jnp.where(row8 < lim, c, -jnp.inf), row8 + float(k1 * P_TOPK)))
    top = t1[0] + t2[0]
    z = jnp.zeros_like(top)
    k1_row = lax.broadcasted_iota(jnp.int32, (P_TOPK, w), 0).astype(F32)
    cnt_k1 = jnp.zeros((P_TOPK, w), F32)
    for k in range(P_TOPK):
        m = cands[0][0]
        for c, _ in cands[1:]:
            m = jnp.maximum(m, c)
        m = jnp.max(m, axis=0, keepdims=True)
        first = None
        for c, pos in cands:
            f = jnp.min(jnp.where(c == m, pos, float(P_TOPK * P_TOPK)), axis=0, keepdims=True)
            first = f if first is None else jnp.minimum(first, f)
        z = z + jnp.exp(m - top)
        cnt_k1 = cnt_k1 + jnp.where(k1_row == jnp.floor(first * (1.0 / P_TOPK)), 1.0, 0.0)
        cands = [(jnp.where(pos == first, -jnp.inf, c), pos) for c, pos in cands]
    t1_sc[:, cols] = cnt_k1
    cnt = jnp.zeros_like(s1)
    for k1 in range(P_TOPK):
        cnt = jnp.where(rank1 == float(k1), t1_sc[k1:k1 + 1, cols], cnt)
    return rank2, cnt, jnp.exp(s1 - t1[0]) / z


def _post_mix_kernel(o_ref, x_ref, g_ref, sh_ref, sc_ref, nw_ref, wo_ref, wq_ref, sk1_ref, sk2_ref,
                     x1_ref, h2_ref, rank_ref, cnt_ref, e2_ref, r_ref, t1_sc, t2_sc, s_sc, tied_sc):
    mix = jnp.dot(o_ref[0], wo_ref[...], preferred_element_type=F32)
    x1 = x_ref[0] + g_ref[0] * mix
    x1_ref[0] = x1
    h2 = _modulate(x1, nw_ref[...], sh_ref[0], sc_ref[0]).astype(BF16)
    h2_ref[0] = h2
    tb = h2.shape[0]
    q = jnp.dot(h2, wq_ref[...], preferred_element_type=F32).astype(BF16)
    for h in range(P_HEADS):
        qh = q[:, h * 2 * P_HALF:(h + 1) * 2 * P_HALF]
        s_sc[h, 0] = lax.dot_general(sk1_ref[h], qh, _NT, preferred_element_type=F32)
        s_sc[h, 1] = lax.dot_general(sk2_ref[h], qh, _NT, preferred_element_type=F32)

    def head_body(h, carry):
        s1 = s_sc[h, 0]
        s2 = s_sc[h, 1]
        lane_tiles = [slice(nt * LANES, (nt + 1) * LANES) for nt in range(tb // LANES)]
        routed = _route_tiles([(s1[:, cols], s2[:, cols], cols) for cols in lane_tiles], t1_sc, t2_sc)
        for cols, (rank, cnt, e2, r, tied) in zip(lane_tiles, routed):
            rank_ref[h, :, cols] = rank.astype(BF16)
            cnt_ref[h, :, cols] = cnt
            e2_ref[h, :, cols] = e2.astype(BF16)
            r_ref[h, :, cols] = r
            tied_sc[h, :, cols] = jnp.broadcast_to(tied, (8, tied.shape[1]))
        return carry

    lax.fori_loop(0, P_HEADS, head_body, 0)

    @pl.when(jnp.max(tied_sc[...]) > 0.0)
    def _():
        def fix_head(h, carry):
            for nt in range(tb // LANES):
                cols = slice(nt * LANES, (nt + 1) * LANES)

                @pl.when(jnp.max(tied_sc[h, :, cols]) > 0.0)
                def _(cols=cols):
                    rank_x, cnt_x, r_x = _route_tile_exact(s_sc[h, 0, :, cols], s_sc[h, 1, :, cols],
                                                           t1_sc, t2_sc, cols)
                    rank_ref[h, :, cols] = rank_x.astype(BF16)
                    cnt_ref[h, :, cols] = cnt_x
                    r_ref[h, :, cols] = r_x
            return carry

        lax.fori_loop(0, P_HEADS, fix_head, 0)


def _post_mix(o, x, gate, shift, scale, nw, w_o, wq_heads, sk1, sk2, tb):
    B, T, D = x.shape
    nT = T // tb
    n = B * T
    tok = lambda b, i: (b, i, 0)
    rt_spec = pl.BlockSpec((P_HEADS, N_KEYS, tb), lambda b, i: (0, 0, b * nT + i))
    rt_shape = jax.ShapeDtypeStruct((P_HEADS, N_KEYS, n), F32)
    rt_shape_b = jax.ShapeDtypeStruct((P_HEADS, N_KEYS, n), BF16)
    return pl.pallas_call(
        _post_mix_kernel,
        grid=(B, nT),
        in_specs=[pl.BlockSpec((1, tb, o.shape[2]), tok), pl.BlockSpec((1, tb, D), tok),
                  _mod_spec(gate, tb), _mod_spec(shift, tb), _mod_spec(scale, tb),
                  _full_spec(nw), _full_spec(w_o), _full_spec(wq_heads), _full_spec(sk1), _full_spec(sk2)],
        out_specs=[pl.BlockSpec((1, tb, D), tok), pl.BlockSpec((1, tb, D), tok),
                   rt_spec, rt_spec, rt_spec, rt_spec],
        out_shape=[jax.ShapeDtypeStruct((B, T, D), F32), jax.ShapeDtypeStruct((B, T, D), BF16),
                   rt_shape_b, rt_shape, rt_shape_b, rt_shape],
        scratch_shapes=[pltpu.VMEM((P_TOPK, tb), F32), pltpu.VMEM((P_TOPK, tb), F32),
                        pltpu.VMEM((P_HEADS, 2, N_KEYS, tb), F32), pltpu.VMEM((P_HEADS, 8, tb), F32)],
        compiler_params=_cparams(("parallel", "parallel")),
        name="post_mix_route",
    )(o, x, gate, shift, scale, nw, w_o, wq_heads, sk1, sk2)


def _gelu_tanh(a):
    c = math.sqrt(2.0 / math.pi)
    return (0.5 * a) * (1.0 + jnp.tanh(a * ((0.044715 * c) * (a * a) + c)))


def _peer_dense_kernel(h_ref, u_ref, vt_ref, rank_ref, cnt_ref, e2_ref, r_ref, x_ref, g_ref,
                       o_ref, acc_ref, a0_ref, a1_ref, hc0_ref, hc1_ref, *, ec, ne):
    g = pl.program_id(0)
    per = ec // N_KEYS
    s = lax.rem(jnp.maximum(g - 1, 0), ne) + 1
    c3 = lax.rem(jnp.maximum(g - 2, 0), ne)

    @pl.when(g == 0)
    def _():
        for ref in (a0_ref, a1_ref, hc0_ref, hc1_ref):
            ref[...] = jnp.zeros(ref.shape, ref.dtype)

    @pl.when(c3 == 0)
    def _():
        acc_ref[...] = jnp.zeros(acc_ref.shape, F32)

    def stages(a_out, a_in, hc_out, hc_in):
        zero = jnp.zeros((), BF16)
        tb = h_ref.shape[1]
        nsplit = PEER_SPLIT
        gate_w = min(GATE_LANES, tb)
        mxu_w = min(MXU_N, tb)
        assert tb % gate_w == 0 and tb % mxu_w == 0

        row_cache = {}
        reps = N_KEYS // BF16_ROWS

        def routed_rows(ii):
            if ii not in row_cache:
                i1 = jnp.clip((s - 1) * per + ii, 0, N_KEYS - 1)
                wide = lambda ref, h: jnp.broadcast_to(ref[h, pl.ds(i1, 1), :], (BF16_ROWS, tb)).astype(BF16)
                row_cache[ii] = [(wide(cnt_ref, h), wide(r_ref, h)) for h in range(P_HEADS)]
            return row_cache[ii]

        def gate_piece(ii, n):
            cols = slice(n * gate_w, (n + 1) * gate_w)
            rows = slice(ii * N_KEYS, (ii + 1) * N_KEYS)
            w = None
            for h, (cnt_row, r_row) in enumerate(routed_rows(ii)):
                cnt_t = jnp.concatenate([cnt_row[:, cols]] * reps, axis=0)
                r_t = jnp.concatenate([r_row[:, cols]] * reps, axis=0)
                contrib = jnp.where(rank_ref[h, :, cols] < cnt_t, e2_ref[h, :, cols], zero) * r_t
                w = contrib if w is None else w + contrib
            hc_out[rows, cols] = _gelu_tanh(a_in[rows, cols]) * w

        def score_piece(q, n):
            cols = slice(n * mxu_w, (n + 1) * mxu_w)
            rows = slice(q * (ec // nsplit), (q + 1) * (ec // nsplit))
            a_out[rows, cols] = lax.dot_general(u_ref[rows, :], h_ref[0, cols, :], _NT,
                                                preferred_element_type=F32).astype(BF16)

        def out_piece(q, n):
            cols = slice(n * mxu_w, (n + 1) * mxu_w)
            rows = slice(q * (D_MODEL // nsplit), (q + 1) * (D_MODEL // nsplit))
            acc_ref[rows, cols] += jnp.dot(vt_ref[0, rows, :], hc_in[:, cols], preferred_element_type=F32)

        vec = [functools.partial(gate_piece, ii, n) for n in range(tb // gate_w) for ii in range(per)]
        mxu = []
        for n in range(tb // mxu_w):
            for q in range(nsplit):
                mxu.append(functools.partial(score_piece, q, n))
                mxu.append(functools.partial(out_piece, q, n))
        for k in range(max(len(vec), len(mxu))):
            if k < len(mxu):
                mxu[k]()
            if k < len(vec):
                vec[k]()

    even = lax.rem(g, 2) == 0
    pl.when(even)(lambda: stages(a0_ref, a1_ref, hc1_ref, hc0_ref))
    pl.when(jnp.logical_not(even))(lambda: stages(a1_ref, a0_ref, hc0_ref, hc1_ref))

    @pl.when((c3 == ne - 1) & (g >= 2))
    def _():
        o_ref[0] = x_ref[0] + g_ref[0] * acc_ref[...].T


def _peer_dense(h2, u, vt, rank, cnt, e2, r, x, gate, tb, ec=PEER_EC):
    B, T, D = x.shape
    nT = T // tb
    nblk = B * nT
    ne = N_EXPERTS // ec
    assert vt.shape == (ne, D, ec)
    blk1 = lambda g: jnp.minimum(g // ne, nblk - 1)
    blk2 = lambda g: jnp.minimum(jnp.maximum(g - 1, 0) // ne, nblk - 1)
    blk3 = lambda g: jnp.maximum(g - 2, 0) // ne
    tok1 = lambda g: (blk1(g) // nT, blk1(g) % nT, 0)
    tok3 = lambda g: (blk3(g) // nT, blk3(g) % nT, 0)
    rt_spec = pl.BlockSpec((P_HEADS, N_KEYS, tb), lambda g: (0, 0, blk2(g)))
    if gate.shape[1] == 1:
        gate_spec = pl.BlockSpec((1, 1, D), lambda g: (blk3(g) // nT, 0, 0))
    else:
        gate_spec = pl.BlockSpec((1, tb, D), tok3)
    return pl.pallas_call(
        functools.partial(_peer_dense_kernel, ec=ec, ne=ne),
        grid=(nblk * ne + 2,),
        in_specs=[pl.BlockSpec((1, tb, D), tok1),
                  pl.BlockSpec((ec, D), lambda g: (g % ne, 0)),
                  pl.BlockSpec((1, D, ec), lambda g: (jnp.maximum(g - 2, 0) % ne, 0, 0)),
                  rt_spec, rt_spec, rt_spec, rt_spec,
                  pl.BlockSpec((1, tb, D), tok3), gate_spec],
        out_specs=pl.BlockSpec((1, tb, D), tok3),
        out_shape=jax.ShapeDtypeStruct((B, T, D), F32),
        scratch_shapes=[pltpu.VMEM((D, tb), F32),
                        pltpu.VMEM((ec, tb), BF16), pltpu.VMEM((ec, tb), BF16),
                        pltpu.VMEM((ec, tb), BF16), pltpu.VMEM((ec, tb), BF16)],
        compiler_params=_cparams(("arbitrary",)),
        name="peer_dense",
    )(h2, u, vt, rank, cnt, e2, r, x, gate)


def _pair_rms(y, w, lane):
    y2 = y * y
    lo = jnp.sum(jnp.where(lane < B_HEAD_DIM, y2, 0.0), axis=-1, keepdims=True)
    hi = jnp.sum(y2, axis=-1, keepdims=True) - lo
    ms = jnp.where(lane < B_HEAD_DIM, lo, hi) * (1.0 / B_HEAD_DIM)
    return y * lax.rsqrt(ms + EPS) * w


def _shared_kv_kernel(x_ref, sh_ref, sc_ref, nw_ref, wk_ref, wv_ref, kn_ref, k_ref, v_ref, kb_ref, vb_ref,
                      *, first_tail):
    h = _modulate(x_ref[0], nw_ref[...], sh_ref[0], sc_ref[0]).astype(BF16)
    kraw = jnp.dot(h, wk_ref[...], preferred_element_type=F32)
    v = jnp.dot(h, wv_ref[...], preferred_element_type=F32)
    vb_ref[0] = v.astype(BF16)
    knw = kn_ref[...]
    lane = lax.broadcasted_iota(jnp.int32, (h.shape[0], LANES), 1)
    ks = [_pair_rms(kraw[:, hp * LANES:(hp + 1) * LANES], knw, lane) for hp in range(B_HEADS // 2)]
    for hp, kh in enumerate(ks):
        kb_ref[0, :, hp * LANES:(hp + 1) * LANES] = kh.astype(BF16)

    @pl.when(pl.program_id(1) >= first_tail)
    def _():
        v_ref[0] = v
        for hp, kh in enumerate(ks):
            k_ref[0, :, hp * LANES:(hp + 1) * LANES] = kh


def _shared_kv(x, shift, scale, nw, wk, wv, kn_pair, tb, tail):
    B, T, D = x.shape
    nT = T // tb
    assert tail % tb == 0 and tail <= T
    first_tail = nT - tail // tb
    tok = lambda b, i: (b, i, 0)
    blk = pl.BlockSpec((1, tb, D), tok)
    tail_blk = pl.BlockSpec((1, tb, D), lambda b, i: (b, jnp.maximum(i - first_tail, 0), 0))
    return pl.pallas_call(
        functools.partial(_shared_kv_kernel, first_tail=first_tail),
        grid=(B, nT),
        in_specs=[blk, _mod_spec(shift, tb), _mod_spec(scale, tb), _full_spec(nw),
                  _full_spec(wk), _full_spec(wv), _full_spec(kn_pair)],
        out_specs=[tail_blk, tail_blk, blk, blk],
        out_shape=[jax.ShapeDtypeStruct((B, tail, D), F32), jax.ShapeDtypeStruct((B, tail, D), F32),
                   jax.ShapeDtypeStruct((B, T, D), BF16), jax.ShapeDtypeStruct((B, T, D), BF16)],
        compiler_params=_cparams(("parallel", "arbitrary")),
        name="shared_kv",
    )(x, shift, scale, nw, wk, wv, kn_pair)


def _band_pre_kernel(x_ref, sh_ref, sc_ref, nw_ref, wq_ref, qn_ref, q_ref):
    h = _modulate(x_ref[0], nw_ref[...], sh_ref[0], sc_ref[0]).astype(BF16)
    q = jnp.dot(h, wq_ref[...], preferred_element_type=F32)
    qnw = qn_ref[...]
    lane = lax.broadcasted_iota(jnp.int32, (h.shape[0], LANES), 1)
    scale = B_HEAD_DIM ** -0.5
    for hp in range(B_HEADS // 2):
        qh = _pair_rms(q[:, hp * LANES:(hp + 1) * LANES], qnw, lane)
        q_ref[0, :, hp * LANES:(hp + 1) * LANES] = (qh * scale).astype(BF16)


def _band_pre(x, shift, scale, nw, wq, qn_pair, tb):
    B, T, D = x.shape
    tok = lambda b, i: (b, i, 0)
    blk = pl.BlockSpec((1, tb, D), tok)
    return pl.pallas_call(
        _band_pre_kernel,
        grid=(B, T // tb),
        in_specs=[blk, _mod_spec(shift, tb), _mod_spec(scale, tb), _full_spec(nw),
                  _full_spec(wq), _full_spec(qn_pair)],
        out_specs=blk,
        out_shape=jax.ShapeDtypeStruct((B, T, D), BF16),
        compiler_params=_cparams(("parallel", "parallel")),
        name="band_pre",
    )(x, shift, scale, nw, wq, qn_pair)


def _band_bias_kernel(tab_ref, o_ref):
    h = pl.program_id(0)
    nvar = 2 * LANES
    r = lax.broadcasted_iota(jnp.int32, (BAND_SUB, nvar), 0)
    w = lax.broadcasted_iota(jnp.int32, (BAND_SUB, nvar), 1) + (BAND_WIN - nvar)
    idx = jnp.clip(r + B_WINDOW - w, -REL_CLIP, REL_CLIP) + REL_CLIP
    far = tab_ref[h, 2 * REL_CLIP]

    def body(t, acc):
        return jnp.where(idx == t, tab_ref[h, t], acc)

    var = lax.fori_loop(0, 2 * REL_CLIP, body, jnp.full((BAND_SUB, nvar), far, F32))
    full = jnp.concatenate([jnp.full((BAND_SUB, BAND_WIN - nvar), far, F32), var], axis=1)
    rr = lax.broadcasted_iota(jnp.int32, (BAND_SUB, BAND_WIN), 0)
    ww = lax.broadcasted_iota(jnp.int32, (BAND_SUB, BAND_WIN), 1)
    qc = rr // CHUNK + LEFT_CHUNKS
    kc = ww // CHUNK
    allowed = (kc <= qc) & (kc >= qc - LEFT_CHUNKS)
    o_ref[0] = jnp.where(allowed, full, NEG)


def _band_bias(table):
    nh = table.shape[0]
    return pl.pallas_call(
        _band_bias_kernel,
        grid=(nh,),
        in_specs=[pl.BlockSpec(memory_space=pltpu.SMEM)],
        out_specs=pl.BlockSpec((1, BAND_SUB, BAND_WIN), lambda h: (h, 0, 0)),
        out_shape=jax.ShapeDtypeStruct((nh, BAND_SUB, BAND_WIN), F32),
        compiler_params=_cparams(("arbitrary",)),
        name="band_bias",
    )(table)


def _band_windows(windows, bias_ref, lane):
    chains = [(w, hh) for w in range(len(windows)) for hh in range(2)]
    scores = []
    for w, hh in chains:
        qs, kw, _ = windows[w]
        nkw = kw.shape[0]
        sel = (lane < B_HEAD_DIM) if hh == 0 else (lane >= B_HEAD_DIM)
        qh = jnp.where(sel, qs, jnp.zeros_like(qs))
        s = lax.dot_general(qh, kw, _NT, preferred_element_type=F32)
        scores.append(s + bias_ref[hh, :, BAND_WIN - nkw:BAND_WIN])
    maxes = [jnp.max(s, axis=-1, keepdims=True) for s in scores]
    probs = [jnp.exp((s - m).astype(BF16)) for s, m in zip(scores, maxes)]
    vext = []
    for _, _, vw in windows:
        ones_lane = (lax.broadcasted_iota(jnp.int32, vw.shape, 1) == 0).astype(BF16)
        vext.append(jnp.concatenate([vw, ones_lane], axis=1))
    outs = []
    for (w, hh), p in zip(chains, probs):
        pv = jnp.dot(p, vext[w], preferred_element_type=F32)
        outs.append(pv[:, :LANES] / pv[:, LANES:LANES + 1])
    return [jnp.where(lane < B_HEAD_DIM, outs[2 * w], outs[2 * w + 1]) for w in range(len(windows))]


def _band_attn_kernel(q_ref, kp_ref, kc_ref, vp_ref, vc_ref, bias_ref, o_ref, *, tq):
    i = pl.program_id(2)
    lane = lax.broadcasted_iota(jnp.int32, (BAND_SUB, LANES), 1)
    nsub = tq // BAND_SUB

    def run(first):
        windows = []
        for c in range(nsub):
            qs = q_ref[0, c * BAND_SUB:(c + 1) * BAND_SUB, :]
            hi = (c + 1) * BAND_SUB
            if first or hi >= BAND_WIN:
                lo = max(hi - BAND_WIN, 0)
                kw = kc_ref[0, lo:hi, :]
                vw = vc_ref[0, lo:hi, :]
            else:
                lo = tq - (BAND_WIN - hi)
                kw = jnp.concatenate([kp_ref[0, lo:tq, :], kc_ref[0, 0:hi, :]], axis=0)
                vw = jnp.concatenate([vp_ref[0, lo:tq, :], vc_ref[0, 0:hi, :]], axis=0)
            windows.append((qs, kw, vw))
        for c, o in enumerate(_band_windows(windows, bias_ref, lane)):
            o_ref[0, c * BAND_SUB:(c + 1) * BAND_SUB, :] = o.astype(BF16)

    pl.when(i == 0)(lambda: run(True))
    pl.when(i > 0)(lambda: run(False))


def _band_attn(q, k, v, bias, tq):
    B, T, D = q.shape
    assert tq >= B_WINDOW and T % tq == 0
    cur = lambda b, hp, i: (b, i, hp)
    prev = lambda b, hp, i: (b, jnp.maximum(i - 1, 0), hp)
    blk = lambda m: pl.BlockSpec((1, tq, LANES), m)
    return pl.pallas_call(
        functools.partial(_band_attn_kernel, tq=tq),
        grid=(B, B_HEADS // 2, T // tq),
        in_specs=[blk(cur), blk(prev), blk(cur), blk(prev), blk(cur),
                  pl.BlockSpec((2, BAND_SUB, BAND_WIN), lambda b, hp, i: (hp, 0, 0))],
        out_specs=blk(cur),
        out_shape=jax.ShapeDtypeStruct((B, T, D), BF16),
        compiler_params=_cparams(("parallel", "parallel", "arbitrary")),
        name="band_attn",
    )(q, k, k, v, v, bias)


def _band_step_kernel(q_ref, k_ref, v_ref, bias_ref, o_ref):
    lane = lax.broadcasted_iota(jnp.int32, (BAND_SUB, LANES), 1)
    o, = _band_windows([(q_ref[0], k_ref[0], v_ref[0])], bias_ref, lane)
    o_ref[0] = o.astype(BF16)


def _band_step(q, kwin, vwin, bias):
    B, _, D = q.shape
    return pl.pallas_call(
        _band_step_kernel,
        grid=(B, B_HEADS // 2),
        in_specs=[pl.BlockSpec((1, BAND_SUB, LANES), lambda b, hp: (b, 0, hp)),
                  pl.BlockSpec((1, BAND_WIN, LANES), lambda b, hp: (b, 0, hp)),
                  pl.BlockSpec((1, BAND_WIN, LANES), lambda b, hp: (b, 0, hp)),
                  pl.BlockSpec((2, BAND_SUB, BAND_WIN), lambda b, hp: (hp, 0, 0))],
        out_specs=pl.BlockSpec((1, BAND_SUB, LANES), lambda b, hp: (b, 0, hp)),
        out_shape=jax.ShapeDtypeStruct((B, BAND_SUB, D), BF16),
        compiler_params=_cparams(("parallel", "parallel")),
        name="band_step",
    )(q, kwin, vwin, bias)


def _rope_tables(pos):
    half = ROPE_DIM // 2
    freqs = ROPE_THETA ** (-jnp.arange(half, dtype=F32) / half)
    ang = pos.astype(F32)[:, None] * freqs[None, :]
    c, s = jnp.cos(ang), jnp.sin(ang)
    n = pos.shape[0]
    ones = jnp.ones((n, NOPE_DIM), F32)
    zeros = jnp.zeros((n, NOPE_DIM), F32)
    pad1 = jnp.ones((n, HEAD_PAD - QK_DIM), F32)
    pad0 = jnp.zeros((n, HEAD_PAD - QK_DIM), F32)
    return (jnp.concatenate([ones, c, c, pad1], axis=1),
            jnp.concatenate([zeros, -s, s, pad0], axis=1))


def _pad_heads(w, nheads, width):
    lead = w.shape[:-1]
    w = w.reshape(lead + (nheads, width))
    w = jnp.pad(w, [(0, 0)] * len(lead) + [(0, 0), (0, HEAD_PAD - width)])
    return w.reshape(lead + (nheads * HEAD_PAD,))


def _block(n, pref):
    for t in pref:
        if n % t == 0:
            return t
    return n


def kernel(x_prompt, x_sample, c_prompt, c_sample, cache_a_ckv, cache_a_krope, cache_b_k, cache_b_v, ada_w, ada_b, norm_mix_w, norm_ffn_w, a_w_in, a_q_lora_norm, a_kv_lora_norm, a_w_uq, a_w_ukv, a_q_norm, a_k_norm, a_w_o, kv_ada_w, kv_ada_b, kv_norm_w, b_w_kv, b_k_norm, b_w_q, b_q_norm, b_rel_bias, b_w_o, p_w_q, p_subkeys, p_u, p_v):
    D = D_MODEL
    Bp, Tp, _ = x_prompt.shape
    Bs, Ts, _ = x_sample.shape
    P = cache_a_ckv.shape[2]
    Pb = cache_b_k.shape[1]
    assert Ts == CHUNK and Pb == B_WINDOW and P % CHUNK == 0

    ada_all_w = jnp.concatenate([ada_w[0], ada_w[1], kv_ada_w], axis=1).astype(BF16)
    ada_all_b = jnp.concatenate([ada_b[0], ada_b[1], kv_ada_b])[None, :]
    c_all = jnp.concatenate([c_prompt, c_sample], axis=0)
    mod = _ada(c_all, ada_all_w, ada_all_b)

    w_in = a_w_in[0].astype(BF16)
    qln = a_q_lora_norm[0][None, :]
    kvln = a_kv_lora_norm[0][None, :]
    wuq_pad = _pad_heads(a_w_uq[0], A_HEADS, QK_DIM).astype(BF16)
    wukv = a_w_ukv[0].reshape(KV_LORA, A_HEADS, NOPE_DIM + V_DIM)
    wuk_pad = jnp.pad(wukv[:, :, :NOPE_DIM], ((0, 0), (0, 0), (0, HEAD_PAD - NOPE_DIM))
                      ).reshape(KV_LORA, A_HEADS * HEAD_PAD).astype(BF16)
    wuv_pad = jnp.pad(wukv[:, :, NOPE_DIM:], ((0, 0), (0, 0), (0, HEAD_PAD - V_DIM))
                      ).reshape(KV_LORA, A_HEADS * HEAD_PAD).astype(BF16)
    vone = jnp.asarray(np.tile(np.eye(1, HEAD_PAD, k=V_DIM), (1, A_HEADS)), F32)
    qn_pad = jnp.pad(a_q_norm[0], (0, HEAD_PAD - QK_DIM))[None, :]
    kn_pad = jnp.pad(a_k_norm[0], (0, HEAD_PAD - QK_DIM))[None, :]
    place = jnp.asarray(np.eye(ROPE_DIM, HEAD_PAD, k=NOPE_DIM), BF16)
    a_wo = a_w_o[0].astype(BF16)
    wk_b = b_w_kv[:, :D].astype(BF16)
    wv_b = b_w_kv[:, D:].astype(BF16)
    bkn_pair = jnp.tile(b_k_norm, 2)[None, :]
    bqn_pair = jnp.tile(b_q_norm[0], 2)[None, :]
    bwq = b_w_q[0].astype(BF16)
    bwo = b_w_o[0].astype(BF16)
    bias_tile = _band_bias(b_rel_bias[0])

    def peer_weights(layer):
        wq = p_w_q[layer].astype(BF16)
        sk = p_subkeys[layer]
        sk1 = jnp.pad(sk[:, 0], ((0, 0), (0, 0), (0, P_HALF))).astype(BF16)
        sk2 = jnp.pad(sk[:, 1], ((0, 0), (0, 0), (P_HALF, 0))).astype(BF16)
        vt = p_v[layer].astype(BF16).reshape(N_EXPERTS // PEER_EC, PEER_EC, D).transpose(0, 2, 1)
        return wq, sk1, sk2, p_u[layer].astype(BF16), vt

    peer_w = [peer_weights(0), peer_weights(1)]
    norm_mix = norm_mix_w[:, None, :]
    norm_ffn = norm_ffn_w[:, None, :]
    kv_nw = kv_norm_w[None, :]

    def run(x, modp, per_token, pos_q, past):
        B, T, _ = x.shape
        if per_token:
            modv = jnp.repeat(modp, T, axis=0)[None]
            xw = x.reshape(1, B * T, D)
        else:
            modv = modp[:, None, :]
            xw = x
        Bw, Tw, _ = xw.shape
        sl = lambda k: modv[:, :, k * D:(k + 1) * D]
        tb = _block(Tw, (512, 256, 128))
        tbr = _block(Tw, (512, 256, 128))
        tbe = _block(Tw, (512, 256, 128))

        cos_q, sin_q = _rope_tables(pos_q)
        ckv, kr, q = _mla_pre(xw, sl(0), sl(1), norm_mix[0], w_in, qln, kvln, wuq_pad, qn_pad,
                              cos_q, sin_q, tb)
        ckv = ckv.reshape(B, T, KV_LORA)
        kr = kr.reshape(B, T, ROPE_DIM)
        q = q.reshape(B, T, A_HEADS * HEAD_PAD)
        if past is None:
            ckv_all, kr_all = ckv, kr
            pos_k = pos_q[:T]
        else:
            ckv_all = jnp.concatenate([past[0], ckv], axis=1)
            kr_all = jnp.concatenate([past[1], kr], axis=1)
            pos_k = jnp.arange(P + T, dtype=jnp.int32)
        S = ckv_all.shape[1]
        cos_k, sin_k = _rope_tables(pos_k)
        tbk = _block(S, (512, 704, 256, 192, 64))
        k, v = _mla_kv(ckv_all, kr_all, wuk_pad, wuv_pad, vone, kn_pad, place, cos_k, sin_k, tbk)
        if past is None:
            ta = _block(T, (512, 256, 128, 64))
            o = _mla_attn(q, k, v, ta, ta, True)
        else:
            o = _mla_attn(q, k, v, T, tbk, False)
        o = o.reshape(Bw, Tw, A_HEADS * V_DIM)

        wq, sk1, sk2, u, vt = peer_w[0]
        x1, h2, rank, cnt, e2, r = _post_mix(o, xw, sl(2), sl(3), sl(4), norm_ffn[0], a_wo, wq, sk1, sk2, tbr)
        x2 = _peer_dense(h2, u, vt, rank, cnt, e2, r, x1, sl(5), tbe)

        tail = B_WINDOW if past is None else Tw
        kf, vf, kb, vb = _shared_kv(x2, sl(12), sl(13), kv_nw, wk_b, wv_b, bkn_pair, tb, tail)

        qb = _band_pre(x2, sl(6), sl(7), norm_mix[1], bwq, bqn_pair, tb)
        if past is None:
            ob = _band_attn(qb, kb, vb, bias_tile, B_WINDOW)
            new_bk = kf.reshape(B, B_WINDOW, B_HEADS, B_HEAD_DIM)
            new_bv = vf.reshape(B, B_WINDOW, B_HEADS, B_HEAD_DIM)
        else:
            zq = jnp.zeros((B, CHUNK, D), BF16)
            qpad = jnp.concatenate([zq, qb.reshape(B, T, D)], axis=1)
            kwin = jnp.concatenate([zq, past[2].reshape(B, Pb, D).astype(BF16), kb.reshape(B, T, D)], axis=1)
            vwin = jnp.concatenate([zq, past[3].reshape(B, Pb, D).astype(BF16), vb.reshape(B, T, D)], axis=1)
            ob = _band_step(qpad, kwin, vwin, bias_tile)[:, CHUNK:].reshape(Bw, Tw, D)
            new_bk = jnp.concatenate([past[2], kf.reshape(B, T, B_HEADS, B_HEAD_DIM)], axis=1)[:, -Pb:]
            new_bv = jnp.concatenate([past[3], vf.reshape(B, T, B_HEADS, B_HEAD_DIM)], axis=1)[:, -Pb:]

        wq, sk1, sk2, u, vt = peer_w[1]
        x3, h4, rank, cnt, e2, r = _post_mix(ob, x2, sl(8), sl(9), sl(10), norm_ffn[1], bwo, wq, sk1, sk2, tbr)
        y = _peer_dense(h4, u, vt, rank, cnt, e2, r, x3, sl(11), tbe)
        return y.reshape(B, T, D), ckv[None], kr[None], new_bk, new_bv

    pos_p = jnp.arange(Tp, dtype=jnp.int32)
    pos_s = jnp.tile(P + jnp.arange(Ts, dtype=jnp.int32), Bs)
    y_p, p_ckv, p_kr, p_bk, p_bv = run(x_prompt, mod[:Bp], False, pos_p, None)
    y_s, s_ckv, s_kr, s_bk, s_bv = run(x_sample, mod[Bp:], True, pos_s,
                                       (cache_a_ckv[0], cache_a_krope[0], cache_b_k, cache_b_v))
    return (y_p, y_s, p_ckv, p_kr, p_bk, p_bv, s_ckv, s_kr, s_bk, s_bv)
```

```python
import functools
import math

import numpy as np
import jax
import jax.numpy as jnp
from jax import lax
from jax.experimental import pallas as pl
from jax.experimental.pallas import tpu as pltpu

F32 = jnp.float32
BF16 = jnp.bfloat16

D_MODEL = 1024
CHUNK = 64
A_HEADS = 16
Q_LORA = 384
KV_LORA = 256
NOPE_DIM = 64
ROPE_DIM = 32
V_DIM = 64
QK_DIM = NOPE_DIM + ROPE_DIM
ROPE_THETA = 10000.0
B_HEADS = 16
B_HEAD_DIM = 64
LEFT_CHUNKS = 8
B_WINDOW = LEFT_CHUNKS * CHUNK
REL_CLIP = 128
P_HEADS = 8
N_KEYS = 128
N_EXPERTS = N_KEYS * N_KEYS
P_HALF = 64
P_TOPK = 16
NEG = -1e30
EPS = 1e-6

LANES = 128
BF16_ROWS = 16
HEAD_PAD = 128
BAND_SUB = 2 * CHUNK
BAND_WIN = B_WINDOW + BAND_SUB
VMEM_LIMIT = 56 * 1024 * 1024
PEER_EC = 512
PEER_SPLIT = 2
MXU_N = 256
GATE_LANES = 256

_NT = (((1,), (1,)), ((), ()))


def _cparams(sem):
    return pltpu.CompilerParams(dimension_semantics=sem, vmem_limit_bytes=VMEM_LIMIT)


def _modulate(x, w, shift, scale):
    ms = jnp.mean(x * x, axis=-1, keepdims=True)
    y = x * lax.rsqrt(ms + EPS) * w
    return y * (1.0 + scale) + shift


def _rms(x, w):
    ms = jnp.mean(x * x, axis=-1, keepdims=True)
    return x * lax.rsqrt(ms + EPS) * w


def _mod_spec(arr, tb):
    if arr.shape[1] == 1:
        return pl.BlockSpec((1, 1, arr.shape[2]), lambda b, i, *_: (b, 0, 0))
    return pl.BlockSpec((1, tb, arr.shape[2]), lambda b, i, *_: (b, i, 0))


def _full_spec(arr):
    nd = arr.ndim
    return pl.BlockSpec(arr.shape, lambda *_: (0,) * nd)


def _ada_kernel(c_ref, w_ref, b_ref, o_ref):
    c = c_ref[...]
    a = c / (1.0 + jnp.exp(-c))
    o_ref[...] = jnp.dot(a.astype(BF16), w_ref[...], preferred_element_type=F32) + b_ref[...]


def _ada(c, w, b, tn=2048):
    m, k = c.shape
    n = w.shape[1]
    return pl.pallas_call(
        _ada_kernel,
        grid=(n // tn,),
        in_specs=[pl.BlockSpec((m, k), lambda j: (0, 0)),
                  pl.BlockSpec((k, tn), lambda j: (0, j)),
                  pl.BlockSpec((1, tn), lambda j: (0, j))],
        out_specs=pl.BlockSpec((m, tn), lambda j: (0, j)),
        out_shape=jax.ShapeDtypeStruct((m, n), F32),
        compiler_params=_cparams(("arbitrary",)),
        name="ada_mod",
    )(c, w, b)


def _rope_head(xh, cos, sin, lane):
    del lane
    return xh * cos + pltpu.roll(xh, LANES // 2, 1) * sin


def _mla_pre_kernel(x_ref, sh_ref, sc_ref, nw_ref, win_ref, qln_ref, kvln_ref, wuq_ref, qn_ref,
                    cos_ref, sin_ref, ckv_ref, kr_ref, q_ref):
    x = x_ref[0]
    h = _modulate(x, nw_ref[...], sh_ref[0], sc_ref[0])
    proj = jnp.dot(h.astype(BF16), win_ref[...], preferred_element_type=F32)
    c_q = _rms(proj[:, :Q_LORA], qln_ref[...])
    ckv_ref[0] = _rms(proj[:, Q_LORA:Q_LORA + KV_LORA], kvln_ref[...])
    kr_ref[0] = proj[:, Q_LORA + KV_LORA:]
    q = jnp.dot(c_q.astype(BF16), wuq_ref[...], preferred_element_type=F32)
    cos = cos_ref[...]
    sin = sin_ref[...]
    qn = qn_ref[...]
    lane = lax.broadcasted_iota(jnp.int32, cos.shape, 1)
    scale = QK_DIM ** -0.5
    for hd in range(A_HEADS):
        qh = q[:, hd * HEAD_PAD:(hd + 1) * HEAD_PAD]
        ss = jnp.sum(qh * qh, axis=-1, keepdims=True) * (1.0 / QK_DIM)
        qh = qh * lax.rsqrt(ss + EPS) * qn
        qh = _rope_head(qh, cos, sin, lane)
        q_ref[0, :, hd * HEAD_PAD:(hd + 1) * HEAD_PAD] = (qh * scale).astype(BF16)


def _mla_pre(x, shift, scale, nw, w_in, qln, kvln, wuq_pad, qn_pad, cos_t, sin_t, tb):
    B, T, D = x.shape
    nT = T // tb
    tab_spec = pl.BlockSpec((tb, LANES), lambda b, i: (i, 0))
    return pl.pallas_call(
        _mla_pre_kernel,
        grid=(B, nT),
        in_specs=[pl.BlockSpec((1, tb, D), lambda b, i: (b, i, 0)),
                  _mod_spec(shift, tb), _mod_spec(scale, tb),
                  _full_spec(nw), _full_spec(w_in), _full_spec(qln), _full_spec(kvln),
                  _full_spec(wuq_pad), _full_spec(qn_pad), tab_spec, tab_spec],
        out_specs=[pl.BlockSpec((1, tb, KV_LORA), lambda b, i: (b, i, 0)),
                   pl.BlockSpec((1, tb, ROPE_DIM), lambda b, i: (b, i, 0)),
                   pl.BlockSpec((1, tb, A_HEADS * HEAD_PAD), lambda b, i: (b, i, 0))],
        out_shape=[jax.ShapeDtypeStruct((B, T, KV_LORA), F32),
                   jax.ShapeDtypeStruct((B, T, ROPE_DIM), F32),
                   jax.ShapeDtypeStruct((B, T, A_HEADS * HEAD_PAD), BF16)],
        compiler_params=_cparams(("parallel", "parallel")),
        name="mla_pre",
    )(x, shift, scale, nw, w_in, qln, kvln, wuq_pad, qn_pad, cos_t, sin_t)


def _mla_kv_kernel(ckv_ref, kr_ref, wuk_ref, wuv_ref, vone_ref, kn_ref, place_ref, cos_ref, sin_ref,
                   k_ref, v_ref):
    ckv = ckv_ref[0].astype(BF16)
    kn = jnp.dot(ckv, wuk_ref[...], preferred_element_type=F32)
    v_ref[0] = (jnp.dot(ckv, wuv_ref[...], preferred_element_type=F32) + vone_ref[...]).astype(BF16)
    kr = kr_ref[0]
    kr_hi = kr.astype(BF16)
    kr_lo = (kr - kr_hi.astype(F32)).astype(BF16)
    place = place_ref[...]
    krp = (jnp.dot(kr_hi, place, preferred_element_type=F32)
           + jnp.dot(kr_lo, place, preferred_element_type=F32))
    cos = cos_ref[...]
    sin = sin_ref[...]
    knw = kn_ref[...]
    lane = lax.broadcasted_iota(jnp.int32, cos.shape, 1)
    for hd in range(A_HEADS):
        kh = kn[:, hd * HEAD_PAD:(hd + 1) * HEAD_PAD] + krp
        ss = jnp.sum(kh * kh, axis=-1, keepdims=True) * (1.0 / QK_DIM)
        kh = kh * lax.rsqrt(ss + EPS) * knw
        kh = _rope_head(kh, cos, sin, lane)
        k_ref[0, :, hd * HEAD_PAD:(hd + 1) * HEAD_PAD] = kh.astype(BF16)


def _mla_kv(ckv, kr, wuk_pad, wuv_pad, vone, kn_pad, place, cos_t, sin_t, tb):
    B, S, _ = ckv.shape
    tab_spec = pl.BlockSpec((tb, LANES), lambda b, i: (i, 0))
    wide = pl.BlockSpec((1, tb, A_HEADS * HEAD_PAD), lambda b, i: (b, i, 0))
    return pl.pallas_call(
        _mla_kv_kernel,
        grid=(B, S // tb),
        in_specs=[pl.BlockSpec((1, tb, KV_LORA), lambda b, i: (b, i, 0)),
                  pl.BlockSpec((1, tb, ROPE_DIM), lambda b, i: (b, i, 0)),
                  _full_spec(wuk_pad), _full_spec(wuv_pad), _full_spec(vone), _full_spec(kn_pad),
                  _full_spec(place), tab_spec, tab_spec],
        out_specs=[wide, wide],
        out_shape=[jax.ShapeDtypeStruct((B, S, A_HEADS * HEAD_PAD), BF16),
                   jax.ShapeDtypeStruct((B, S, A_HEADS * HEAD_PAD), BF16)],
        compiler_params=_cparams(("parallel", "parallel")),
        name="mla_kv",
    )(ckv, kr, wuk_pad, wuv_pad, vone, kn_pad, place, cos_t, sin_t)


def _mla_attn_kernel(q_ref, k_ref, v_ref, o_ref, m_sc, acc_sc, *, tq, tk, causal, nk):
    i = pl.program_id(2)
    m_sc[...] = jnp.full(m_sc.shape, -jnp.inf, F32)
    acc_sc[...] = jnp.zeros(acc_sc.shape, F32)

    def block(start, masked):
        k = k_ref[0, pl.ds(start, tk), :]
        v = v_ref[0, pl.ds(start, tk), :]
        if masked:
            qc = lax.broadcasted_iota(jnp.int32, (tq, tk), 0) // CHUNK
            kc = lax.broadcasted_iota(jnp.int32, (tq, tk), 1) // CHUNK
            allowed = kc <= qc
        for hh in range(2):
            qh = q_ref[0, :, hh * HEAD_PAD:(hh + 1) * HEAD_PAD]
            s = lax.dot_general(qh, k[:, hh * HEAD_PAD:(hh + 1) * HEAD_PAD], _NT,
                                preferred_element_type=F32)
            if masked:
                s = jnp.where(allowed, s, NEG)
            m_prev = m_sc[hh]
            m_new = jnp.maximum(m_prev, jnp.max(s, axis=-1, keepdims=True))
            alpha = jnp.exp(m_prev - m_new)
            if tk % LANES == 0:
                m_wide = jnp.concatenate([m_new] * (tk // LANES), axis=1)
            else:
                m_wide = m_new[:, :1]
            p = jnp.exp((s - m_wide).astype(BF16))
            acc_sc[hh] = alpha * acc_sc[hh] + jnp.dot(p, v[:, hh * HEAD_PAD:(hh + 1) * HEAD_PAD],
                                                      preferred_element_type=F32)
            m_sc[hh] = m_new

    def body(j, carry):
        block(pl.multiple_of(j * tk, tk), False)
        return carry

    if causal:
        lax.fori_loop(0, i, body, 0)
        block(pl.multiple_of(i * tk, tk), True)
    else:
        lax.fori_loop(0, nk, body, 0)

    lane = lax.broadcasted_iota(jnp.int32, (tq, LANES), 1)
    outs = []
    for hh in range(2):
        acc = acc_sc[hh]
        outs.append(acc / acc[:, V_DIM:V_DIM + 1])
    o = jnp.where(lane < V_DIM, outs[0], pltpu.roll(outs[1], V_DIM, 1))
    o_ref[0] = o.astype(BF16)


def _mla_attn(q, k, v, tq, tk, causal):
    B, T, _ = q.shape
    S = k.shape[1]
    nq, nk = T // tq, S // tk
    assert T % tq == 0 and S % tk == 0 and (not causal or (tq == tk and T == S))
    kv_spec = pl.BlockSpec((1, S, 2 * HEAD_PAD), lambda b, hp, i: (b, 0, hp))
    return pl.pallas_call(
        functools.partial(_mla_attn_kernel, tq=tq, tk=tk, causal=causal, nk=nk),
        grid=(B, A_HEADS // 2, nq),
        in_specs=[pl.BlockSpec((1, tq, 2 * HEAD_PAD), lambda b, hp, i: (b, i, hp)), kv_spec, kv_spec],
        out_specs=pl.BlockSpec((1, tq, 2 * V_DIM), lambda b, hp, i: (b, i, hp)),
        out_shape=jax.ShapeDtypeStruct((B, T, A_HEADS * V_DIM), BF16),
        scratch_shapes=[pltpu.VMEM((2, tq, LANES), F32), pltpu.VMEM((2, tq, LANES), F32)],
        compiler_params=_cparams(("parallel", "parallel", "arbitrary")),
        name="mla_attn",
    )(q, k, v)


def _odd_even_merge_sort_pairs(n):
    pairs = []

    def merge(lo, m, r):
        step = r * 2
        if step < m:
            merge(lo, m, step)
            merge(lo + r, m, step)
            pairs.extend((i, i + r) for i in range(lo + r, lo + m - r, step))
        else:
            pairs.append((lo, lo + r))

    def sort(lo, m):
        if m > 1:
            sort(lo, m // 2)
            sort(lo + m // 2, m // 2)
            merge(lo, m, 1)

    sort(0, n)
    return pairs


_SORT16 = _odd_even_merge_sort_pairs(P_TOPK)
_BITONIC16 = [(i, i + d) for d in (8, 4, 2, 1) for i in range(P_TOPK) if not i & d]
SUBLANES = 8


def _top16_rows(chains):
    assert all(s.shape[0] == P_TOPK * SUBLANES for s, _, _ in chains)
    vs = [[s[SUBLANES * j:SUBLANES * (j + 1), :] for j in range(P_TOPK)] for s, _, _ in chains]

    def compare_exchange(pairs):
        for i, j in pairs:
            for v in vs:
                v[i], v[j] = jnp.maximum(v[i], v[j]), jnp.minimum(v[i], v[j])

    compare_exchange(_SORT16)
    for shift in (4, 2, 1):
        for v in vs:
            other = [pltpu.roll(x, shift, 0) for x in v]
            v[:] = [jnp.maximum(v[k], other[P_TOPK - 1 - k]) for k in range(P_TOPK)]
        compare_exchange(_BITONIC16)
    for v, (_, t_sc, cols) in zip(vs, chains):
        for k in range(P_TOPK):
            t_sc[k:k + 1, cols] = v[k][0:1, :]


def _prefix_count(test, rows):
    assert len(rows) == P_TOPK
    m = [test(rows[4 * q + 3]) for q in range(4)]
    base = jnp.where(m[0], 4.0, 0.0)
    for q in range(1, 4):
        base = jnp.where(m[q], 4.0 * (q + 1), base)
    sub = jnp.zeros_like(base)
    for j in range(3):
        t_sel = jnp.where(m[2], rows[12 + j], jnp.where(m[1], rows[8 + j], jnp.where(m[0], rows[4 + j], rows[j])))
        sub = jnp.where(test(t_sel), float(j + 1), sub)
    return jnp.minimum(base + sub, float(P_TOPK))


def _route_tiles(tiles, t1_sc, t2_sc):
    _top16_rows([(s1, t1_sc, cols) for s1, _, cols in tiles] + [(s2, t2_sc, cols) for _, s2, cols in tiles])
    state = []
    for s1, s2, cols in tiles:
        row8 = lax.broadcasted_iota(jnp.int32, (8, s1.shape[1]), 0)
        t1 = [t1_sc[k:k + 1, cols] for k in range(P_TOPK)]
        t2 = [t2_sc[k:k + 1, cols] for k in range(P_TOPK)]
        t2_lo = t2_sc[0:8, cols]
        t2_hi = t2_sc[8:16, cols]
        t1_hi = t1_sc[8:16, cols]
        cands = [t1[0] + t2_lo, t1[0] + t2_hi, t1_hi + t2[0]]
        for k1 in range(1, 8):
            lim = P_TOPK // (k1 + 1)
            c = t1[k1] + t2_lo
            cands.append(c if lim >= 8 else jnp.where(row8 < lim, c, -jnp.inf))
        top = t1[0] + t2[0]
        state.append(dict(t1=t1, t2=t2, cands=cands, top=top, z=jnp.zeros_like(top), tau=top))
    for k in range(P_TOPK):
        for st in state:
            m = st["cands"][0]
            for c in st["cands"][1:]:
                m = jnp.maximum(m, c)
            m = jnp.max(m, axis=0, keepdims=True)
            st["z"] = st["z"] + jnp.exp(m - st["top"])
            st["tau"] = m
            if k + 1 < P_TOPK:
                st["cands"] = [jnp.where(c == m, -jnp.inf, c) for c in st["cands"]]
    outs = []
    for (s1, s2, cols), st in zip(tiles, state):
        tau = st["tau"]
        cnt = _prefix_count(lambda t: s1 + t >= tau, st["t2"])
        rank = _prefix_count(lambda t: t > s2, st["t2"])
        n1 = jnp.sum(jnp.where(s1 >= st["t1"][P_TOPK - 1], 1.0, 0.0), axis=0, keepdims=True)
        n2 = jnp.sum(jnp.where(s2 >= st["t2"][P_TOPK - 1], 1.0, 0.0), axis=0, keepdims=True)
        npair = jnp.sum(cnt, axis=0, keepdims=True)
        k = float(P_TOPK)
        tied = jnp.where((n1 != k) | (n2 != k) | (npair != k), 1.0, 0.0)
        outs.append((rank, cnt, jnp.exp(s2 - st["t2"][0]), jnp.exp(s1 - st["t1"][0]) / st["z"], tied))
    return outs


def _route_tile_exact(s1, s2, t1_sc, t2_sc, cols):
    nk, w = s1.shape
    key = lax.broadcasted_iota(jnp.int32, (nk, w), 0).astype(F32)

    def top16(s, t_sc):
        work, rank, tops = s, jnp.full(s.shape, float(P_TOPK), F32), []
        for k in range(P_TOPK):
            m = jnp.max(work, axis=0, keepdims=True)
            first = jnp.min(jnp.where(work == m, key, float(nk)), axis=0, keepdims=True)
            hit = key == first
            rank = jnp.where(hit, float(k), rank)
            work = jnp.where(hit, -jnp.inf, work)
            t_sc[k:k + 1, cols] = m
            tops.append(m)
        return tops, rank

    t1, rank1 = top16(s1, t1_sc)
    t2, rank2 = top16(s2, t2_sc)
    row8 = lax.broadcasted_iota(jnp.int32, (8, w), 0).astype(F32)
    t2_lo = t2_sc[0:8, cols]
    t2_hi = t2_sc[8:16, cols]
    t1_hi = t1_sc[8:16, cols]
    cands = [(t1[0] + t2_lo, row8), (t1[0] + t2_hi, row8 + 8.0), (t1_hi + t2[0], (row8 + 8.0) * P_TOPK)]
    for k1 in range(1, 8):
        lim = P_TOPK // (k1 + 1)
        c = t1[k1] + t2_lo
        cands.append((c if lim >= 8 else jnp.where(row8 < lim, c, -jnp.inf), row8 + float(k1 * P_TOPK)))
    top = t1[0] + t2[0]
    z = jnp.zeros_like(top)
    k1_row = lax.broadcasted_iota(jnp.int32, (P_TOPK, w), 0).astype(F32)
    cnt_k1 = jnp.zeros((P_TOPK, w), F32)
    for k in range(P_TOPK):
        m = cands[0][0]
        for c, _ in cands[1:]:
            m = jnp.maximum(m, c)
        m = jnp.max(m, axis=0, keepdims=True)
        first = None
        for c, pos in cands:
            f = jnp.min(jnp.where(c == m, pos, float(P_TOPK * P_TOPK)), axis=0, keepdims=True)
            first = f if first is None else jnp.minimum(first, f)
        z = z + jnp.exp(m - top)
        cnt_k1 = cnt_k1 + jnp.where(k1_row == jnp.floor(first * (1.0 / P_TOPK)), 1.0, 0.0)
        cands = [(jnp.where(pos == first, -jnp.inf, c), pos) for c, pos in cands]
    t1_sc[:, cols] = cnt_k1
    cnt = jnp.zeros_like(s1)
    for k1 in range(P_TOPK):
        cnt = jnp.where(rank1 == float(k1), t1_sc[k1:k1 + 1, cols], cnt)
    return rank2, cnt, jnp.exp(s1 - t1[0]) / z


def _post_mix_kernel(o_ref, x_ref, g_ref, sh_ref, sc_ref, nw_ref, wo_ref, wq_ref, sk1_ref, sk2_ref,
                     x1_ref, h2_ref, rank_ref, cnt_ref, e2_ref, r_ref, t1_sc, t2_sc, s_sc, tied_sc):
    mix = jnp.dot(o_ref[0], wo_ref[...], preferred_element_type=F32)
    x1 = x_ref[0] + g_ref[0] * mix
    x1_ref[0] = x1
    h2 = _modulate(x1, nw_ref[...], sh_ref[0], sc_ref[0]).astype(BF16)
    h2_ref[0] = h2
    tb = h2.shape[0]
    q = jnp.dot(h2, wq_ref[...], preferred_element_type=F32).astype(BF16)
    for h in range(P_HEADS):
        qh = q[:, h * 2 * P_HALF:(h + 1) * 2 * P_HALF]
        s_sc[h, 0] = lax.dot_general(sk1_ref[h], qh, _NT, preferred_element_type=F32)
        s_sc[h, 1] = lax.dot_general(sk2_ref[h], qh, _NT, preferred_element_type=F32)

    def head_body(h, carry):
        s1 = s_sc[h, 0]
        s2 = s_sc[h, 1]
        lane_tiles = [slice(nt * LANES, (nt + 1) * LANES) for nt in range(tb // LANES)]
        routed = _route_tiles([(s1[:, cols], s2[:, cols], cols) for cols in lane_tiles], t1_sc, t2_sc)
        for cols, (rank, cnt, e2, r, tied) in zip(lane_tiles, routed):
            rank_ref[h, :, cols] = rank.astype(BF16)
            cnt_ref[h, :, cols] = cnt
            e2_ref[h, :, cols] = e2.astype(BF16)
            r_ref[h, :, cols] = r
            tied_sc[h, :, cols] = jnp.broadcast_to(tied, (8, tied.shape[1]))
        return carry

    lax.fori_loop(0, P_HEADS, head_body, 0)

    @pl.when(jnp.max(tied_sc[...]) > 0.0)
    def _():
        def fix_head(h, carry):
            for nt in range(tb // LANES):
                cols = slice(nt * LANES, (nt + 1) * LANES)

                @pl.when(jnp.max(tied_sc[h, :, cols]) > 0.0)
                def _(cols=cols):
                    rank_x, cnt_x, r_x = _route_tile_exact(s_sc[h, 0, :, cols], s_sc[h, 1, :, cols],
                                                           t1_sc, t2_sc, cols)
                    rank_ref[h, :, cols] = rank_x.astype(BF16)
                    cnt_ref[h, :, cols] = cnt_x
                    r_ref[h, :, cols] = r_x
            return carry

        lax.fori_loop(0, P_HEADS, fix_head, 0)


def _post_mix(o, x, gate, shift, scale, nw, w_o, wq_heads, sk1, sk2, tb):
    B, T, D = x.shape
    nT = T // tb
    n = B * T
    tok = lambda b, i: (b, i, 0)
    rt_spec = pl.BlockSpec((P_HEADS, N_KEYS, tb), lambda b, i: (0, 0, b * nT + i))
    rt_shape = jax.ShapeDtypeStruct((P_HEADS, N_KEYS, n), F32)
    rt_shape_b = jax.ShapeDtypeStruct((P_HEADS, N_KEYS, n), BF16)
    return pl.pallas_call(
        _post_mix_kernel,
        grid=(B, nT),
        in_specs=[pl.BlockSpec((1, tb, o.shape[2]), tok), pl.BlockSpec((1, tb, D), tok),
                  _mod_spec(gate, tb), _mod_spec(shift, tb), _mod_spec(scale, tb),
                  _full_spec(nw), _full_spec(w_o), _full_spec(wq_heads), _full_spec(sk1), _full_spec(sk2)],
        out_specs=[pl.BlockSpec((1, tb, D), tok), pl.BlockSpec((1, tb, D), tok),
                   rt_spec, rt_spec, rt_spec, rt_spec],
        out_shape=[jax.ShapeDtypeStruct((B, T, D), F32), jax.ShapeDtypeStruct((B, T, D), BF16),
                   rt_shape_b, rt_shape, rt_shape_b, rt_shape],
        scratch_shapes=[pltpu.VMEM((P_TOPK, tb), F32), pltpu.VMEM((P_TOPK, tb), F32),
                        pltpu.VMEM((P_HEADS, 2, N_KEYS, tb), F32), pltpu.VMEM((P_HEADS, 8, tb), F32)],
        compiler_params=_cparams(("parallel", "parallel")),
        name="post_mix_route",
    )(o, x, gate, shift, scale, nw, w_o, wq_heads, sk1, sk2)


def _gelu_tanh(a):
    c = math.sqrt(2.0 / math.pi)
    return (0.5 * a) * (1.0 + jnp.tanh(a * ((0.044715 * c) * (a * a) + c)))


def _peer_dense_kernel(h_ref, u_ref, vt_ref, rank_ref, cnt_ref, e2_ref, r_ref, x_ref, g_ref,
                       o_ref, acc_ref, a0_ref, a1_ref, hc0_ref, hc1_ref, *, ec, ne):
    g = pl.program_id(0)
    per = ec // N_KEYS
    s = lax.rem(jnp.maximum(g - 1, 0), ne) + 1
    c3 = lax.rem(jnp.maximum(g - 2, 0), ne)

    @pl.when(g == 0)
    def _():
        for ref in (a0_ref, a1_ref, hc0_ref, hc1_ref):
            ref[...] = jnp.zeros(ref.shape, ref.dtype)

    @pl.when(c3 == 0)
    def _():
        acc_ref[...] = jnp.zeros(acc_ref.shape, F32)

    def stages(a_out, a_in, hc_out, hc_in):
        zero = jnp.zeros((), BF16)
        tb = h_ref.shape[1]
        nsplit = PEER_SPLIT
        gate_w = min(GATE_LANES, tb)
        mxu_w = min(MXU_N, tb)
        assert tb % gate_w == 0 and tb % mxu_w == 0

        row_cache = {}
        reps = N_KEYS // BF16_ROWS

        def routed_rows(ii):
            if ii not in row_cache:
                i1 = jnp.clip((s - 1) * per + ii, 0, N_KEYS - 1)
                wide = lambda ref, h: jnp.broadcast_to(ref[h, pl.ds(i1, 1), :], (BF16_ROWS, tb)).astype(BF16)
                row_cache[ii] = [(wide(cnt_ref, h), wide(r_ref, h)) for h in range(P_HEADS)]
            return row_cache[ii]

        def gate_piece(ii, n):
            cols = slice(n * gate_w, (n + 1) * gate_w)
            rows = slice(ii * N_KEYS, (ii + 1) * N_KEYS)
            w = None
            for h, (cnt_row, r_row) in enumerate(routed_rows(ii)):
                cnt_t = jnp.concatenate([cnt_row[:, cols]] * reps, axis=0)
                r_t = jnp.concatenate([r_row[:, cols]] * reps, axis=0)
                contrib = jnp.where(rank_ref[h, :, cols] < cnt_t, e2_ref[h, :, cols], zero) * r_t
                w = contrib if w is None else w + contrib
            hc_out[rows, cols] = _gelu_tanh(a_in[rows, cols]) * w

        def score_piece(q, n):
            cols = slice(n * mxu_w, (n + 1) * mxu_w)
            rows = slice(q * (ec // nsplit), (q + 1) * (ec // nsplit))
            a_out[rows, cols] = lax.dot_general(u_ref[rows, :], h_ref[0, cols, :], _NT,
                                                preferred_element_type=F32).astype(BF16)

        def out_piece(q, n):
            cols = slice(n * mxu_w, (n + 1) * mxu_w)
            rows = slice(q * (D_MODEL // nsplit), (q + 1) * (D_MODEL // nsplit))
            acc_ref[rows, cols] += jnp.dot(vt_ref[0, rows, :], hc_in[:, cols], preferred_element_type=F32)

        vec = [functools.partial(gate_piece, ii, n) for n in range(tb // gate_w) for ii in range(per)]
        mxu = []
        for n in range(tb // mxu_w):
            for q in range(nsplit):
                mxu.append(functools.partial(score_piece, q, n))
                mxu.append(functools.partial(out_piece, q, n))
        for k in range(max(len(vec), len(mxu))):
            if k < len(mxu):
                mxu[k]()
            if k < len(vec):
                vec[k]()

    even = lax.rem(g, 2) == 0
    pl.when(even)(lambda: stages(a0_ref, a1_ref, hc1_ref, hc0_ref))
    pl.when(jnp.logical_not(even))(lambda: stages(a1_ref, a0_ref, hc0_ref, hc1_ref))

    @pl.when((c3 == ne - 1) & (g >= 2))
    def _():
        o_ref[0] = x_ref[0] + g_ref[0] * acc_ref[...].T


def _peer_dense(h2, u, vt, rank, cnt, e2, r, x, gate, tb, ec=PEER_EC):
    B, T, D = x.shape
    nT = T // tb
    nblk = B * nT
    ne = N_EXPERTS // ec
    assert vt.shape == (ne, D, ec)
    blk1 = lambda g: jnp.minimum(g // ne, nblk - 1)
    blk2 = lambda g: jnp.minimum(jnp.maximum(g - 1, 0) // ne, nblk - 1)
    blk3 = lambda g: jnp.maximum(g - 2, 0) // ne
    tok1 = lambda g: (blk1(g) // nT, blk1(g) % nT, 0)
    tok3 = lambda g: (blk3(g) // nT, blk3(g) % nT, 0)
    rt_spec = pl.BlockSpec((P_HEADS, N_KEYS, tb), lambda g: (0, 0, blk2(g)))
    if gate.shape[1] == 1:
        gate_spec = pl.BlockSpec((1, 1, D), lambda g: (blk3(g) // nT, 0, 0))
    else:
        gate_spec = pl.BlockSpec((1, tb, D), tok3)
    return pl.pallas_call(
        functools.partial(_peer_dense_kernel, ec=ec, ne=ne),
        grid=(nblk * ne + 2,),
        in_specs=[pl.BlockSpec((1, tb, D), tok1),
                  pl.BlockSpec((ec, D), lambda g: (g % ne, 0)),
                  pl.BlockSpec((1, D, ec), lambda g: (jnp.maximum(g - 2, 0) % ne, 0, 0)),
                  rt_spec, rt_spec, rt_spec, rt_spec,
                  pl.BlockSpec((1, tb, D), tok3), gate_spec],
        out_specs=pl.BlockSpec((1, tb, D), tok3),
        out_shape=jax.ShapeDtypeStruct((B, T, D), F32),
        scratch_shapes=[pltpu.VMEM((D, tb), F32),
                        pltpu.VMEM((ec, tb), BF16), pltpu.VMEM((ec, tb), BF16),
                        pltpu.VMEM((ec, tb), BF16), pltpu.VMEM((ec, tb), BF16)],
        compiler_params=_cparams(("arbitrary",)),
        name="peer_dense",
    )(h2, u, vt, rank, cnt, e2, r, x, gate)


def _pair_rms(y, w, lane):
    y2 = y * y
    lo = jnp.sum(jnp.where(lane < B_HEAD_DIM, y2, 0.0), axis=-1, keepdims=True)
    hi = jnp.sum(y2, axis=-1, keepdims=True) - lo
    ms = jnp.where(lane < B_HEAD_DIM, lo, hi) * (1.0 / B_HEAD_DIM)
    return y * lax.rsqrt(ms + EPS) * w


def _shared_kv_kernel(x_ref, sh_ref, sc_ref, nw_ref, wk_ref, wv_ref, kn_ref, k_ref, v_ref, kb_ref, vb_ref,
                      *, first_tail):
    h = _modulate(x_ref[0], nw_ref[...], sh_ref[0], sc_ref[0]).astype(BF16)
    kraw = jnp.dot(h, wk_ref[...], preferred_element_type=F32)
    v = jnp.dot(h, wv_ref[...], preferred_element_type=F32)
    vb_ref[0] = v.astype(BF16)
    knw = kn_ref[...]
    lane = lax.broadcasted_iota(jnp.int32, (h.shape[0], LANES), 1)
    ks = [_pair_rms(kraw[:, hp * LANES:(hp + 1) * LANES], knw, lane) for hp in range(B_HEADS // 2)]
    for hp, kh in enumerate(ks):
        kb_ref[0, :, hp * LANES:(hp + 1) * LANES] = kh.astype(BF16)

    @pl.when(pl.program_id(1) >= first_tail)
    def _():
        v_ref[0] = v
        for hp, kh in enumerate(ks):
            k_ref[0, :, hp * LANES:(hp + 1) * LANES] = kh


def _shared_kv(x, shift, scale, nw, wk, wv, kn_pair, tb, tail):
    B, T, D = x.shape
    nT = T // tb
    assert tail % tb == 0 and tail <= T
    first_tail = nT - tail // tb
    tok = lambda b, i: (b, i, 0)
    blk = pl.BlockSpec((1, tb, D), tok)
    tail_blk = pl.BlockSpec((1, tb, D), lambda b, i: (b, jnp.maximum(i - first_tail, 0), 0))
    return pl.pallas_call(
        functools.partial(_shared_kv_kernel, first_tail=first_tail),
        grid=(B, nT),
        in_specs=[blk, _mod_spec(shift, tb), _mod_spec(scale, tb), _full_spec(nw),
                  _full_spec(wk), _full_spec(wv), _full_spec(kn_pair)],
        out_specs=[tail_blk, tail_blk, blk, blk],
        out_shape=[jax.ShapeDtypeStruct((B, tail, D), F32), jax.ShapeDtypeStruct((B, tail, D), F32),
                   jax.ShapeDtypeStruct((B, T, D), BF16), jax.ShapeDtypeStruct((B, T, D), BF16)],
        compiler_params=_cparams(("parallel", "arbitrary")),
        name="shared_kv",
    )(x, shift, scale, nw, wk, wv, kn_pair)


def _band_pre_kernel(x_ref, sh_ref, sc_ref, nw_ref, wq_ref, qn_ref, q_ref):
    h = _modulate(x_ref[0], nw_ref[...], sh_ref[0], sc_ref[0]).astype(BF16)
    q = jnp.dot(h, wq_ref[...], preferred_element_type=F32)
    qnw = qn_ref[...]
    lane = lax.broadcasted_iota(jnp.int32, (h.shape[0], LANES), 1)
    scale = B_HEAD_DIM ** -0.5
    for hp in range(B_HEADS // 2):
        qh = _pair_rms(q[:, hp * LANES:(hp + 1) * LANES], qnw, lane)
        q_ref[0, :, hp * LANES:(hp + 1) * LANES] = (qh * scale).astype(BF16)


def _band_pre(x, shift, scale, nw, wq, qn_pair, tb):
    B, T, D = x.shape
    tok = lambda b, i: (b, i, 0)
    blk = pl.BlockSpec((1, tb, D), tok)
    return pl.pallas_call(
        _band_pre_kernel,
        grid=(B, T // tb),
        in_specs=[blk, _mod_spec(shift, tb), _mod_spec(scale, tb), _full_spec(nw),
                  _full_spec(wq), _full_spec(qn_pair)],
        out_specs=blk,
        out_shape=jax.ShapeDtypeStruct((B, T, D), BF16),
        compiler_params=_cparams(("parallel", "parallel")),
        name="band_pre",
    )(x, shift, scale, nw, wq, qn_pair)


def _band_bias_kernel(tab_ref, o_ref):
    h = pl.program_id(0)
    nvar = 2 * LANES
    r = lax.broadcasted_iota(jnp.int32, (BAND_SUB, nvar), 0)
    w = lax.broadcasted_iota(jnp.int32, (BAND_SUB, nvar), 1) + (BAND_WIN - nvar)
    idx = jnp.clip(r + B_WINDOW - w, -REL_CLIP, REL_CLIP) + REL_CLIP
    far = tab_ref[h, 2 * REL_CLIP]

    def body(t, acc):
        return jnp.where(idx == t, tab_ref[h, t], acc)

    var = lax.fori_loop(0, 2 * REL_CLIP, body, jnp.full((BAND_SUB, nvar), far, F32))
    full = jnp.concatenate([jnp.full((BAND_SUB, BAND_WIN - nvar), far, F32), var], axis=1)
    rr = lax.broadcasted_iota(jnp.int32, (BAND_SUB, BAND_WIN), 0)
    ww = lax.broadcasted_iota(jnp.int32, (BAND_SUB, BAND_WIN), 1)
    qc = rr // CHUNK + LEFT_CHUNKS
    kc = ww // CHUNK
    allowed = (kc <= qc) & (kc >= qc - LEFT_CHUNKS)
    o_ref[0] = jnp.where(allowed, full, NEG)


def _band_bias(table):
    nh = table.shape[0]
    return pl.pallas_call(
        _band_bias_kernel,
        grid=(nh,),
        in_specs=[pl.BlockSpec(memory_space=pltpu.SMEM)],
        out_specs=pl.BlockSpec((1, BAND_SUB, BAND_WIN), lambda h: (h, 0, 0)),
        out_shape=jax.ShapeDtypeStruct((nh, BAND_SUB, BAND_WIN), F32),
        compiler_params=_cparams(("arbitrary",)),
        name="band_bias",
    )(table)


def _band_windows(windows, bias_ref, lane):
    chains = [(w, hh) for w in range(len(windows)) for hh in range(2)]
    scores = []
    for w, hh in chains:
        qs, kw, _ = windows[w]
        nkw = kw.shape[0]
        sel = (lane < B_HEAD_DIM) if hh == 0 else (lane >= B_HEAD_DIM)
        qh = jnp.where(sel, qs, jnp.zeros_like(qs))
        s = lax.dot_general(qh, kw, _NT, preferred_element_type=F32)
        scores.append(s + bias_ref[hh, :, BAND_WIN - nkw:BAND_WIN])
    maxes = [jnp.max(s, axis=-1, keepdims=True) for s in scores]
    probs = [jnp.exp((s - m).astype(BF16)) for s, m in zip(scores, maxes)]
    vext = []
    for _, _, vw in windows:
        ones_lane = (lax.broadcasted_iota(jnp.int32, vw.shape, 1) == 0).astype(BF16)
        vext.append(jnp.concatenate([vw, ones_lane], axis=1))
    outs = []
    for (w, hh), p in zip(chains, probs):
        pv = jnp.dot(p, vext[w], preferred_element_type=F32)
        outs.append(pv[:, :LANES] / pv[:, LANES:LANES + 1])
    return [jnp.where(lane < B_HEAD_DIM, outs[2 * w], outs[2 * w + 1]) for w in range(len(windows))]


def _band_attn_kernel(q_ref, kp_ref, kc_ref, vp_ref, vc_ref, bias_ref, o_ref, *, tq):
    i = pl.program_id(2)
    lane = lax.broadcasted_iota(jnp.int32, (BAND_SUB, LANES), 1)
    nsub = tq // BAND_SUB

    def run(first):
        windows = []
        for c in range(nsub):
            qs = q_ref[0, c * BAND_SUB:(c + 1) * BAND_SUB, :]
            hi = (c + 1) * BAND_SUB
            if first or hi >= BAND_WIN:
                lo = max(hi - BAND_WIN, 0)
                kw = kc_ref[0, lo:hi, :]
                vw = vc_ref[0, lo:hi, :]
            else:
                lo = tq - (BAND_WIN - hi)
                kw = jnp.concatenate([kp_ref[0, lo:tq, :], kc_ref[0, 0:hi, :]], axis=0)
                vw = jnp.concatenate([vp_ref[0, lo:tq, :], vc_ref[0, 0:hi, :]], axis=0)
            windows.append((qs, kw, vw))
        for c, o in enumerate(_band_windows(windows, bias_ref, lane)):
            o_ref[0, c * BAND_SUB:(c + 1) * BAND_SUB, :] = o.astype(BF16)

    pl.when(i == 0)(lambda: run(True))
    pl.when(i > 0)(lambda: run(False))


def _band_attn(q, k, v, bias, tq):
    B, T, D = q.shape
    assert tq >= B_WINDOW and T % tq == 0
    cur = lambda b, hp, i: (b, i, hp)
    prev = lambda b, hp, i: (b, jnp.maximum(i - 1, 0), hp)
    blk = lambda m: pl.BlockSpec((1, tq, LANES), m)
    return pl.pallas_call(
        functools.partial(_band_attn_kernel, tq=tq),
        grid=(B, B_HEADS // 2, T // tq),
        in_specs=[blk(cur), blk(prev), blk(cur), blk(prev), blk(cur),
                  pl.BlockSpec((2, BAND_SUB, BAND_WIN), lambda b, hp, i: (hp, 0, 0))],
        out_specs=blk(cur),
        out_shape=jax.ShapeDtypeStruct((B, T, D), BF16),
        compiler_params=_cparams(("parallel", "parallel", "arbitrary")),
        name="band_attn",
    )(q, k, k, v, v, bias)


def _band_step_kernel(q_ref, k_ref, v_ref, bias_ref, o_ref):
    lane = lax.broadcasted_iota(jnp.int32, (BAND_SUB, LANES), 1)
    o, = _band_windows([(q_ref[0], k_ref[0], v_ref[0])], bias_ref, lane)
    o_ref[0] = o.astype(BF16)


def _band_step(q, kwin, vwin, bias):
    B, _, D = q.shape
    return pl.pallas_call(
        _band_step_kernel,
        grid=(B, B_HEADS // 2),
        in_specs=[pl.BlockSpec((1, BAND_SUB, LANES), lambda b, hp: (b, 0, hp)),
                  pl.BlockSpec((1, BAND_WIN, LANES), lambda b, hp: (b, 0, hp)),
                  pl.BlockSpec((1, BAND_WIN, LANES), lambda b, hp: (b, 0, hp)),
                  pl.BlockSpec((2, BAND_SUB, BAND_WIN), lambda b, hp: (hp, 0, 0))],
        out_specs=pl.BlockSpec((1, BAND_SUB, LANES), lambda b, hp: (b, 0, hp)),
        out_shape=jax.ShapeDtypeStruct((B, BAND_SUB, D), BF16),
        compiler_params=_cparams(("parallel", "parallel")),
        name="band_step",
    )(q, kwin, vwin, bias)


def _rope_tables(pos):
    half = ROPE_DIM // 2
    freqs = ROPE_THETA ** (-jnp.arange(half, dtype=F32) / half)
    ang = pos.astype(F32)[:, None] * freqs[None, :]
    c, s = jnp.cos(ang), jnp.sin(ang)
    n = pos.shape[0]
    ones = jnp.ones((n, NOPE_DIM), F32)
    zeros = jnp.zeros((n, NOPE_DIM), F32)
    pad1 = jnp.ones((n, HEAD_PAD - QK_DIM), F32)
    pad0 = jnp.zeros((n, HEAD_PAD - QK_DIM), F32)
    return (_permute_heads(jnp.concatenate([ones, c, c, pad1], axis=1)),
            _permute_heads(jnp.concatenate([zeros, -s, s, pad0], axis=1)))


def _pad_heads(w, nheads, width):
    lead = w.shape[:-1]
    w = w.reshape(lead + (nheads, width))
    w = jnp.pad(w, [(0, 0)] * len(lead) + [(0, 0), (0, HEAD_PAD - width)])
    return w.reshape(lead + (nheads * HEAD_PAD,))


def _head_lanes():
    half = ROPE_DIM // 2
    keep = LANES // 2 - half
    old = np.arange(HEAD_PAD)
    return np.concatenate([old[NOPE_DIM:NOPE_DIM + half], old[:keep],
                           old[NOPE_DIM + half:QK_DIM], old[keep:NOPE_DIM], old[QK_DIM:]])


def _permute_heads(x):
    lead = x.shape[:-1]
    return x.reshape(lead + (-1, HEAD_PAD))[..., _head_lanes()].reshape(x.shape)


def _block(n, pref):
    for t in pref:
        if n % t == 0:
            return t
    return n


def kernel(x_prompt, x_sample, c_prompt, c_sample, cache_a_ckv, cache_a_krope, cache_b_k, cache_b_v, ada_w, ada_b, norm_mix_w, norm_ffn_w, a_w_in, a_q_lora_norm, a_kv_lora_norm, a_w_uq, a_w_ukv, a_q_norm, a_k_norm, a_w_o, kv_ada_w, kv_ada_b, kv_norm_w, b_w_kv, b_k_norm, b_w_q, b_q_norm, b_rel_bias, b_w_o, p_w_q, p_subkeys, p_u, p_v):
    D = D_MODEL
    Bp, Tp, _ = x_prompt.shape
    Bs, Ts, _ = x_sample.shape
    P = cache_a_ckv.shape[2]
    Pb = cache_b_k.shape[1]
    assert Ts == CHUNK and Pb == B_WINDOW and P % CHUNK == 0

    ada_all_w = jnp.concatenate([ada_w[0], ada_w[1], kv_ada_w], axis=1).astype(BF16)
    ada_all_b = jnp.concatenate([ada_b[0], ada_b[1], kv_ada_b])[None, :]
    c_all = jnp.concatenate([c_prompt, c_sample], axis=0)
    mod = _ada(c_all, ada_all_w, ada_all_b)

    w_in = a_w_in[0].astype(BF16)
    qln = a_q_lora_norm[0][None, :]
    kvln = a_kv_lora_norm[0][None, :]
    wuq_pad = _pad_heads(a_w_uq[0], A_HEADS, QK_DIM).astype(BF16)
    wukv = a_w_ukv[0].reshape(KV_LORA, A_HEADS, NOPE_DIM + V_DIM)
    wuk_pad = jnp.pad(wukv[:, :, :NOPE_DIM], ((0, 0), (0, 0), (0, HEAD_PAD - NOPE_DIM))
                      ).reshape(KV_LORA, A_HEADS * HEAD_PAD).astype(BF16)
    wuv_pad = jnp.pad(wukv[:, :, NOPE_DIM:], ((0, 0), (0, 0), (0, HEAD_PAD - V_DIM))
                      ).reshape(KV_LORA, A_HEADS * HEAD_PAD).astype(BF16)
    vone = jnp.asarray(np.tile(np.eye(1, HEAD_PAD, k=V_DIM), (1, A_HEADS)), F32)
    qn_pad = jnp.pad(a_q_norm[0], (0, HEAD_PAD - QK_DIM))[None, :]
    kn_pad = jnp.pad(a_k_norm[0], (0, HEAD_PAD - QK_DIM))[None, :]
    place = jnp.asarray(np.eye(ROPE_DIM, HEAD_PAD, k=NOPE_DIM), BF16)
    wuq_pad, wuk_pad, qn_pad, kn_pad, place = map(_permute_heads, (wuq_pad, wuk_pad, qn_pad, kn_pad, place))
    a_wo = a_w_o[0].astype(BF16)
    wk_b = b_w_kv[:, :D].astype(BF16)
    wv_b = b_w_kv[:, D:].astype(BF16)
    bkn_pair = jnp.tile(b_k_norm, 2)[None, :]
    bqn_pair = jnp.tile(b_q_norm[0], 2)[None, :]
    bwq = b_w_q[0].astype(BF16)
    bwo = b_w_o[0].astype(BF16)
    bias_tile = _band_bias(b_rel_bias[0])

    def peer_weights(layer):
        wq = p_w_q[layer].astype(BF16)
        sk = p_subkeys[layer]
        sk1 = jnp.pad(sk[:, 0], ((0, 0), (0, 0), (0, P_HALF))).astype(BF16)
        sk2 = jnp.pad(sk[:, 1], ((0, 0), (0, 0), (P_HALF, 0))).astype(BF16)
        vt = p_v[layer].astype(BF16).reshape(N_EXPERTS // PEER_EC, PEER_EC, D).transpose(0, 2, 1)
        return wq, sk1, sk2, p_u[layer].astype(BF16), vt

    peer_w = [peer_weights(0), peer_weights(1)]
    norm_mix = norm_mix_w[:, None, :]
    norm_ffn = norm_ffn_w[:, None, :]
    kv_nw = kv_norm_w[None, :]

    def run(x, modp, per_token, pos_q, past):
        B, T, _ = x.shape
        if per_token:
            modv = jnp.repeat(modp, T, axis=0)[None]
            xw = x.reshape(1, B * T, D)
        else:
            modv = modp[:, None, :]
            xw = x
        Bw, Tw, _ = xw.shape
        sl = lambda k: modv[:, :, k * D:(k + 1) * D]
        tb = _block(Tw, (512, 256, 128))
        tbp = _block(Tw, (256, 128))
        tbr = _block(Tw, (512, 256, 128))
        tbe = _block(Tw, (512, 256, 128))

        cos_q, sin_q = _rope_tables(pos_q)
        ckv, kr, q = _mla_pre(xw, sl(0), sl(1), norm_mix[0], w_in, qln, kvln, wuq_pad, qn_pad,
                              cos_q, sin_q, tbp)
        ckv = ckv.reshape(B, T, KV_LORA)
        kr = kr.reshape(B, T, ROPE_DIM)
        q = q.reshape(B, T, A_HEADS * HEAD_PAD)
        if past is None:
            ckv_all, kr_all = ckv, kr
            pos_k = pos_q[:T]
        else:
            ckv_all = jnp.concatenate([past[0], ckv], axis=1)
            kr_all = jnp.concatenate([past[1], kr], axis=1)
            pos_k = jnp.arange(P + T, dtype=jnp.int32)
        S = ckv_all.shape[1]
        cos_k, sin_k = _rope_tables(pos_k)
        tbk = _block(S, (512, 704, 256, 192, 64))
        k, v = _mla_kv(ckv_all, kr_all, wuk_pad, wuv_pad, vone, kn_pad, place, cos_k, sin_k, tbk)
        if past is None:
            ta = _block(T, (512, 256, 128, 64))
            o = _mla_attn(q, k, v, ta, ta, True)
        else:
            o = _mla_attn(q, k, v, T, tbk, False)
        o = o.reshape(Bw, Tw, A_HEADS * V_DIM)

        wq, sk1, sk2, u, vt = peer_w[0]
        x1, h2, rank, cnt, e2, r = _post_mix(o, xw, sl(2), sl(3), sl(4), norm_ffn[0], a_wo, wq, sk1, sk2, tbr)
        x2 = _peer_dense(h2, u, vt, rank, cnt, e2, r, x1, sl(5), tbe)

        tail = B_WINDOW if past is None else Tw
        kf, vf, kb, vb = _shared_kv(x2, sl(12), sl(13), kv_nw, wk_b, wv_b, bkn_pair, tb, tail)

        qb = _band_pre(x2, sl(6), sl(7), norm_mix[1], bwq, bqn_pair, tb)
        if past is None:
            ob = _band_attn(qb, kb, vb, bias_tile, B_WINDOW)
            new_bk = kf.reshape(B, B_WINDOW, B_HEADS, B_HEAD_DIM)
            new_bv = vf.reshape(B, B_WINDOW, B_HEADS, B_HEAD_DIM)
        else:
            zq = jnp.zeros((B, CHUNK, D), BF16)
            qpad = jnp.concatenate([zq, qb.reshape(B, T, D)], axis=1)
            kwin = jnp.concatenate([zq, past[2].reshape(B, Pb, D).astype(BF16), kb.reshape(B, T, D)], axis=1)
            vwin = jnp.concatenate([zq, past[3].reshape(B, Pb, D).astype(BF16), vb.reshape(B, T, D)], axis=1)
            ob = _band_step(qpad, kwin, vwin, bias_tile)[:, CHUNK:].reshape(Bw, Tw, D)
            new_bk = jnp.concatenate([past[2], kf.reshape(B, T, B_HEADS, B_HEAD_DIM)], axis=1)[:, -Pb:]
            new_bv = jnp.concatenate([past[3], vf.reshape(B, T, B_HEADS, B_HEAD_DIM)], axis=1)[:, -Pb:]

        wq, sk1, sk2, u, vt = peer_w[1]
        x3, h4, rank, cnt, e2, r = _post_mix(ob, x2, sl(8), sl(9), sl(10), norm_ffn[1], bwo, wq, sk1, sk2, tbr)
        y = _peer_dense(h4, u, vt, rank, cnt, e2, r, x3, sl(11), tbe)
        return y.reshape(B, T, D), ckv[None], kr[None], new_bk, new_bv

    pos_p = jnp.arange(Tp, dtype=jnp.int32)
    pos_s = jnp.tile(P + jnp.arange(Ts, dtype=jnp.int32), Bs)
    y_p, p_ckv, p_kr, p_bk, p_bv = run(x_prompt, mod[:Bp], False, pos_p, None)
    y_s, s_ckv, s_kr, s_bk, s_bv = run(x_sample, mod[Bp:], True, pos_s,
                                       (cache_a_ckv[0], cache_a_krope[0], cache_b_k, cache_b_v))
    return (y_p, y_s, p_ckv, p_kr, p_bk, p_bv, s_ckv, s_kr, s_bk, s_bv)
```

```python
import functools
import math

import numpy as np
import jax
import jax.numpy as jnp
from jax import lax
from jax.experimental import pallas as pl
from jax.experimental.pallas import tpu as pltpu

F32 = jnp.float32
BF16 = jnp.bfloat16

D_MODEL = 1024
CHUNK = 64
A_HEADS = 16
Q_LORA = 384
KV_LORA = 256
NOPE_DIM = 64
ROPE_DIM = 32
V_DIM = 64
QK_DIM = NOPE_DIM + ROPE_DIM
ROPE_THETA = 10000.0
B_HEADS = 16
B_HEAD_DIM = 64
LEFT_CHUNKS = 8
B_WINDOW = LEFT_CHUNKS * CHUNK
REL_CLIP = 128
P_HEADS = 8
N_KEYS = 128
N_EXPERTS = N_KEYS * N_KEYS
P_HALF = 64
P_TOPK = 16
NEG = -1e30
EPS = 1e-6

LANES = 128
BF16_ROWS = 16
HEAD_PAD = 128
BAND_SUB = 2 * CHUNK
BAND_WIN = B_WINDOW + BAND_SUB
VMEM_LIMIT = 56 * 1024 * 1024
PEER_EC = 512
PEER_SPLIT = 2
MXU_N = 256
GATE_LANES = 256

_NT = (((1,), (1,)), ((), ()))


def _cparams(sem):
    return pltpu.CompilerParams(dimension_semantics=sem, vmem_limit_bytes=VMEM_LIMIT)


def _modulate(x, w, shift, scale):
    ms = jnp.mean(x * x, axis=-1, keepdims=True)
    y = x * lax.rsqrt(ms + EPS) * w
    return y * (1.0 + scale) + shift


def _rms(x, w):
    ms = jnp.mean(x * x, axis=-1, keepdims=True)
    return x * lax.rsqrt(ms + EPS) * w


def _mod_spec(arr, tb):
    if arr.shape[1] == 1:
        return pl.BlockSpec((1, 1, arr.shape[2]), lambda b, i, *_: (b, 0, 0))
    return pl.BlockSpec((1, tb, arr.shape[2]), lambda b, i, *_: (b, i, 0))


def _full_spec(arr):
    nd = arr.ndim
    return pl.BlockSpec(arr.shape, lambda *_: (0,) * nd)


def _ada_kernel(c_ref, w_ref, b_ref, o_ref):
    c = c_ref[...]
    a = c / (1.0 + jnp.exp(-c))
    o_ref[...] = jnp.dot(a.astype(BF16), w_ref[...], preferred_element_type=F32) + b_ref[...]


def _ada(c, w, b, tn=2048):
    m, k = c.shape
    n = w.shape[1]
    return pl.pallas_call(
        _ada_kernel,
        grid=(n // tn,),
        in_specs=[pl.BlockSpec((m, k), lambda j: (0, 0)),
                  pl.BlockSpec((k, tn), lambda j: (0, j)),
                  pl.BlockSpec((1, tn), lambda j: (0, j))],
        out_specs=pl.BlockSpec((m, tn), lambda j: (0, j)),
        out_shape=jax.ShapeDtypeStruct((m, n), F32),
        compiler_params=_cparams(("arbitrary",)),
        name="ada_mod",
    )(c, w, b)


def _rope_head(xh, cos, sin, lane):
    del lane
    return xh * cos + pltpu.roll(xh, LANES // 2, 1) * sin


def _mla_pre_kernel(x_ref, sh_ref, sc_ref, nw_ref, win_ref, qln_ref, kvln_ref, wuq_ref, qn_ref,
                    cos_ref, sin_ref, ckv_ref, kr_ref, q_ref):
    x = x_ref[0]
    h = _modulate(x, nw_ref[...], sh_ref[0], sc_ref[0])
    proj = jnp.dot(h.astype(BF16), win_ref[...], preferred_element_type=F32)
    c_q = _rms(proj[:, :Q_LORA], qln_ref[...])
    ckv_ref[0] = _rms(proj[:, Q_LORA:Q_LORA + KV_LORA], kvln_ref[...])
    kr_ref[0] = proj[:, Q_LORA + KV_LORA:]
    q = jnp.dot(c_q.astype(BF16), wuq_ref[...], preferred_element_type=F32)
    cos = cos_ref[...]
    sin = sin_ref[...]
    qn = qn_ref[...]
    lane = lax.broadcasted_iota(jnp.int32, cos.shape, 1)
    scale = QK_DIM ** -0.5
    for hd in range(A_HEADS):
        qh = q[:, hd * HEAD_PAD:(hd + 1) * HEAD_PAD]
        ss = jnp.sum(qh * qh, axis=-1, keepdims=True) * (1.0 / QK_DIM)
        qh = qh * lax.rsqrt(ss + EPS) * qn
        qh = _rope_head(qh, cos, sin, lane)
        q_ref[0, :, hd * HEAD_PAD:(hd + 1) * HEAD_PAD] = (qh * scale).astype(BF16)


def _mla_pre(x, shift, scale, nw, w_in, qln, kvln, wuq_pad, qn_pad, cos_t, sin_t, tb):
    B, T, D = x.shape
    nT = T // tb
    tab_spec = pl.BlockSpec((tb, LANES), lambda b, i: (i, 0))
    return pl.pallas_call(
        _mla_pre_kernel,
        grid=(B, nT),
        in_specs=[pl.BlockSpec((1, tb, D), lambda b, i: (b, i, 0)),
                  _mod_spec(shift, tb), _mod_spec(scale, tb),
                  _full_spec(nw), _full_spec(w_in), _full_spec(qln), _full_spec(kvln),
                  _full_spec(wuq_pad), _full_spec(qn_pad), tab_spec, tab_spec],
        out_specs=[pl.BlockSpec((1, tb, KV_LORA), lambda b, i: (b, i, 0)),
                   pl.BlockSpec((1, tb, ROPE_DIM), lambda b, i: (b, i, 0)),
                   pl.BlockSpec((1, tb, A_HEADS * HEAD_PAD), lambda b, i: (b, i, 0))],
        out_shape=[jax.ShapeDtypeStruct((B, T, KV_LORA), F32),
                   jax.ShapeDtypeStruct((B, T, ROPE_DIM), F32),
                   jax.ShapeDtypeStruct((B, T, A_HEADS * HEAD_PAD), BF16)],
        compiler_params=_cparams(("parallel", "parallel")),
        name="mla_pre",
    )(x, shift, scale, nw, w_in, qln, kvln, wuq_pad, qn_pad, cos_t, sin_t)


def _mla_kv_kernel(ckv_ref, kr_ref, wuk_ref, wuv_ref, vone_ref, kn_ref, place_ref, cos_ref, sin_ref,
                   k_ref, v_ref):
    ckv = ckv_ref[0].astype(BF16)
    kn = jnp.dot(ckv, wuk_ref[...], preferred_element_type=F32)
    v_ref[0] = (jnp.dot(ckv, wuv_ref[...], preferred_element_type=F32) + vone_ref[...]).astype(BF16)
    kr = kr_ref[0]
    kr_hi = kr.astype(BF16)
    kr_lo = (kr - kr_hi.astype(F32)).astype(BF16)
    place = place_ref[...]
    krp = (jnp.dot(kr_hi, place, preferred_element_type=F32)
           + jnp.dot(kr_lo, place, preferred_element_type=F32))
    cos = cos_ref[...]
    sin = sin_ref[...]
    knw = kn_ref[...]
    lane = lax.broadcasted_iota(jnp.int32, cos.shape, 1)
    for hd in range(A_HEADS):
        kh = kn[:, hd * HEAD_PAD:(hd + 1) * HEAD_PAD] + krp
        ss = jnp.sum(kh * kh, axis=-1, keepdims=True) * (1.0 / QK_DIM)
        kh = kh * lax.rsqrt(ss + EPS) * knw
        kh = _rope_head(kh, cos, sin, lane)
        k_ref[0, :, hd * HEAD_PAD:(hd + 1) * HEAD_PAD] = kh.astype(BF16)


def _mla_kv(ckv, kr, wuk_pad, wuv_pad, vone, kn_pad, place, cos_t, sin_t, tb):
    B, S, _ = ckv.shape
    tab_spec = pl.BlockSpec((tb, LANES), lambda b, i: (i, 0))
    wide = pl.BlockSpec((1, tb, A_HEADS * HEAD_PAD), lambda b, i: (b, i, 0))
    return pl.pallas_call(
        _mla_kv_kernel,
        grid=(B, S // tb),
        in_specs=[pl.BlockSpec((1, tb, KV_LORA), lambda b, i: (b, i, 0)),
                  pl.BlockSpec((1, tb, ROPE_DIM), lambda b, i: (b, i, 0)),
                  _full_spec(wuk_pad), _full_spec(wuv_pad), _full_spec(vone), _full_spec(kn_pad),
                  _full_spec(place), tab_spec, tab_spec],
        out_specs=[wide, wide],
        out_shape=[jax.ShapeDtypeStruct((B, S, A_HEADS * HEAD_PAD), BF16),
                   jax.ShapeDtypeStruct((B, S, A_HEADS * HEAD_PAD), BF16)],
        compiler_params=_cparams(("parallel", "parallel")),
        name="mla_kv",
    )(ckv, kr, wuk_pad, wuv_pad, vone, kn_pad, place, cos_t, sin_t)


def _mla_attn_kernel(q_ref, k_ref, v_ref, o_ref, m_sc, acc_sc, *, tq, tk, causal, nk):
    i = pl.program_id(2)
    m_sc[...] = jnp.full(m_sc.shape, -jnp.inf, F32)
    acc_sc[...] = jnp.zeros(acc_sc.shape, F32)

    def block(start, masked):
        k = k_ref[0, pl.ds(start, tk), :]
        v = v_ref[0, pl.ds(start, tk), :]
        if masked:
            qc = lax.broadcasted_iota(jnp.int32, (tq, tk), 0) // CHUNK
            kc = lax.broadcasted_iota(jnp.int32, (tq, tk), 1) // CHUNK
            allowed = kc <= qc
        for hh in range(2):
            qh = q_ref[0, :, hh * HEAD_PAD:(hh + 1) * HEAD_PAD]
            s = lax.dot_general(qh, k[:, hh * HEAD_PAD:(hh + 1) * HEAD_PAD], _NT,
                                preferred_element_type=F32)
            if masked:
                s = jnp.where(allowed, s, NEG)
            m_prev = m_sc[hh]
            m_new = jnp.maximum(m_prev, jnp.max(s, axis=-1, keepdims=True))
            alpha = jnp.exp(m_prev - m_new)
            if tk % LANES == 0:
                m_wide = jnp.concatenate([m_new] * (tk // LANES), axis=1)
            else:
                m_wide = m_new[:, :1]
            p = jnp.exp((s - m_wide).astype(BF16))
            acc_sc[hh] = alpha * acc_sc[hh] + jnp.dot(p, v[:, hh * HEAD_PAD:(hh + 1) * HEAD_PAD],
                                                      preferred_element_type=F32)
            m_sc[hh] = m_new

    def body(j, carry):
        block(pl.multiple_of(j * tk, tk), False)
        return carry

    if causal:
        lax.fori_loop(0, i, body, 0)
        block(pl.multiple_of(i * tk, tk), True)
    else:
        lax.fori_loop(0, nk, body, 0)

    lane = lax.broadcasted_iota(jnp.int32, (tq, LANES), 1)
    outs = []
    for hh in range(2):
        acc = acc_sc[hh]
        outs.append(acc / acc[:, V_DIM:V_DIM + 1])
    o = jnp.where(lane < V_DIM, outs[0], pltpu.roll(outs[1], V_DIM, 1))
    o_ref[0] = o.astype(BF16)


def _mla_attn(q, k, v, tq, tk, causal):
    B, T, _ = q.shape
    S = k.shape[1]
    nq, nk = T // tq, S // tk
    assert T % tq == 0 and S % tk == 0 and (not causal or (tq == tk and T == S))
    kv_spec = pl.BlockSpec((1, S, 2 * HEAD_PAD), lambda b, hp, i: (b, 0, hp))
    return pl.pallas_call(
        functools.partial(_mla_attn_kernel, tq=tq, tk=tk, causal=causal, nk=nk),
        grid=(B, A_HEADS // 2, nq),
        in_specs=[pl.BlockSpec((1, tq, 2 * HEAD_PAD), lambda b, hp, i: (b, i, hp)), kv_spec, kv_spec],
        out_specs=pl.BlockSpec((1, tq, 2 * V_DIM), lambda b, hp, i: (b, i, hp)),
        out_shape=jax.ShapeDtypeStruct((B, T, A_HEADS * V_DIM), BF16),
        scratch_shapes=[pltpu.VMEM((2, tq, LANES), F32), pltpu.VMEM((2, tq, LANES), F32)],
        compiler_params=_cparams(("parallel", "parallel", "arbitrary")),
        name="mla_attn",
    )(q, k, v)


def _odd_even_merge_sort_pairs(n):
    pairs = []

    def merge(lo, m, r):
        step = r * 2
        if step < m:
            merge(lo, m, step)
            merge(lo + r, m, step)
            pairs.extend((i, i + r) for i in range(lo + r, lo + m - r, step))
        else:
            pairs.append((lo, lo + r))

    def sort(lo, m):
        if m > 1:
            sort(lo, m // 2)
            sort(lo + m // 2, m // 2)
            merge(lo, m, 1)

    sort(0, n)
    return pairs


_SORT16 = _odd_even_merge_sort_pairs(P_TOPK)
_BITONIC16 = [(i, i + d) for d in (8, 4, 2, 1) for i in range(P_TOPK) if not i & d]
SUBLANES = 8


def _top16_rows(chains):
    assert all(s.shape[0] == P_TOPK * SUBLANES for s, _, _ in chains)
    vs = [[s[SUBLANES * j:SUBLANES * (j + 1), :] for j in range(P_TOPK)] for s, _, _ in chains]

    def compare_exchange(pairs):
        for i, j in pairs:
            for v in vs:
                v[i], v[j] = jnp.maximum(v[i], v[j]), jnp.minimum(v[i], v[j])

    compare_exchange(_SORT16)
    for shift in (4, 2, 1):
        for v in vs:
            other = [pltpu.roll(x, shift, 0) for x in v]
            v[:] = [jnp.maximum(v[k], other[P_TOPK - 1 - k]) for k in range(P_TOPK)]
        compare_exchange(_BITONIC16)
    for v, (_, t_sc, cols) in zip(vs, chains):
        for k in range(P_TOPK):
            t_sc[k:k + 1, cols] = v[k][0:1, :]


def _prefix_count(test, rows):
    assert len(rows) == P_TOPK
    m = [test(rows[4 * q + 3]) for q in range(4)]
    base = jnp.where(m[0], 4.0, 0.0)
    for q in range(1, 4):
        base = jnp.where(m[q], 4.0 * (q + 1), base)
    sub = jnp.zeros_like(base)
    for j in range(3):
        t_sel = jnp.where(m[2], rows[12 + j], jnp.where(m[1], rows[8 + j], jnp.where(m[0], rows[4 + j], rows[j])))
        sub = jnp.where(test(t_sel), float(j + 1), sub)
    return jnp.minimum(base + sub, float(P_TOPK))


def _route_tiles(tiles, t1_sc, t2_sc):
    _top16_rows([(s1, t1_sc, cols) for s1, _, cols in tiles] + [(s2, t2_sc, cols) for _, s2, cols in tiles])
    state = []
    for s1, s2, cols in tiles:
        row8 = lax.broadcasted_iota(jnp.int32, (8, s1.shape[1]), 0)
        t1 = [t1_sc[k:k + 1, cols] for k in range(P_TOPK)]
        t2 = [t2_sc[k:k + 1, cols] for k in range(P_TOPK)]
        t2_lo = t2_sc[0:8, cols]
        t2_hi = t2_sc[8:16, cols]
        t1_hi = t1_sc[8:16, cols]
        cands = [t1[0] + t2_lo, t1[0] + t2_hi, t1_hi + t2[0]]
        for k1 in range(1, 8):
            lim = P_TOPK // (k1 + 1)
            c = t1[k1] + t2_lo
            cands.append(c if lim >= 8 else jnp.where(row8 < lim, c, -jnp.inf))
        top = t1[0] + t2[0]
        state.append(dict(t1=t1, t2=t2, cands=cands, top=top, z=jnp.zeros_like(top), tau=top))
    for k in range(P_TOPK):
        for st in state:
            m = st["cands"][0]
            for c in st["cands"][1:]:
                m = jnp.maximum(m, c)
            m = jnp.max(m, axis=0, keepdims=True)
            st["z"] = st["z"] + jnp.exp(m - st["top"])
            st["tau"] = m
            if k + 1 < P_TOPK:
                st["cands"] = [jnp.where(c == m, -jnp.inf, c) for c in st["cands"]]
    outs = []
    for (s1, s2, cols), st in zip(tiles, state):
        tau = st["tau"]
        cnt = _prefix_count(lambda t: s1 + t >= tau, st["t2"])
        rank = _prefix_count(lambda t: t > s2, st["t2"])
        n1 = jnp.sum(jnp.where(s1 >= st["t1"][P_TOPK - 1], 1.0, 0.0), axis=0, keepdims=True)
        n2 = jnp.sum(jnp.where(s2 >= st["t2"][P_TOPK - 1], 1.0, 0.0), axis=0, keepdims=True)
        npair = jnp.sum(cnt, axis=0, keepdims=True)
        k = float(P_TOPK)
        tied = jnp.where((n1 != k) | (n2 != k) | (npair != k), 1.0, 0.0)
        outs.append((rank, cnt, jnp.exp(s2 - st["t2"][0]), jnp.exp(s1 - st["t1"][0]) / st["z"], tied))
    return outs


def _route_tile_exact(s1, s2, t1_sc, t2_sc, cols):
    nk, w = s1.shape
    key = lax.broadcasted_iota(jnp.int32, (nk, w), 0).astype(F32)

    def top16(s, t_sc):
        work, rank, tops = s, jnp.full(s.shape, float(P_TOPK), F32), []
        for k in range(P_TOPK):
            m = jnp.max(work, axis=0, keepdims=True)
            first = jnp.min(jnp.where(work == m, key, float(nk)), axis=0, keepdims=True)
            hit = key == first
            rank = jnp.where(hit, float(k), rank)
            work = jnp.where(hit, -jnp.inf, work)
            t_sc[k:k + 1, cols] = m
            tops.append(m)
        return tops, rank

    t1, rank1 = top16(s1, t1_sc)
    t2, rank2 = top16(s2, t2_sc)
    row8 = lax.broadcasted_iota(jnp.int32, (8, w), 0).astype(F32)
    t2_lo = t2_sc[0:8, cols]
    t2_hi = t2_sc[8:16, cols]
    t1_hi = t1_sc[8:16, cols]
    cands = [(t1[0] + t2_lo, row8), (t1[0] + t2_hi, row8 + 8.0), (t1_hi + t2[0], (row8 + 8.0) * P_TOPK)]
    for k1 in range(1, 8):
        lim = P_TOPK // (k1 + 1)
        c = t1[k1] + t2_lo
        cands.append((c if lim >= 8 else jnp.where(row8 < lim, c, -jnp.inf), row8 + float(k1 * P_TOPK)))
    top = t1[0] + t2[0]
    z = jnp.zeros_like(top)
    k1_row = lax.broadcasted_iota(jnp.int32, (P_TOPK, w), 0).astype(F32)
    cnt_k1 = jnp.zeros((P_TOPK, w), F32)
    for k in range(P_TOPK):
        m = cands[0][0]
        for c, _ in cands[1:]:
            m = jnp.maximum(m, c)
        m = jnp.max(m, axis=0, keepdims=True)
        first = None
        for c, pos in cands:
            f = jnp.min(jnp.where(c == m, pos, float(P_TOPK * P_TOPK)), axis=0, keepdims=True)
            first = f if first is None else jnp.minimum(first, f)
        z = z + jnp.exp(m - top)
        cnt_k1 = cnt_k1 + jnp.where(k1_row == jnp.floor(first * (1.0 / P_TOPK)), 1.0, 0.0)
        cands = [(jnp.where(pos == first, -jnp.inf, c), pos) for c, pos in cands]
    t1_sc[:, cols] = cnt_k1
    cnt = jnp.zeros_like(s1)
    for k1 in range(P_TOPK):
        cnt = jnp.where(rank1 == float(k1), t1_sc[k1:k1 + 1, cols], cnt)
    return rank2, cnt, jnp.exp(s1 - t1[0]) / z


def _post_mix_kernel(o_ref, x_ref, g_ref, sh_ref, sc_ref, nw_ref, wo_ref, wq_ref, sk1_ref, sk2_ref,
                     x1_ref, h2_ref, rank_ref, cnt_ref, e2_ref, r_ref, t1_sc, t2_sc, s_sc, tied_sc):
    mix = jnp.dot(o_ref[0], wo_ref[...], preferred_element_type=F32)
    x1 = x_ref[0] + g_ref[0] * mix
    x1_ref[0] = x1
    h2 = _modulate(x1, nw_ref[...], sh_ref[0], sc_ref[0]).astype(BF16)
    h2_ref[0] = h2
    tb = h2.shape[0]
    q = jnp.dot(h2, wq_ref[...], preferred_element_type=F32).astype(BF16)
    for h in range(P_HEADS):
        qh = q[:, h * 2 * P_HALF:(h + 1) * 2 * P_HALF]
        s_sc[h, 0] = lax.dot_general(sk1_ref[h], qh, _NT, preferred_element_type=F32)
        s_sc[h, 1] = lax.dot_general(sk2_ref[h], qh, _NT, preferred_element_type=F32)

    def head_body(h, carry):
        s1 = s_sc[h, 0]
        s2 = s_sc[h, 1]
        lane_tiles = [slice(nt * LANES, (nt + 1) * LANES) for nt in range(tb // LANES)]
        routed = _route_tiles([(s1[:, cols], s2[:, cols], cols) for cols in lane_tiles], t1_sc, t2_sc)
        for cols, (rank, cnt, e2, r, tied) in zip(lane_tiles, routed):
            rank_ref[h, :, cols] = rank.astype(BF16)
            cnt_ref[h, :, cols] = cnt
            e2_ref[h, :, cols] = e2.astype(BF16)
            r_ref[h, :, cols] = r
            tied_sc[h, :, cols] = jnp.broadcast_to(tied, (8, tied.shape[1]))
        return carry

    lax.fori_loop(0, P_HEADS, head_body, 0)

    @pl.when(jnp.max(tied_sc[...]) > 0.0)
    def _():
        def fix_head(h, carry):
            for nt in range(tb // LANES):
                cols = slice(nt * LANES, (nt + 1) * LANES)

                @pl.when(jnp.max(tied_sc[h, :, cols]) > 0.0)
                def _(cols=cols):
                    rank_x, cnt_x, r_x = _route_tile_exact(s_sc[h, 0, :, cols], s_sc[h, 1, :, cols],
                                                           t1_sc, t2_sc, cols)
                    rank_ref[h, :, cols] = rank_x.astype(BF16)
                    cnt_ref[h, :, cols] = cnt_x
                    r_ref[h, :, cols] = r_x
            return carry

        lax.fori_loop(0, P_HEADS, fix_head, 0)


def _post_mix(o, x, gate, shift, scale, nw, w_o, wq_heads, sk1, sk2, tb):
    B, T, D = x.shape
    nT = T // tb
    n = B * T
    tok = lambda b, i: (b, i, 0)
    rt_spec = pl.BlockSpec((P_HEADS, N_KEYS, tb), lambda b, i: (0, 0, b * nT + i))
    rt_shape = jax.ShapeDtypeStruct((P_HEADS, N_KEYS, n), F32)
    rt_shape_b = jax.ShapeDtypeStruct((P_HEADS, N_KEYS, n), BF16)
    return pl.pallas_call(
        _post_mix_kernel,
        grid=(B, nT),
        in_specs=[pl.BlockSpec((1, tb, o.shape[2]), tok), pl.BlockSpec((1, tb, D), tok),
                  _mod_spec(gate, tb), _mod_spec(shift, tb), _mod_spec(scale, tb),
                  _full_spec(nw), _full_spec(w_o), _full_spec(wq_heads), _full_spec(sk1), _full_spec(sk2)],
        out_specs=[pl.BlockSpec((1, tb, D), tok), pl.BlockSpec((1, tb, D), tok),
                   rt_spec, rt_spec, rt_spec, rt_spec],
        out_shape=[jax.ShapeDtypeStruct((B, T, D), F32), jax.ShapeDtypeStruct((B, T, D), BF16),
                   rt_shape_b, rt_shape, rt_shape_b, rt_shape],
        scratch_shapes=[pltpu.VMEM((P_TOPK, tb), F32), pltpu.VMEM((P_TOPK, tb), F32),
                        pltpu.VMEM((P_HEADS, 2, N_KEYS, tb), F32), pltpu.VMEM((P_HEADS, 8, tb), F32)],
        compiler_params=_cparams(("parallel", "parallel")),
        name="post_mix_route",
    )(o, x, gate, shift, scale, nw, w_o, wq_heads, sk1, sk2)


def _gelu_tanh(a):
    c = math.sqrt(2.0 / math.pi)
    return (0.5 * a) * (1.0 + jnp.tanh(a * ((0.044715 * c) * (a * a) + c)))


def _peer_dense_kernel(h_ref, u_ref, vt_ref, rank_ref, cnt_ref, e2_ref, r_ref, x_ref, g_ref,
                       o_ref, acc_ref, a0_ref, a1_ref, hc0_ref, hc1_ref, *, ec, ne):
    g = pl.program_id(0)
    per = ec // N_KEYS
    s = lax.rem(jnp.maximum(g - 1, 0), ne) + 1
    c3 = lax.rem(jnp.maximum(g - 2, 0), ne)

    @pl.when(g == 0)
    def _():
        for ref in (a0_ref, a1_ref, hc0_ref, hc1_ref):
            ref[...] = jnp.zeros(ref.shape, ref.dtype)

    @pl.when(c3 == 0)
    def _():
        acc_ref[...] = jnp.zeros(acc_ref.shape, F32)

    def stages(a_out, a_in, hc_out, hc_in):
        zero = jnp.zeros((), BF16)
        tb = h_ref.shape[1]
        nsplit = PEER_SPLIT
        gate_w = min(GATE_LANES, tb)
        mxu_w = min(MXU_N, tb)
        assert tb % gate_w == 0 and tb % mxu_w == 0

        row_cache = {}
        reps = N_KEYS // BF16_ROWS

        def routed_rows(ii):
            if ii not in row_cache:
                i1 = jnp.clip((s - 1) * per + ii, 0, N_KEYS - 1)
                wide = lambda ref, h: jnp.broadcast_to(ref[h, pl.ds(i1, 1), :], (BF16_ROWS, tb)).astype(BF16)
                row_cache[ii] = [(wide(cnt_ref, h), wide(r_ref, h)) for h in range(P_HEADS)]
            return row_cache[ii]

        def gate_piece(ii, n):
            cols = slice(n * gate_w, (n + 1) * gate_w)
            rows = slice(ii * N_KEYS, (ii + 1) * N_KEYS)
            w = None
            for h, (cnt_row, r_row) in enumerate(routed_rows(ii)):
                cnt_t = jnp.concatenate([cnt_row[:, cols]] * reps, axis=0)
                r_t = jnp.concatenate([r_row[:, cols]] * reps, axis=0)
                contrib = jnp.where(rank_ref[h, :, cols] < cnt_t, e2_ref[h, :, cols], zero) * r_t
                w = contrib if w is None else w + contrib
            hc_out[rows, cols] = _gelu_tanh(a_in[rows, cols]) * w

        def score_piece(q, n):
            cols = slice(n * mxu_w, (n + 1) * mxu_w)
            rows = slice(q * (ec // nsplit), (q + 1) * (ec // nsplit))
            a_out[rows, cols] = lax.dot_general(u_ref[rows, :], h_ref[0, cols, :], _NT,
                                                preferred_element_type=F32).astype(BF16)

        def out_piece(q, n):
            cols = slice(n * mxu_w, (n + 1) * mxu_w)
            rows = slice(q * (D_MODEL // nsplit), (q + 1) * (D_MODEL // nsplit))
            acc_ref[rows, cols] += jnp.dot(vt_ref[0, rows, :], hc_in[:, cols], preferred_element_type=F32)

        vec = [functools.partial(gate_piece, ii, n) for n in range(tb // gate_w) for ii in range(per)]
        mxu = []
        for n in range(tb // mxu_w):
            for q in range(nsplit):
                mxu.append(functools.partial(score_piece, q, n))
                mxu.append(functools.partial(out_piece, q, n))
        for k in range(max(len(vec), len(mxu))):
            if k < len(mxu):
                mxu[k]()
            if k < len(vec):
                vec[k]()

    even = lax.rem(g, 2) == 0
    pl.when(even)(lambda: stages(a0_ref, a1_ref, hc1_ref, hc0_ref))
    pl.when(jnp.logical_not(even))(lambda: stages(a1_ref, a0_ref, hc0_ref, hc1_ref))

    @pl.when((c3 == ne - 1) & (g >= 2))
    def _():
        o_ref[0] = x_ref[0] + g_ref[0] * acc_ref[...].T


def _peer_dense(h2, u, vt, rank, cnt, e2, r, x, gate, tb, ec=PEER_EC):
    B, T, D = x.shape
    nT = T // tb
    nblk = B * nT
    ne = N_EXPERTS // ec
    assert vt.shape == (ne, D, ec)
    blk1 = lambda g: jnp.minimum(g // ne, nblk - 1)
    blk2 = lambda g: jnp.minimum(jnp.maximum(g - 1, 0) // ne, nblk - 1)
    blk3 = lambda g: jnp.maximum(g - 2, 0) // ne
    tok1 = lambda g: (blk1(g) // nT, blk1(g) % nT, 0)
    tok3 = lambda g: (blk3(g) // nT, blk3(g) % nT, 0)
    rt_spec = pl.BlockSpec((P_HEADS, N_KEYS, tb), lambda g: (0, 0, blk2(g)))
    if gate.shape[1] == 1:
        gate_spec = pl.BlockSpec((1, 1, D), lambda g: (blk3(g) // nT, 0, 0))
    else:
        gate_spec = pl.BlockSpec((1, tb, D), tok3)
    return pl.pallas_call(
        functools.partial(_peer_dense_kernel, ec=ec, ne=ne),
        grid=(nblk * ne + 2,),
        in_specs=[pl.BlockSpec((1, tb, D), tok1),
                  pl.BlockSpec((ec, D), lambda g: (g % ne, 0)),
                  pl.BlockSpec((1, D, ec), lambda g: (jnp.maximum(g - 2, 0) % ne, 0, 0)),
                  rt_spec, rt_spec, rt_spec, rt_spec,
                  pl.BlockSpec((1, tb, D), tok3), gate_spec],
        out_specs=pl.BlockSpec((1, tb, D), tok3),
        out_shape=jax.ShapeDtypeStruct((B, T, D), F32),
        scratch_shapes=[pltpu.VMEM((D, tb), F32),
                        pltpu.VMEM((ec, tb), BF16), pltpu.VMEM((ec, tb), BF16),
                        pltpu.VMEM((ec, tb), BF16), pltpu.VMEM((ec, tb), BF16)],
        compiler_params=_cparams(("arbitrary",)),
        name="peer_dense",
    )(h2, u, vt, rank, cnt, e2, r, x, gate)


def _pair_rms(y, w, lane):
    y2 = y * y
    lo = jnp.sum(jnp.where(lane < B_HEAD_DIM, y2, 0.0), axis=-1, keepdims=True)
    hi = jnp.sum(y2, axis=-1, keepdims=True) - lo
    ms = jnp.where(lane < B_HEAD_DIM, lo, hi) * (1.0 / B_HEAD_DIM)
    return y * lax.rsqrt(ms + EPS) * w


def _shared_kv_kernel(x_ref, sh_ref, sc_ref, nw_ref, wk_ref, wv_ref, kn_ref, k_ref, v_ref, kb_ref, vb_ref,
                      *, first_tail):
    h = _modulate(x_ref[0], nw_ref[...], sh_ref[0], sc_ref[0]).astype(BF16)
    kraw = jnp.dot(h, wk_ref[...], preferred_element_type=F32)
    v = jnp.dot(h, wv_ref[...], preferred_element_type=F32)
    vb_ref[0] = v.astype(BF16)
    knw = kn_ref[...]
    lane = lax.broadcasted_iota(jnp.int32, (h.shape[0], LANES), 1)
    ks = [_pair_rms(kraw[:, hp * LANES:(hp + 1) * LANES], knw, lane) for hp in range(B_HEADS // 2)]
    for hp, kh in enumerate(ks):
        kb_ref[0, :, hp * LANES:(hp + 1) * LANES] = kh.astype(BF16)

    @pl.when(pl.program_id(1) >= first_tail)
    def _():
        v_ref[0] = v
        for hp, kh in enumerate(ks):
            k_ref[0, :, hp * LANES:(hp + 1) * LANES] = kh


def _shared_kv(x, shift, scale, nw, wk, wv, kn_pair, tb, tail):
    B, T, D = x.shape
    nT = T // tb
    assert tail % tb == 0 and tail <= T
    first_tail = nT - tail // tb
    tok = lambda b, i: (b, i, 0)
    blk = pl.BlockSpec((1, tb, D), tok)
    tail_blk = pl.BlockSpec((1, tb, D), lambda b, i: (b, jnp.maximum(i - first_tail, 0), 0))
    return pl.pallas_call(
        functools.partial(_shared_kv_kernel, first_tail=first_tail),
        grid=(B, nT),
        in_specs=[blk, _mod_spec(shift, tb), _mod_spec(scale, tb), _full_spec(nw),
                  _full_spec(wk), _full_spec(wv), _full_spec(kn_pair)],
        out_specs=[tail_blk, tail_blk, blk, blk],
        out_shape=[jax.ShapeDtypeStruct((B, tail, D), F32), jax.ShapeDtypeStruct((B, tail, D), F32),
                   jax.ShapeDtypeStruct((B, T, D), BF16), jax.ShapeDtypeStruct((B, T, D), BF16)],
        compiler_params=_cparams(("parallel", "arbitrary")),
        name="shared_kv",
    )(x, shift, scale, nw, wk, wv, kn_pair)


def _band_pre_kernel(x_ref, sh_ref, sc_ref, nw_ref, wq_ref, qn_ref, q_ref):
    h = _modulate(x_ref[0], nw_ref[...], sh_ref[0], sc_ref[0]).astype(BF16)
    q = jnp.dot(h, wq_ref[...], preferred_element_type=F32)
    qnw = qn_ref[...]
    lane = lax.broadcasted_iota(jnp.int32, (h.shape[0], LANES), 1)
    scale = B_HEAD_DIM ** -0.5
    for hp in range(B_HEADS // 2):
        qh = _pair_rms(q[:, hp * LANES:(hp + 1) * LANES], qnw, lane)
        q_ref[0, :, hp * LANES:(hp + 1) * LANES] = (qh * scale).astype(BF16)


def _band_pre(x, shift, scale, nw, wq, qn_pair, tb):
    B, T, D = x.shape
    tok = lambda b, i: (b, i, 0)
    blk = pl.BlockSpec((1, tb, D), tok)
    return pl.pallas_call(
        _band_pre_kernel,
        grid=(B, T // tb),
        in_specs=[blk, _mod_spec(shift, tb), _mod_spec(scale, tb), _full_spec(nw),
                  _full_spec(wq), _full_spec(qn_pair)],
        out_specs=blk,
        out_shape=jax.ShapeDtypeStruct((B, T, D), BF16),
        compiler_params=_cparams(("parallel", "parallel")),
        name="band_pre",
    )(x, shift, scale, nw, wq, qn_pair)


def _band_bias_kernel(tab_ref, o_ref):
    h = pl.program_id(0)
    nvar = 2 * LANES
    r = lax.broadcasted_iota(jnp.int32, (BAND_SUB, nvar), 0)
    w = lax.broadcasted_iota(jnp.int32, (BAND_SUB, nvar), 1) + (BAND_WIN - nvar)
    idx = jnp.clip(r + B_WINDOW - w, -REL_CLIP, REL_CLIP) + REL_CLIP
    far = tab_ref[h, 2 * REL_CLIP]

    def body(t, acc):
        return jnp.where(idx == t, tab_ref[h, t], acc)

    var = lax.fori_loop(0, 2 * REL_CLIP, body, jnp.full((BAND_SUB, nvar), far, F32))
    full = jnp.concatenate([jnp.full((BAND_SUB, BAND_WIN - nvar), far, F32), var], axis=1)
    rr = lax.broadcasted_iota(jnp.int32, (BAND_SUB, BAND_WIN), 0)
    ww = lax.broadcasted_iota(jnp.int32, (BAND_SUB, BAND_WIN), 1)
    qc = rr // CHUNK + LEFT_CHUNKS
    kc = ww // CHUNK
    allowed = (kc <= qc) & (kc >= qc - LEFT_CHUNKS)
    o_ref[0] = jnp.where(allowed, full, NEG)


def _band_bias(table):
    nh = table.shape[0]
    return pl.pallas_call(
        _band_bias_kernel,
        grid=(nh,),
        in_specs=[pl.BlockSpec(memory_space=pltpu.SMEM)],
        out_specs=pl.BlockSpec((1, BAND_SUB, BAND_WIN), lambda h: (h, 0, 0)),
        out_shape=jax.ShapeDtypeStruct((nh, BAND_SUB, BAND_WIN), F32),
        compiler_params=_cparams(("arbitrary",)),
        name="band_bias",
    )(table)


def _band_windows(windows, bias_ref, lane):
    chains = [(w, hh) for w in range(len(windows)) for hh in range(2)]
    scores = []
    for w, hh in chains:
        qs, kw, _ = windows[w]
        nkw = kw.shape[0]
        sel = (lane < B_HEAD_DIM) if hh == 0 else (lane >= B_HEAD_DIM)
        qh = jnp.where(sel, qs, jnp.zeros_like(qs))
        s = lax.dot_general(qh, kw, _NT, preferred_element_type=F32)
        scores.append(s + bias_ref[hh, :, BAND_WIN - nkw:BAND_WIN])
    maxes = [jnp.max(s, axis=-1, keepdims=True) for s in scores]
    probs = [jnp.exp((s - m).astype(BF16)) for s, m in zip(scores, maxes)]
    vext = []
    for _, _, vw in windows:
        ones_lane = (lax.broadcasted_iota(jnp.int32, vw.shape, 1) == 0).astype(BF16)
        vext.append(jnp.concatenate([vw, ones_lane], axis=1))
    outs = []
    for (w, hh), p in zip(chains, probs):
        pv = jnp.dot(p, vext[w], preferred_element_type=F32)
        outs.append(pv[:, :LANES] / pv[:, LANES:LANES + 1])
    return [jnp.where(lane < B_HEAD_DIM, outs[2 * w], outs[2 * w + 1]) for w in range(len(windows))]


def _band_attn_kernel(q_ref, kp_ref, kc_ref, vp_ref, vc_ref, bias_ref, o_ref, *, tq):
    i = pl.program_id(2)
    lane = lax.broadcasted_iota(jnp.int32, (BAND_SUB, LANES), 1)
    nsub = tq // BAND_SUB

    def run(first):
        windows = []
        for c in range(nsub):
            qs = q_ref[0, c * BAND_SUB:(c + 1) * BAND_SUB, :]
            hi = (c + 1) * BAND_SUB
            if first or hi >= BAND_WIN:
                lo = max(hi - BAND_WIN, 0)
                kw = kc_ref[0, lo:hi, :]
                vw = vc_ref[0, lo:hi, :]
            else:
                lo = tq - (BAND_WIN - hi)
                kw = jnp.concatenate([kp_ref[0, lo:tq, :], kc_ref[0, 0:hi, :]], axis=0)
                vw = jnp.concatenate([vp_ref[0, lo:tq, :], vc_ref[0, 0:hi, :]], axis=0)
            windows.append((qs, kw, vw))
        for c, o in enumerate(_band_windows(windows, bias_ref, lane)):
            o_ref[0, c * BAND_SUB:(c + 1) * BAND_SUB, :] = o.astype(BF16)

    pl.when(i == 0)(lambda: run(True))
    pl.when(i > 0)(lambda: run(False))


def _band_attn(q, k, v, bias, tq):
    B, T, D = q.shape
    assert tq >= B_WINDOW and T % tq == 0
    cur = lambda b, hp, i: (b, i, hp)
    prev = lambda b, hp, i: (b, jnp.maximum(i - 1, 0), hp)
    blk = lambda m: pl.BlockSpec((1, tq, LANES), m)
    return pl.pallas_call(
        functools.partial(_band_attn_kernel, tq=tq),
        grid=(B, B_HEADS // 2, T // tq),
        in_specs=[blk(cur), blk(prev), blk(cur), blk(prev), blk(cur),
                  pl.BlockSpec((2, BAND_SUB, BAND_WIN), lambda b, hp, i: (hp, 0, 0))],
        out_specs=blk(cur),
        out_shape=jax.ShapeDtypeStruct((B, T, D), BF16),
        compiler_params=_cparams(("parallel", "parallel", "arbitrary")),
        name="band_attn",
    )(q, k, k, v, v, bias)


def _band_step_kernel(q_ref, k_ref, v_ref, bias_ref, o_ref):
    lane = lax.broadcasted_iota(jnp.int32, (BAND_SUB, LANES), 1)
    o, = _band_windows([(q_ref[0], k_ref[0], v_ref[0])], bias_ref, lane)
    o_ref[0] = o.astype(BF16)


def _band_step(q, kwin, vwin, bias):
    B, _, D = q.shape
    return pl.pallas_call(
        _band_step_kernel,
        grid=(B, B_HEADS // 2),
        in_specs=[pl.BlockSpec((1, BAND_SUB, LANES), lambda b, hp: (b, 0, hp)),
                  pl.BlockSpec((1, BAND_WIN, LANES), lambda b, hp: (b, 0, hp)),
                  pl.BlockSpec((1, BAND_WIN, LANES), lambda b, hp: (b, 0, hp)),
                  pl.BlockSpec((2, BAND_SUB, BAND_WIN), lambda b, hp: (hp, 0, 0))],
        out_specs=pl.BlockSpec((1, BAND_SUB, LANES), lambda b, hp: (b, 0, hp)),
        out_shape=jax.ShapeDtypeStruct((B, BAND_SUB, D), BF16),
        compiler_params=_cparams(("parallel", "parallel")),
        name="band_step",
    )(q, kwin, vwin, bias)


def _rope_tables(pos):
    half = ROPE_DIM // 2
    freqs = ROPE_THETA ** (-jnp.arange(half, dtype=F32) / half)
    ang = pos.astype(F32)[:, None] * freqs[None, :]
    c, s = jnp.cos(ang), jnp.sin(ang)
    n = pos.shape[0]
    ones = jnp.ones((n, NOPE_DIM), F32)
    zeros = jnp.zeros((n, NOPE_DIM), F32)
    pad1 = jnp.ones((n, HEAD_PAD - QK_DIM), F32)
    pad0 = jnp.zeros((n, HEAD_PAD - QK_DIM), F32)
    return (_permute_heads(jnp.concatenate([ones, c, c, pad1], axis=1)),
            _permute_heads(jnp.concatenate([zeros, -s, s, pad0], axis=1)))


def _pad_heads(w, nheads, width):
    lead = w.shape[:-1]
    w = w.reshape(lead + (nheads, width))
    w = jnp.pad(w, [(0, 0)] * len(lead) + [(0, 0), (0, HEAD_PAD - width)])
    return w.reshape(lead + (nheads * HEAD_PAD,))


def _head_lanes():
    half = ROPE_DIM // 2
    keep = LANES // 2 - half
    old = np.arange(HEAD_PAD)
    return np.concatenate([old[NOPE_DIM:NOPE_DIM + half], old[:keep],
                           old[NOPE_DIM + half:QK_DIM], old[keep:NOPE_DIM], old[QK_DIM:]])


def _permute_heads(x):
    lead = x.shape[:-1]
    return x.reshape(lead + (-1, HEAD_PAD))[..., _head_lanes()].reshape(x.shape)


def _block(n, pref):
    for t in pref:
        if n % t == 0:
            return t
    return n


def kernel(x_prompt, x_sample, c_prompt, c_sample, cache_a_ckv, cache_a_krope, cache_b_k, cache_b_v, ada_w, ada_b, norm_mix_w, norm_ffn_w, a_w_in, a_q_lora_norm, a_kv_lora_norm, a_w_uq, a_w_ukv, a_q_norm, a_k_norm, a_w_o, kv_ada_w, kv_ada_b, kv_norm_w, b_w_kv, b_k_norm, b_w_q, b_q_norm, b_rel_bias, b_w_o, p_w_q, p_subkeys, p_u, p_v):
    D = D_MODEL
    Bp, Tp, _ = x_prompt.shape
    Bs, Ts, _ = x_sample.shape
    P = cache_a_ckv.shape[2]
    Pb = cache_b_k.shape[1]
    assert Ts == CHUNK and Pb == B_WINDOW and P % CHUNK == 0

    ada_all_w = jnp.concatenate([ada_w[0], ada_w[1], kv_ada_w], axis=1).astype(BF16)
    ada_all_b = jnp.concatenate([ada_b[0], ada_b[1], kv_ada_b])[None, :]
    c_all = jnp.concatenate([c_prompt, c_sample], axis=0)
    mod = _ada(c_all, ada_all_w, ada_all_b)

    w_in = a_w_in[0].astype(BF16)
    qln = a_q_lora_norm[0][None, :]
    kvln = a_kv_lora_norm[0][None, :]
    wuq_pad = _pad_heads(a_w_uq[0], A_HEADS, QK_DIM).astype(BF16)
    wukv = a_w_ukv[0].reshape(KV_LORA, A_HEADS, NOPE_DIM + V_DIM)
    wuk_pad = jnp.pad(wukv[:, :, :NOPE_DIM], ((0, 0), (0, 0), (0, HEAD_PAD - NOPE_DIM))
                      ).reshape(KV_LORA, A_HEADS * HEAD_PAD).astype(BF16)
    wuv_pad = jnp.pad(wukv[:, :, NOPE_DIM:], ((0, 0), (0, 0), (0, HEAD_PAD - V_DIM))
                      ).reshape(KV_LORA, A_HEADS * HEAD_PAD).astype(BF16)
    vone = jnp.asarray(np.tile(np.eye(1, HEAD_PAD, k=V_DIM), (1, A_HEADS)), F32)
    qn_pad = jnp.pad(a_q_norm[0], (0, HEAD_PAD - QK_DIM))[None, :]
    kn_pad = jnp.pad(a_k_norm[0], (0, HEAD_PAD - QK_DIM))[None, :]
    place = jnp.asarray(np.eye(ROPE_DIM, HEAD_PAD, k=NOPE_DIM), BF16)
    wuq_pad, wuk_pad, qn_pad, kn_pad, place = map(_permute_heads, (wuq_pad, wuk_pad, qn_pad, kn_pad, place))
    a_wo = a_w_o[0].astype(BF16)
    wk_b = b_w_kv[:, :D].astype(BF16)
    wv_b = b_w_kv[:, D:].astype(BF16)
    bkn_pair = jnp.tile(b_k_norm, 2)[None, :]
    bqn_pair = jnp.tile(b_q_norm[0], 2)[None, :]
    bwq = b_w_q[0].astype(BF16)
    bwo = b_w_o[0].astype(BF16)
    bias_tile = _band_bias(b_rel_bias[0])

    def peer_weights(layer):
        wq = p_w_q[layer].astype(BF16)
        sk = p_subkeys[layer]
        sk1 = jnp.pad(sk[:, 0], ((0, 0), (0, 0), (0, P_HALF))).astype(BF16)
        sk2 = jnp.pad(sk[:, 1], ((0, 0), (0, 0), (P_HALF, 0))).astype(BF16)
        vt = p_v[layer].astype(BF16).reshape(N_EXPERTS // PEER_EC, PEER_EC, D).transpose(0, 2, 1)
        return wq, sk1, sk2, p_u[layer].astype(BF16), vt

    peer_w = [peer_weights(0), peer_weights(1)]
    norm_mix = norm_mix_w[:, None, :]
    norm_ffn = norm_ffn_w[:, None, :]
    kv_nw = kv_norm_w[None, :]

    def run(x, modp, per_token, pos_q, past):
        B, T, _ = x.shape
        if per_token:
            modv = jnp.repeat(modp, T, axis=0)[None]
            xw = x.reshape(1, B * T, D)
        else:
            modv = modp[:, None, :]
            xw = x
        Bw, Tw, _ = xw.shape
        sl = lambda k: modv[:, :, k * D:(k + 1) * D]
        tb = _block(Tw, (512, 256, 128))
        tbp = _block(Tw, (256, 128))
        tbr = _block(Tw, (512, 256, 128))
        tbe = _block(Tw, (512, 256, 128))

        cos_q, sin_q = _rope_tables(pos_q)
        ckv, kr, q = _mla_pre(xw, sl(0), sl(1), norm_mix[0], w_in, qln, kvln, wuq_pad, qn_pad,
                              cos_q, sin_q, tbp)
        ckv = ckv.reshape(B, T, KV_LORA)
        kr = kr.reshape(B, T, ROPE_DIM)
        q = q.reshape(B, T, A_HEADS * HEAD_PAD)
        if past is None:
            ckv_all, kr_all = ckv, kr
            pos_k = pos_q[:T]
        else:
            ckv_all = jnp.concatenate([past[0], ckv], axis=1)
            kr_all = jnp.concatenate([past[1], kr], axis=1)
            pos_k = jnp.arange(P + T, dtype=jnp.int32)
        S = ckv_all.shape[1]
        cos_k, sin_k = _rope_tables(pos_k)
        tbk = _block(S, (512, 704, 256, 192, 64))
        k, v = _mla_kv(ckv_all, kr_all, wuk_pad, wuv_pad, vone, kn_pad, place, cos_k, sin_k, tbk)
        if past is None:
            ta = _block(T, (1024, 512, 256, 128, 64))
            o = _mla_attn(q, k, v, ta, ta, True)
        else:
            o = _mla_attn(q, k, v, T, tbk, False)
        o = o.reshape(Bw, Tw, A_HEADS * V_DIM)

        wq, sk1, sk2, u, vt = peer_w[0]
        x1, h2, rank, cnt, e2, r = _post_mix(o, xw, sl(2), sl(3), sl(4), norm_ffn[0], a_wo, wq, sk1, sk2, tbr)
        x2 = _peer_dense(h2, u, vt, rank, cnt, e2, r, x1, sl(5), tbe)

        tail = B_WINDOW if past is None else Tw
        kf, vf, kb, vb = _shared_kv(x2, sl(12), sl(13), kv_nw, wk_b, wv_b, bkn_pair, tb, tail)

        qb = _band_pre(x2, sl(6), sl(7), norm_mix[1], bwq, bqn_pair, tb)
        if past is None:
            ob = _band_attn(qb, kb, vb, bias_tile, B_WINDOW)
            new_bk = kf.reshape(B, B_WINDOW, B_HEADS, B_HEAD_DIM)
            new_bv = vf.reshape(B, B_WINDOW, B_HEADS, B_HEAD_DIM)
        else:
            zq = jnp.zeros((B, CHUNK, D), BF16)
            qpad = jnp.concatenate([zq, qb.reshape(B, T, D)], axis=1)
            kwin = jnp.concatenate([zq, past[2].reshape(B, Pb, D).astype(BF16), kb.reshape(B, T, D)], axis=1)
            vwin = jnp.concatenate([zq, past[3].reshape(B, Pb, D).astype(BF16), vb.reshape(B, T, D)], axis=1)
            ob = _band_step(qpad, kwin, vwin, bias_tile)[:, CHUNK:].reshape(Bw, Tw, D)
            new_bk = jnp.concatenate([past[2], kf.reshape(B, T, B_HEADS, B_HEAD_DIM)], axis=1)[:, -Pb:]
            new_bv = jnp.concatenate([past[3], vf.reshape(B, T, B_HEADS, B_HEAD_DIM)], axis=1)[:, -Pb:]

        wq, sk1, sk2, u, vt = peer_w[1]
        x3, h4, rank, cnt, e2, r = _post_mix(ob, x2, sl(8), sl(9), sl(10), norm_ffn[1], bwo, wq, sk1, sk2, tbr)
        y = _peer_dense(h4, u, vt, rank, cnt, e2, r, x3, sl(11), tbe)
        return y.reshape(B, T, D), ckv[None], kr[None], new_bk, new_bv

    pos_p = jnp.arange(Tp, dtype=jnp.int32)
    pos_s = jnp.tile(P + jnp.arange(Ts, dtype=jnp.int32), Bs)
    y_p, p_ckv, p_kr, p_bk, p_bv = run(x_prompt, mod[:Bp], False, pos_p, None)
    y_s, s_ckv, s_kr, s_bk, s_bv = run(x_sample, mod[Bp:], True, pos_s,
                                       (cache_a_ckv[0], cache_a_krope[0], cache_b_k, cache_b_v))
    return (y_p, y_s, p_ckv, p_kr, p_bk, p_bv, s_ckv, s_kr, s_bk, s_bv)
```
